```python
import math
import jax, jax.numpy as jnp
from jax import lax
import numpy as np

D_MODEL = 1024
BATCH = 8
SEQ = 2048
DEPTH = 4
DEC_BATCH = 128
DEC_SEQ = 4
PAST_LEN = 16384
PAGE_SIZE = 128

CHUNK = 128
A_HEADS = 4
D_A = D_MODEL // 2
A_HEAD_DIM = D_A // A_HEADS
POOL_WINDOWS = (2, 4, 8, 16)
B_GROUPS = len(POOL_WINDOWS)
D_B = D_MODEL // 2
B_GROUP_DIM = D_B // B_GROUPS
POOL_HIST = max(POOL_WINDOWS) - 1
D_C = D_MODEL // 2
CONV_C = 31
D_D = D_MODEL // 2
CONV_D = 3
N_MEM = 256
MEM_HEADS = 4
MEM_HEAD_DIM = D_MODEL // MEM_HEADS
N_GROUPS = 4
EXPERTS_PER_GROUP = 4
N_EXPERTS = N_GROUPS * EXPERTS_PER_GROUP
TOP_K_INNER = 2
D_EXPERT = 512
DN_ALPHA = (2 * DEPTH) ** 0.25
DN_BETA = (8 * DEPTH) ** -0.25
N_EVEN = (DEPTH + 1) // 2
N_ODD = DEPTH // 2
LN_EPS = 1e-5

kernel_name = 'hybrid_gmlp_pool_conformer_shortconv_hmoe_step'


def layer_norm(x, g, b):
    xf = x.astype(jnp.float32)
    mu = jnp.mean(xf, axis=-1, keepdims=True)
    var = jnp.mean(jnp.square(xf - mu), axis=-1, keepdims=True)
    y = (xf - mu) * lax.rsqrt(var + LN_EPS) * g.astype(jnp.float32) + b.astype(jnp.float32)
    return y.astype(x.dtype)


def spatial_gate(u, v, w_s, b_s):
    n, t, _ = v.shape
    n_chunks = -(-t // CHUNK)
    pad = n_chunks * CHUNK - t
    vp = jnp.pad(v, ((0, 0), (0, pad), (0, 0))).reshape(n, n_chunks, CHUNK, A_HEADS, A_HEAD_DIM)
    causal = jnp.tril(jnp.ones((CHUNK, CHUNK), dtype=bool))
    w = jnp.where(causal[None], w_s, 0).astype(v.dtype)
    mixed = jnp.einsum('hts,ncshd->ncthd', w, vp) + b_s.T.astype(v.dtype)[None, None, :, :, None]
    mixed = mixed.reshape(n, n_chunks * CHUNK, D_A)[:, :t]
    return u * mixed


def multi_scale_pool(z, hist, positions):
    zp = jnp.concatenate([hist, z], axis=1)
    t = z.shape[1]
    cs = jnp.pad(jnp.cumsum(zp.astype(jnp.float32), axis=1), ((0, 0), (1, 0), (0, 0)))
    end = cs[:, POOL_HIST + 1:]
    outs = []
    for gi, w in enumerate(POOL_WINDOWS):
        lo, hi = gi * B_GROUP_DIM, (gi + 1) * B_GROUP_DIM
        start = cs[:, POOL_HIST + 1 - w: POOL_HIST + 1 - w + t, lo:hi]
        cnt = jnp.minimum(positions + 1, w).astype(jnp.float32)[None, :, None]
        outs.append((end[:, :, lo:hi] - start) / cnt)
    pooled = jnp.concatenate(outs, axis=-1).astype(z.dtype)
    return pooled - z, zp[:, -POOL_HIST:]


def causal_dwconv(z, hist, w):
    zp = jnp.concatenate([hist, z], axis=1)
    k, c = w.shape
    y = lax.conv_general_dilated(zp, w[:, None, :].astype(zp.dtype), window_strides=(1,), padding='VALID',
                                 dimension_numbers=('NWC', 'WIO', 'NWC'), feature_group_count=c)
    return y, zp[:, -(k - 1):]


def mixer_ab(x, pool_hist, positions, w_in, ln_v_g, ln_v_b, w_s, b_s, w_pool, pool_scale, w_out):
    h = jnp.einsum('ntd,de->nte', x, w_in)
    ua = jax.nn.gelu(h[..., :2 * D_A])
    u = ua[..., :D_A]
    v = layer_norm(ua[..., D_A:], ln_v_g, ln_v_b)
    z = h[..., 2 * D_A:]
    a_out = spatial_gate(u, v, w_s, b_s)
    pooled, new_hist = multi_scale_pool(z, pool_hist, positions)
    n, t, _ = z.shape
    b_out = jnp.einsum('ntgc,gcd->ntgd', pooled.reshape(n, t, B_GROUPS, B_GROUP_DIM), w_pool).reshape(n, t, D_B) * pool_scale
    y = jnp.einsum('nte,ed->ntd', jnp.concatenate([a_out, b_out], axis=-1), w_out)
    return y, v, new_hist


def mixer_cd(x, hist_c, hist_d, w_in, conv_c_w, conv_c_b, ln_c_g, ln_c_b, conv_d_w, w_out):
    h = jnp.einsum('ntd,de->nte', x, w_in)
    glu = h[..., :D_C] * jax.nn.sigmoid(h[..., D_C:2 * D_C])
    cc, new_hist_c = causal_dwconv(glu, hist_c, conv_c_w)
    c_out = jax.nn.silu(layer_norm(cc + conv_c_b, ln_c_g, ln_c_b))
    o = 2 * D_C
    gate_b = h[..., o:o + D_D]
    gate_c = h[..., o + D_D:o + 2 * D_D]
    xt = h[..., o + 2 * D_D:]
    cd, new_hist_d = causal_dwconv(gate_c * xt, hist_d, conv_d_w)
    d_out = gate_b * cd
    y = jnp.einsum('nte,ed->ntd', jnp.concatenate([c_out, d_out], axis=-1), w_out)
    return y, new_hist_c, new_hist_d


def mem_project(mem, w):
    n = mem.shape[0]
    return jnp.einsum('nmd,de->nme', mem, w).reshape(n, N_MEM, MEM_HEADS, MEM_HEAD_DIM)


def mem_attend(x, k, v, w_q, w_o):
    n, t, _ = x.shape
    q = jnp.einsum('ntd,de->nte', x, w_q).reshape(n, t, MEM_HEADS, MEM_HEAD_DIM)
    s = jnp.einsum('nthc,nmhc->nhtm', q, k).astype(jnp.float32) * (MEM_HEAD_DIM ** -0.5)
    p = jax.nn.softmax(s, axis=-1).astype(x.dtype)
    o = jnp.einsum('nhtm,nmhc->nthc', p, v).reshape(n, t, D_MODEL)
    return jnp.einsum('nte,ed->ntd', o, w_o)


def hier_moe(x, w_group, b_group, w_router, b_router, w_gate, w_up, w_down):
    n, t, d = x.shape
    xf = x.reshape(n * t, d)
    g_logits = jnp.einsum('md,dg->mg', xf, w_group).astype(jnp.float32) + b_group.astype(jnp.float32)
    g_prob = jax.nn.softmax(g_logits, axis=-1)
    g_sel = jnp.argmax(g_logits, axis=-1)
    g_w = jnp.take_along_axis(g_prob, g_sel[:, None], axis=-1)
    e_logits = (jnp.einsum('md,de->me', xf, w_router).astype(jnp.float32) + b_router.astype(jnp.float32))
    e_logits = e_logits.reshape(-1, N_GROUPS, EXPERTS_PER_GROUP)
    e_logits = jnp.take_along_axis(e_logits, g_sel[:, None, None], axis=1)[:, 0]
    top_v, top_i = lax.top_k(e_logits, TOP_K_INNER)
    top_w = jax.nn.softmax(top_v, axis=-1) * g_w
    expert_id = g_sel[:, None] * EXPERTS_PER_GROUP + top_i
    gates = jnp.sum(jax.nn.one_hot(expert_id, N_EXPERTS, dtype=jnp.float32) * top_w[..., None], axis=1)
    hg = jnp.einsum('md,edf->mef', xf, w_gate)
    hu = jnp.einsum('md,edf->mef', xf, w_up)
    hid = jax.nn.silu(hg) * hu * gates.astype(x.dtype)[..., None]
    y = jnp.einsum('mef,efd->md', hid, w_down)
    return y.reshape(n, t, d)


def setup_inputs(seed: int = 0) -> dict:
    key = jax.random.key(seed)
    ks = iter(jax.random.split(key, 48))

    def nrm(shape, scale):
        return jax.random.normal(next(ks), shape, jnp.float32) * scale

    def gain(shape):
        return 1.0 + nrm(shape, 0.1)

    dm = D_MODEL ** -0.5
    return {
        'x_prompt': nrm((BATCH, SEQ, D_MODEL), 1.0),
        'x_sample': nrm((DEC_BATCH, DEC_SEQ, D_MODEL), 1.0),
        'cache_mem_k': nrm((DEPTH, DEC_BATCH, N_MEM, MEM_HEADS, MEM_HEAD_DIM), 1.0),
        'cache_mem_v': nrm((DEPTH, DEC_BATCH, N_MEM, MEM_HEADS, MEM_HEAD_DIM), 1.0),
        'state_pool': nrm((N_EVEN, DEC_BATCH, POOL_HIST, D_B), 1.0),
        'state_conv_c': nrm((N_ODD, DEC_BATCH, CONV_C - 1, D_C), 0.5),
        'state_conv_d': nrm((N_ODD, DEC_BATCH, CONV_D - 1, D_D), 1.0),
        'mem_prompt': nrm((BATCH, N_MEM, D_MODEL), 1.0),
        'w_in_ab': nrm((N_EVEN, D_MODEL, 2 * D_A + D_B), dm),
        'ln_v_g': gain((N_EVEN, D_A)),
        'ln_v_b': nrm((N_EVEN, D_A), 0.02),
        'w_spatial': nrm((N_EVEN, A_HEADS, CHUNK, CHUNK), CHUNK ** -0.5),
        'b_spatial': gain((N_EVEN, A_HEADS, CHUNK)),
        'w_pool': nrm((N_EVEN, B_GROUPS, B_GROUP_DIM, B_GROUP_DIM), B_GROUP_DIM ** -0.5),
        'pool_scale': gain((N_EVEN, D_B)),
        'w_out_ab': nrm((N_EVEN, D_A + D_B, D_MODEL), (D_A + D_B) ** -0.5 * DN_BETA),
        'w_in_cd': nrm((N_ODD, D_MODEL, 2 * D_C + 3 * D_D), dm),
        'conv_c_w': nrm((N_ODD, CONV_C, D_C), CONV_C ** -0.5),
        'conv_c_b': nrm((N_ODD, D_C), 0.02),
        'ln_c_g': gain((N_ODD, D_C)),
        'ln_c_b': nrm((N_ODD, D_C), 0.02),
        'conv_d_w': nrm((N_ODD, CONV_D, D_D), CONV_D ** -0.5),
        'w_out_cd': nrm((N_ODD, D_C + D_D, D_MODEL), (D_C + D_D) ** -0.5 * DN_BETA),
        'w_q': nrm((DEPTH, D_MODEL, D_MODEL), dm),
        'w_k': nrm((DEPTH, D_MODEL, D_MODEL), dm),
        'w_v': nrm((DEPTH, D_MODEL, D_MODEL), dm * DN_BETA),
        'w_o': nrm((DEPTH, D_MODEL, D_MODEL), dm * DN_BETA),
        'w_group': nrm((DEPTH, D_MODEL, N_GROUPS), dm),
        'b_group': nrm((DEPTH, N_GROUPS), 0.01),
        'w_router': nrm((DEPTH, D_MODEL, N_EXPERTS), dm),
        'b_router': nrm((DEPTH, N_EXPERTS), 0.01),
        'w_gate': nrm((DEPTH, N_EXPERTS, D_MODEL, D_EXPERT), dm),
        'w_up': nrm((DEPTH, N_EXPERTS, D_MODEL, D_EXPERT), dm),
        'w_down': nrm((DEPTH, N_EXPERTS, D_EXPERT, D_MODEL), D_EXPERT ** -0.5 * DN_BETA),
        'ln_g': gain((DEPTH, 3, D_MODEL)),
        'ln_b': nrm((DEPTH, 3, D_MODEL), 0.02),
    }


def reference(x_prompt, x_sample, cache_mem_k, cache_mem_v, state_pool, state_conv_c, state_conv_d, mem_prompt,
              w_in_ab, ln_v_g, ln_v_b, w_spatial, b_spatial, w_pool, pool_scale, w_out_ab,
              w_in_cd, conv_c_w, conv_c_b, ln_c_g, ln_c_b, conv_d_w, w_out_cd,
              w_q, w_k, w_v, w_o, w_group, b_group, w_router, b_router, w_gate, w_up, w_down, ln_g, ln_b):
    n_p, t_p = x_prompt.shape[0], x_prompt.shape[1]
    t_s = x_sample.shape[1]
    pos_p = jnp.arange(t_p)
    pos_s = PAST_LEN + jnp.arange(t_s)
    hp, hs = x_prompt, x_sample
    mem_k_p, mem_v_p = [], []
    pool_p, pool_s, chunk_v_s = [], [], []
    conv_c_p, conv_c_s, conv_d_p, conv_d_s = [], [], [], []
    for l in range(DEPTH):
        if l % 2 == 0:
            i = l // 2
            wts = (w_in_ab[i], ln_v_g[i], ln_v_b[i], w_spatial[i], b_spatial[i], w_pool[i], pool_scale[i], w_out_ab[i])
            zero_pool = jnp.zeros((n_p, POOL_HIST, D_B), hp.dtype)
            mp, _, hist_p = mixer_ab(hp, zero_pool, pos_p, *wts)
            ms, v_rows, hist_s = mixer_ab(hs, state_pool[i], pos_s, *wts)
            pool_p.append(hist_p)
            pool_s.append(hist_s)
            chunk_v_s.append(v_rows)
        else:
            i = l // 2
            wts = (w_in_cd[i], conv_c_w[i], conv_c_b[i], ln_c_g[i], ln_c_b[i], conv_d_w[i], w_out_cd[i])
            zero_c = jnp.zeros((n_p, CONV_C - 1, D_C), hp.dtype)
            zero_d = jnp.zeros((n_p, CONV_D - 1, D_D), hp.dtype)
            mp, hc_p, hd_p = mixer_cd(hp, zero_c, zero_d, *wts)
            ms, hc_s, hd_s = mixer_cd(hs, state_conv_c[i], state_conv_d[i], *wts)
            conv_c_p.append(hc_p)
            conv_c_s.append(hc_s)
            conv_d_p.append(hd_p)
            conv_d_s.append(hd_s)
        hp = layer_norm(DN_ALPHA * hp + mp, ln_g[l, 0], ln_b[l, 0])
        hs = layer_norm(DN_ALPHA * hs + ms, ln_g[l, 0], ln_b[l, 0])
        kp = mem_project(mem_prompt, w_k[l])
        vp = mem_project(mem_prompt, w_v[l])
        mem_k_p.append(kp)
        mem_v_p.append(vp)
        hp = layer_norm(DN_ALPHA * hp + mem_attend(hp, kp, vp, w_q[l], w_o[l]), ln_g[l, 1], ln_b[l, 1])
        hs = layer_norm(DN_ALPHA * hs + mem_attend(hs, cache_mem_k[l], cache_mem_v[l], w_q[l], w_o[l]), ln_g[l, 1], ln_b[l, 1])
        moe_w = (w_group[l], b_group[l], w_router[l], b_router[l], w_gate[l], w_up[l], w_down[l])
        hp = layer_norm(DN_ALPHA * hp + hier_moe(hp, *moe_w), ln_g[l, 2], ln_b[l, 2])
        hs = layer_norm(DN_ALPHA * hs + hier_moe(hs, *moe_w), ln_g[l, 2], ln_b[l, 2])
    return (hp, hs, jnp.stack(mem_k_p), jnp.stack(mem_v_p), jnp.stack(pool_p), jnp.stack(conv_c_p), jnp.stack(conv_d_p),
            jnp.stack(chunk_v_s), jnp.stack(pool_s), jnp.stack(conv_c_s), jnp.stack(conv_d_s))
```

```python
import functools

import jax
import jax.numpy as jnp
from jax import lax
from jax.experimental import pallas as pl
from jax.experimental.pallas import tpu as pltpu

D_MODEL = 1024
BATCH = 8
SEQ = 2048
DEPTH = 4
DEC_BATCH = 128
DEC_SEQ = 4
PAST_LEN = 16384

CHUNK = 128
A_HEADS = 4
D_A = D_MODEL // 2
A_HEAD_DIM = D_A // A_HEADS
POOL_WINDOWS = (2, 4, 8, 16)
B_GROUPS = len(POOL_WINDOWS)
D_B = D_MODEL // 2
B_GROUP_DIM = D_B // B_GROUPS
POOL_HIST = max(POOL_WINDOWS) - 1
D_C = D_MODEL // 2
CONV_C = 31
D_D = D_MODEL // 2
CONV_D = 3
N_MEM = 256
MEM_HEADS = 4
MEM_HEAD_DIM = D_MODEL // MEM_HEADS
N_GROUPS = 4
EXPERTS_PER_GROUP = 4
N_EXPERTS = N_GROUPS * EXPERTS_PER_GROUP
D_EXPERT = 512
DN_ALPHA = (2 * DEPTH) ** 0.25
LN_EPS = 1e-5

LANES = 128
POOL_HIST_PAD = 16
CONV_C_HIST_PAD = 32
CONV_D_HIST_PAD = 8
ROUTER_LANES = 128
ROUTER_MASKED = -1e30
VMEM_LIMIT_BYTES = 52 * 1024 * 1024

F32 = jnp.float32
BF16 = jnp.bfloat16


def _layer_norm(x, g, b):
    mu = jnp.mean(x, axis=-1, keepdims=True)
    xc = x - mu
    var = jnp.mean(xc * xc, axis=-1, keepdims=True)
    return xc * lax.rsqrt(var + LN_EPS) * g + b


def _dot(a, b):
    return jnp.dot(a.astype(BF16), b.astype(BF16), preferred_element_type=F32)


def _params(*semantics):
    return pltpu.CompilerParams(dimension_semantics=semantics, vmem_limit_bytes=VMEM_LIMIT_BYTES)


def _const_spec(shape):
    nd = len(shape)
    return pl.BlockSpec(shape, lambda *_: (0,) * nd)


def _mixer_ab_kernel(x_ref, hist_ref, w_in_ref, lnv_g_ref, lnv_b_ref, ws_ref, bs_ref, wpool_ref,
                     pscale_ref, w_out_ref, lng_ref, lnb_ref,
                     h_out_ref, v_out_ref, hist_out_ref, zbuf, *, seqs, tt, pos0):
    t_idx = pl.program_id(1)
    rows = tt * seqs
    hp = POOL_HIST_PAD

    @pl.when(t_idx == 0)
    def _():
        zbuf[pl.ds((hp - POOL_HIST) * seqs, POOL_HIST * seqs), :] = hist_ref[0]

    x = x_ref[0]
    h = _dot(x, w_in_ref[...])
    ua = jax.nn.gelu(h[:, :2 * D_A])
    u = ua[:, :D_A]
    v = _layer_norm(ua[:, D_A:], lnv_g_ref[...], lnv_b_ref[...])
    z = h[:, 2 * D_A:]
    v_out_ref[0] = v
    zbuf[pl.ds(hp * seqs, rows), :] = z

    if seqs == 1:
        tri = (lax.broadcasted_iota(jnp.int32, (CHUNK, CHUNK), 0)
               >= lax.broadcasted_iota(jnp.int32, (CHUNK, CHUNK), 1))
        w_heads = [jnp.where(tri, ws_ref[hh], 0.0).astype(BF16) for hh in range(A_HEADS)]
        vb = v.astype(BF16)
        chunks = []
        for c in range(tt // CHUNK):
            heads = []
            for hh in range(A_HEADS):
                vc = vb[c * CHUNK:(c + 1) * CHUNK, hh * A_HEAD_DIM:(hh + 1) * A_HEAD_DIM]
                heads.append(jnp.dot(w_heads[hh], vc, preferred_element_type=F32))
            chunks.append(jnp.concatenate(heads, axis=-1) + bs_ref[...])
        mixed = jnp.concatenate(chunks, axis=0)
    else:
        parts = []
        for t in range(tt):
            acc = bs_ref[t:t + 1, :]
            for s in range(t + 1):
                acc = acc + ws_ref[t * tt + s:t * tt + s + 1, :] * v[s * seqs:(s + 1) * seqs, :]
            parts.append(acc)
        mixed = jnp.concatenate(parts, axis=0)
    a_out = u * mixed

    outs = []
    for gi, w in enumerate(POOL_WINDOWS):
        lo, hi = gi * B_GROUP_DIM, (gi + 1) * B_GROUP_DIM
        acc = zbuf[pl.ds(hp * seqs, rows), lo:hi]
        for j in range(1, w):
            acc = acc + zbuf[pl.ds((hp - j) * seqs, rows), lo:hi]
        if pos0 + 1 >= w:
            cnt = float(w)
        else:
            assert seqs == 1
            pos = pos0 + t_idx * tt + lax.broadcasted_iota(jnp.int32, (rows, 1), 0)
            cnt = jnp.minimum(pos + 1, w).astype(F32)
        pooled = acc / cnt - z[:, lo:hi]
        outs.append(_dot(pooled, wpool_ref[gi]))
    b_out = jnp.concatenate(outs, axis=-1) * pscale_ref[...]

    y = _dot(jnp.concatenate([a_out, b_out], axis=-1), w_out_ref[...])
    h_out_ref[0] = _layer_norm(DN_ALPHA * x + y, lng_ref[...], lnb_ref[...])

    hist_out_ref[0] = zbuf[pl.ds((tt + hp - POOL_HIST) * seqs, POOL_HIST * seqs), :]
    zbuf[pl.ds(0, hp * seqs), :] = zbuf[pl.ds(tt * seqs, hp * seqs), :]


def _mixer_ab(x, hist, w_in, lnv_g, lnv_b, ws, bs, wpool, pscale, w_out, lng, lnb, *, seqs, tt, pos0):
    nb, total_rows, _ = x.shape
    rows = tt * seqs
    n_t = total_rows // rows
    kern = functools.partial(_mixer_ab_kernel, seqs=seqs, tt=tt, pos0=pos0)
    return pl.pallas_call(
        kern,
        grid=(nb, n_t),
        in_specs=[
            pl.BlockSpec((1, rows, D_MODEL), lambda n, t: (n, t, 0)),
            pl.BlockSpec((1, POOL_HIST * seqs, D_B), lambda n, t: (n, 0, 0)),
            _const_spec(w_in.shape), _const_spec(lnv_g.shape), _const_spec(lnv_b.shape),
            _const_spec(ws.shape), _const_spec(bs.shape), _const_spec(wpool.shape),
            _const_spec(pscale.shape), _const_spec(w_out.shape), _const_spec(lng.shape),
            _const_spec(lnb.shape),
        ],
        out_specs=[
            pl.BlockSpec((1, rows, D_MODEL), lambda n, t: (n, t, 0)),
            pl.BlockSpec((1, rows, D_A), lambda n, t: (n, t, 0)),
            pl.BlockSpec((1, POOL_HIST * seqs, D_B), lambda n, t: (n, 0, 0)),
        ],
        out_shape=[
            jax.ShapeDtypeStruct(x.shape, F32),
            jax.ShapeDtypeStruct((nb, total_rows, D_A), F32),
            jax.ShapeDtypeStruct((nb, POOL_HIST * seqs, D_B), F32),
        ],
        scratch_shapes=[pltpu.VMEM(((POOL_HIST_PAD + tt) * seqs, D_B), F32)],
        compiler_params=_params("parallel", "arbitrary"),
        name="mixer_ab",
    )(x, hist, w_in, lnv_g, lnv_b, ws, bs, wpool, pscale, w_out, lng, lnb)


def _dwconv(buf, w_ref, out, *, taps, hist_pad, seqs, rows, chunk):
    base = (hist_pad - (taps - 1)) * seqs

    def body(i, carry):
        r0 = i * chunk if isinstance(i, int) else pl.multiple_of(i * chunk, chunk)
        acc = w_ref[0:1, :] * buf[pl.ds(base + r0, chunk), :]
        for k in range(1, taps):
            acc = acc + w_ref[k:k + 1, :] * buf[pl.ds(base + k * seqs + r0, chunk), :]
        out[pl.ds(r0, chunk), :] = acc
        return carry

    if seqs % 8 == 0:
        lax.fori_loop(0, rows // chunk, body, 0)
    else:
        for i in range(rows // chunk):
            body(i, 0)


def _mixer_cd_kernel(x_ref, hist_c_ref, hist_d_ref, w_in_ref, ccw_ref, ccb_ref, lncg_ref, lncb_ref,
                     cdw_ref, w_out_ref, lng_ref, lnb_ref,
                     h_out_ref, hist_c_out_ref, hist_d_out_ref, cbuf, dbuf, cc_scr, cd_scr,
                     *, seqs, tt):
    t_idx = pl.program_id(1)
    rows = tt * seqs
    hc, hd = CONV_C_HIST_PAD, CONV_D_HIST_PAD
    nc, nd = CONV_C - 1, CONV_D - 1

    @pl.when(t_idx == 0)
    def _():
        cbuf[pl.ds((hc - nc) * seqs, nc * seqs), :] = hist_c_ref[0]
        dbuf[pl.ds((hd - nd) * seqs, nd * seqs), :] = hist_d_ref[0]

    x = x_ref[0]
    h = _dot(x, w_in_ref[...])
    glu = h[:, :D_C] * jax.nn.sigmoid(h[:, D_C:2 * D_C])
    o = 2 * D_C
    gate_b = h[:, o:o + D_D]
    gx = h[:, o + D_D:o + 2 * D_D] * h[:, o + 2 * D_D:]
    cbuf[pl.ds(hc * seqs, rows), :] = glu
    dbuf[pl.ds(hd * seqs, rows), :] = gx

    chunk = min(rows, 32)
    _dwconv(cbuf, ccw_ref, cc_scr, taps=CONV_C, hist_pad=hc, seqs=seqs, rows=rows, chunk=chunk)
    _dwconv(dbuf, cdw_ref, cd_scr, taps=CONV_D, hist_pad=hd, seqs=seqs, rows=rows, chunk=chunk)

    c_out = jax.nn.silu(_layer_norm(cc_scr[...] + ccb_ref[...], lncg_ref[...], lncb_ref[...]))
    d_out = gate_b * cd_scr[...]
    y = _dot(jnp.concatenate([c_out, d_out], axis=-1), w_out_ref[...])
    h_out_ref[0] = _layer_norm(DN_ALPHA * x + y, lng_ref[...], lnb_ref[...])

    hist_c_out_ref[0] = cbuf[pl.ds((tt + hc - nc) * seqs, nc * seqs), :]
    hist_d_out_ref[0] = dbuf[pl.ds((tt + hd - nd) * seqs, nd * seqs), :]
    cbuf[pl.ds(0, hc * seqs), :] = cbuf[pl.ds(tt * seqs, hc * seqs), :]
    dbuf[pl.ds(0, hd * seqs), :] = dbuf[pl.ds(tt * seqs, hd * seqs), :]


def _mixer_cd(x, hist_c, hist_d, w_in, ccw, ccb, lncg, lncb, cdw, w_out, lng, lnb, *, seqs, tt):
    nb, total_rows, _ = x.shape
    rows = tt * seqs
    n_t = total_rows // rows
    nc, nd = CONV_C - 1, CONV_D - 1
    kern = functools.partial(_mixer_cd_kernel, seqs=seqs, tt=tt)
    return pl.pallas_call(
        kern,
        grid=(nb, n_t),
        in_specs=[
            pl.BlockSpec((1, rows, D_MODEL), lambda n, t: (n, t, 0)),
            pl.BlockSpec((1, nc * seqs, D_C), lambda n, t: (n, 0, 0)),
            pl.BlockSpec((1, nd * seqs, D_D), lambda n, t: (n, 0, 0)),
            _const_spec(w_in.shape), _const_spec(ccw.shape), _const_spec(ccb.shape),
            _const_spec(lncg.shape), _const_spec(lncb.shape), _const_spec(cdw.shape),
            _const_spec(w_out.shape), _const_spec(lng.shape), _const_spec(lnb.shape),
        ],
        out_specs=[
            pl.BlockSpec((1, rows, D_MODEL), lambda n, t: (n, t, 0)),
            pl.BlockSpec((1, nc * seqs, D_C), lambda n, t: (n, 0, 0)),
            pl.BlockSpec((1, nd * seqs, D_D), lambda n, t: (n, 0, 0)),
        ],
        out_shape=[
            jax.ShapeDtypeStruct(x.shape, F32),
            jax.ShapeDtypeStruct((nb, nc * seqs, D_C), F32),
            jax.ShapeDtypeStruct((nb, nd * seqs, D_D), F32),
        ],
        scratch_shapes=[
            pltpu.VMEM(((CONV_C_HIST_PAD + tt) * seqs, D_C), F32),
            pltpu.VMEM(((CONV_D_HIST_PAD + tt) * seqs, D_D), F32),
            pltpu.VMEM((rows, D_C), F32),
            pltpu.VMEM((rows, D_D), F32),
        ],
        compiler_params=_params("parallel", "arbitrary"),
        name="mixer_cd",
    )(x, hist_c, hist_d, w_in, ccw, ccb, lncg, lncb, cdw, w_out, lng, lnb)


def _kv_proj_kernel(mem_ref, wk_ref, wv_ref, k_ref, v_ref):
    m = mem_ref[...].astype(BF16)
    k_ref[...] = jnp.dot(m, wk_ref[...], preferred_element_type=F32)
    v_ref[...] = jnp.dot(m, wv_ref[...], preferred_element_type=F32)


def _kv_proj(mem, wk, wv, *, tm):
    m = mem.shape[0]
    return pl.pallas_call(
        _kv_proj_kernel,
        grid=(m // tm,),
        in_specs=[pl.BlockSpec((tm, D_MODEL), lambda i: (i, 0)),
                  _const_spec(wk.shape), _const_spec(wv.shape)],
        out_specs=[pl.BlockSpec((tm, D_MODEL), lambda i: (i, 0))] * 2,
        out_shape=[jax.ShapeDtypeStruct((m, D_MODEL), F32)] * 2,
        compiler_params=_params("parallel"),
        name="kv_proj",
    )(mem, wk, wv)


def _attend(q, k, v):
    scale = MEM_HEAD_DIM ** -0.5
    qb = q.astype(BF16)
    outs = []
    for hh in range(MEM_HEADS):
        lo, hi = hh * MEM_HEAD_DIM, (hh + 1) * MEM_HEAD_DIM
        s = lax.dot_general(qb[:, lo:hi], k[:, lo:hi], (((1,), (1,)), ((), ())),
                            preferred_element_type=F32) * scale
        s = s - jnp.max(s, axis=-1, keepdims=True)
        p = jnp.exp(s)
        p = p / jnp.sum(p, axis=-1, keepdims=True)
        outs.append(jnp.dot(p.astype(BF16), v[:, lo:hi], preferred_element_type=F32))
    return jnp.concatenate(outs, axis=-1)


def _attn_prompt_kernel(x_ref, k_ref, v_ref, wq_ref, wo_ref, lng_ref, lnb_ref, o_ref):
    x = x_ref[0]
    q = _dot(x, wq_ref[...])
    o = _attend(q, k_ref[0].astype(BF16), v_ref[0].astype(BF16))
    y = _dot(o, wo_ref[...])
    o_ref[0] = _layer_norm(DN_ALPHA * x + y, lng_ref[...], lnb_ref[...])


def _attn_prompt(x, k, v, wq, wo, lng, lnb, *, tq):
    nb, t, _ = x.shape
    return pl.pallas_call(
        _attn_prompt_kernel,
        grid=(nb, t // tq),
        in_specs=[
            pl.BlockSpec((1, tq, D_MODEL), lambda n, i: (n, i, 0)),
            pl.BlockSpec((1, N_MEM, D_MODEL), lambda n, i: (n, 0, 0)),
            pl.BlockSpec((1, N_MEM, D_MODEL), lambda n, i: (n, 0, 0)),
            _const_spec(wq.shape), _const_spec(wo.shape), _const_spec(lng.shape), _const_spec(lnb.shape),
        ],
        out_specs=pl.BlockSpec((1, tq, D_MODEL), lambda n, i: (n, i, 0)),
        out_shape=jax.ShapeDtypeStruct(x.shape, F32),
        compiler_params=_params("parallel", "arbitrary"),
        name="attn_prompt",
    )(x, k, v, wq, wo, lng, lnb)


def _attn_sample_kernel(x_ref, k_ref, v_ref, wq_ref, wo_ref, lng_ref, lnb_ref, o_ref, q_scr, a_scr,
                        *, seq_blk, n_blk):
    i = pl.program_id(0)

    @pl.when(i == 0)
    def _():
        q_scr[...] = _dot(x_ref[...], wq_ref[...])

    def body(j, carry):
        n = i * seq_blk + j
        q = jnp.concatenate([q_scr[pl.ds(t * DEC_BATCH + n, 1), :] for t in range(DEC_SEQ)], axis=0)
        o = _attend(q, k_ref[j].astype(BF16), v_ref[j].astype(BF16))
        for t in range(DEC_SEQ):
            a_scr[pl.ds(t * DEC_BATCH + n, 1), :] = o[t:t + 1, :]
        return carry

    lax.fori_loop(0, seq_blk, body, 0)

    @pl.when(i == n_blk - 1)
    def _():
        x = x_ref[...]
        y = _dot(a_scr[...], wo_ref[...])
        o_ref[...] = _layer_norm(DN_ALPHA * x + y, lng_ref[...], lnb_ref[...])


def _attn_sample(x, k, v, wq, wo, lng, lnb, *, seq_blk):
    m = x.shape[0]
    n_blk = DEC_BATCH // seq_blk
    kern = functools.partial(_attn_sample_kernel, seq_blk=seq_blk, n_blk=n_blk)
    return pl.pallas_call(
        kern,
        grid=(n_blk,),
        in_specs=[
            _const_spec(x.shape),
            pl.BlockSpec((seq_blk, N_MEM, D_MODEL), lambda i: (i, 0, 0)),
            pl.BlockSpec((seq_blk, N_MEM, D_MODEL), lambda i: (i, 0, 0)),
            _const_spec(wq.shape), _const_spec(wo.shape), _const_spec(lng.shape), _const_spec(lnb.shape),
        ],
        out_specs=_const_spec(x.shape),
        out_shape=jax.ShapeDtypeStruct(x.shape, F32),
        scratch_shapes=[pltpu.VMEM((m, D_MODEL), F32), pltpu.VMEM((m, D_MODEL), F32)],
        compiler_params=_params("arbitrary"),
        name="attn_sample",
    )(x, k, v, wq, wo, lng, lnb)


def _route(logits):
    lane = lax.broadcasted_iota(jnp.int32, logits.shape, 1)
    lane_f = lane.astype(F32)
    neg = ROUTER_MASKED
    is_grp = lane < N_GROUPS
    gl = jnp.where(is_grp, logits, neg)
    gmax = jnp.max(gl, axis=-1, keepdims=True)
    gsel = jnp.min(jnp.where(gl == gmax, lane_f, float(ROUTER_LANES)), axis=-1, keepdims=True)
    g_w = 1.0 / jnp.sum(jnp.exp(gl - gmax), axis=-1, keepdims=True)
    e_idx = lane - N_GROUPS
    e_grp = lax.shift_right_arithmetic(e_idx, 2).astype(F32)
    in_grp = (e_idx >= 0) & (e_idx < N_EXPERTS) & (e_grp == gsel)
    el = jnp.where(in_grp, logits, neg)
    m1 = jnp.max(el, axis=-1, keepdims=True)
    i1 = jnp.min(jnp.where(el == m1, lane_f, float(ROUTER_LANES)), axis=-1, keepdims=True)
    el2 = jnp.where(lane_f == i1, neg, el)
    m2 = jnp.max(el2, axis=-1, keepdims=True)
    i2 = jnp.min(jnp.where(el2 == m2, lane_f, float(ROUTER_LANES)), axis=-1, keepdims=True)
    e2 = jnp.exp(m2 - m1)
    den = 1.0 + e2
    w1 = (1.0 / den) * g_w
    w2 = (e2 / den) * g_w
    return jnp.where(lane_f == i1, w1, 0.0) + jnp.where(lane_f == i2, w2, 0.0)


def _moe_dense_kernel(x_ref, wr_ref, br_ref, wg_ref, wu_ref, wd_ref, lng_ref, lnb_ref, o_ref,
                      gates_scr, acc_scr):
    e = pl.program_id(1)

    @pl.when(e == 0)
    def _():
        logits = _dot(x_ref[...], wr_ref[...]) + br_ref[...]
        gates_scr[...] = _route(logits)
        acc_scr[...] = jnp.zeros_like(acc_scr)

    xb = x_ref[...].astype(BF16)
    hg = jnp.dot(xb, wg_ref[0], preferred_element_type=F32)
    hu = jnp.dot(xb, wu_ref[0], preferred_element_type=F32)
    lane = lax.broadcasted_iota(jnp.int32, gates_scr.shape, 1)
    gate = jnp.sum(jnp.where(lane == e + N_GROUPS, gates_scr[...], 0.0), axis=-1, keepdims=True)
    hid = jax.nn.silu(hg) * hu * gate
    acc_scr[...] += _dot(hid, wd_ref[0])

    @pl.when(e == N_EXPERTS - 1)
    def _():
        o_ref[...] = _layer_norm(DN_ALPHA * x_ref[...] + acc_scr[...], lng_ref[...], lnb_ref[...])


def _moe_dense(x, wr, br, wg, wu, wd, lng, lnb, *, tm):
    m = x.shape[0]
    return pl.pallas_call(
        _moe_dense_kernel,
        grid=(m // tm, N_EXPERTS),
        in_specs=[
            pl.BlockSpec((tm, D_MODEL), lambda i, e: (i, 0)),
            _const_spec(wr.shape), _const_spec(br.shape),
            pl.BlockSpec((1, D_MODEL, D_EXPERT), lambda i, e: (e, 0, 0)),
            pl.BlockSpec((1, D_MODEL, D_EXPERT), lambda i, e: (e, 0, 0)),
            pl.BlockSpec((1, D_EXPERT, D_MODEL), lambda i, e: (e, 0, 0)),
            _const_spec(lng.shape), _const_spec(lnb.shape),
        ],
        out_specs=pl.BlockSpec((tm, D_MODEL), lambda i, e: (i, 0)),
        out_shape=jax.ShapeDtypeStruct(x.shape, F32),
        scratch_shapes=[pltpu.VMEM((tm, ROUTER_LANES), F32), pltpu.VMEM((tm, D_MODEL), F32)],
        compiler_params=_params("parallel", "arbitrary"),
        name="moe_dense",
    )(x, wr, br, wg, wu, wd, lng, lnb)


def _row(v):
    return v.reshape(1, -1)


def _to_time_major(s):
    return jnp.transpose(s, (1, 0, 2)).reshape(1, -1, s.shape[-1])


def _from_time_major(s, steps):
    return jnp.transpose(s.reshape(steps, DEC_BATCH, s.shape[-1]), (1, 0, 2))


def kernel(x_prompt, x_sample, cache_mem_k, cache_mem_v, state_pool, state_conv_c, state_conv_d, mem_prompt,
           w_in_ab, ln_v_g, ln_v_b, w_spatial, b_spatial, w_pool, pool_scale, w_out_ab,
           w_in_cd, conv_c_w, conv_c_b, ln_c_g, ln_c_b, conv_d_w, w_out_cd,
           w_q, w_k, w_v, w_o, w_group, b_group, w_router, b_router, w_gate, w_up, w_down, ln_g, ln_b):
    hp = x_prompt
    hs = _to_time_major(x_sample)
    mem_flat = mem_prompt.reshape(BATCH * N_MEM, D_MODEL)
    cache_k = cache_mem_k.reshape(DEPTH, DEC_BATCH, N_MEM, D_MODEL)
    cache_v = cache_mem_v.reshape(DEPTH, DEC_BATCH, N_MEM, D_MODEL)

    mem_k_p, mem_v_p = [], []
    pool_p, pool_s, chunk_v_s = [], [], []
    conv_c_p, conv_c_s, conv_d_p, conv_d_s = [], [], [], []

    for l in range(DEPTH):
        i = l // 2
        lng0, lnb0 = _row(ln_g[l, 0]), _row(ln_b[l, 0])
        if l % 2 == 0:
            bias = jnp.repeat(b_spatial[i].T, A_HEAD_DIM, axis=1)
            small = jnp.repeat(
                jnp.transpose(w_spatial[i][:, :DEC_SEQ, :DEC_SEQ], (1, 2, 0)).reshape(DEC_SEQ * DEC_SEQ, A_HEADS),
                A_HEAD_DIM, axis=1)
            common = (w_in_ab[i].astype(BF16), _row(ln_v_g[i]), _row(ln_v_b[i]))
            tail = (w_pool[i].astype(BF16), _row(pool_scale[i]), w_out_ab[i].astype(BF16), lng0, lnb0)
            hp, _, hist_p = _mixer_ab(hp, jnp.zeros((BATCH, POOL_HIST, D_B), F32), *common, w_spatial[i], bias,
                                      *tail, seqs=1, tt=512, pos0=0)
            hs, v_rows, hist_s = _mixer_ab(hs, _to_time_major(state_pool[i]), *common, small, bias[:DEC_SEQ],
                                           *tail, seqs=DEC_BATCH, tt=DEC_SEQ, pos0=PAST_LEN)
            pool_p.append(hist_p)
            pool_s.append(_from_time_major(hist_s, POOL_HIST))
            chunk_v_s.append(_from_time_major(v_rows, DEC_SEQ))
        else:
            wts = (w_in_cd[i].astype(BF16), conv_c_w[i], _row(conv_c_b[i]), _row(ln_c_g[i]), _row(ln_c_b[i]),
                   conv_d_w[i], w_out_cd[i].astype(BF16), lng0, lnb0)
            hp, hc_p, hd_p = _mixer_cd(hp, jnp.zeros((BATCH, CONV_C - 1, D_C), F32),
                                       jnp.zeros((BATCH, CONV_D - 1, D_D), F32), *wts, seqs=1, tt=512)
            hs, hc_s, hd_s = _mixer_cd(hs, _to_time_major(state_conv_c[i]), _to_time_major(state_conv_d[i]),
                                       *wts, seqs=DEC_BATCH, tt=DEC_SEQ)
            conv_c_p.append(hc_p)
            conv_d_p.append(hd_p)
            conv_c_s.append(_from_time_major(hc_s, CONV_C - 1))
            conv_d_s.append(_from_time_major(hd_s, CONV_D - 1))

        lng1, lnb1 = _row(ln_g[l, 1]), _row(ln_b[l, 1])
        wq, wo = w_q[l].astype(BF16), w_o[l].astype(BF16)
        kp, vp = _kv_proj(mem_flat, w_k[l].astype(BF16), w_v[l].astype(BF16), tm=512)
        mem_k_p.append(kp.reshape(BATCH, N_MEM, MEM_HEADS, MEM_HEAD_DIM))
        mem_v_p.append(vp.reshape(BATCH, N_MEM, MEM_HEADS, MEM_HEAD_DIM))
        hp = _attn_prompt(hp, kp.reshape(BATCH, N_MEM, D_MODEL), vp.reshape(BATCH, N_MEM, D_MODEL),
                          wq, wo, lng1, lnb1, tq=512)
        hs = _attn_sample(hs[0], cache_k[l], cache_v[l], wq, wo, lng1, lnb1, seq_blk=4)[None]

        lng2, lnb2 = _row(ln_g[l, 2]), _row(ln_b[l, 2])
        wr = jnp.zeros((D_MODEL, ROUTER_LANES), F32)
        wr = wr.at[:, :N_GROUPS].set(w_group[l]).at[:, N_GROUPS:N_GROUPS + N_EXPERTS].set(w_router[l])
        br = jnp.zeros((1, ROUTER_LANES), F32)
        br = br.at[0, :N_GROUPS].set(b_group[l]).at[0, N_GROUPS:N_GROUPS + N_EXPERTS].set(b_router[l])
        moe_w = (wr.astype(BF16), br, w_gate[l].astype(BF16), w_up[l].astype(BF16), w_down[l].astype(BF16),
                 lng2, lnb2)
        hp = _moe_dense(hp.reshape(BATCH * SEQ, D_MODEL), *moe_w, tm=1024).reshape(BATCH, SEQ, D_MODEL)
        hs = _moe_dense(hs[0], *moe_w, tm=DEC_SEQ * DEC_BATCH)[None]

    y_sample = _from_time_major(hs, DEC_SEQ)
    return (hp, y_sample, jnp.stack(mem_k_p), jnp.stack(mem_v_p), jnp.stack(pool_p), jnp.stack(conv_c_p),
            jnp.stack(conv_d_p), jnp.stack(chunk_v_s), jnp.stack(pool_s), jnp.stack(conv_c_s),
            jnp.stack(conv_d_s))
```

```python
import functools

import jax
import jax.numpy as jnp
from jax import lax
from jax.experimental import pallas as pl
from jax.experimental.pallas import tpu as pltpu

D_MODEL = 1024
BATCH = 8
SEQ = 2048
DEPTH = 4
DEC_BATCH = 128
DEC_SEQ = 4
PAST_LEN = 16384

CHUNK = 128
A_HEADS = 4
D_A = D_MODEL // 2
A_HEAD_DIM = D_A // A_HEADS
POOL_WINDOWS = (2, 4, 8, 16)
B_GROUPS = len(POOL_WINDOWS)
D_B = D_MODEL // 2
B_GROUP_DIM = D_B // B_GROUPS
POOL_HIST = max(POOL_WINDOWS) - 1
D_C = D_MODEL // 2
CONV_C = 31
D_D = D_MODEL // 2
CONV_D = 3
N_MEM = 256
MEM_HEADS = 4
MEM_HEAD_DIM = D_MODEL // MEM_HEADS
N_GROUPS = 4
EXPERTS_PER_GROUP = 4
N_EXPERTS = N_GROUPS * EXPERTS_PER_GROUP
D_EXPERT = 512
DN_ALPHA = (2 * DEPTH) ** 0.25
LN_EPS = 1e-5

LANES = 128
POOL_HIST_PAD = 16
CONV_C_HIST_PAD = 32
CONV_D_HIST_PAD = 8
ROUTER_LANES = 128
ROUTER_MASKED = -1e30
CLASS_LANE = N_GROUPS + N_EXPERTS
PAIRS = ((0, 1), (0, 2), (0, 3), (1, 2), (1, 3), (2, 3))
N_CLASSES = N_GROUPS * len(PAIRS)
D_WIDE = D_MODEL + ROUTER_LANES
KV_ROWS = N_MEM * MEM_HEADS * MEM_HEAD_DIM // LANES
ATT_ROWS = KV_ROWS // N_MEM * DEC_SEQ
VMEM_LIMIT_BYTES = 52 * 1024 * 1024

F32 = jnp.float32
BF16 = jnp.bfloat16


def _layer_norm(x, g, b):
    mu = jnp.mean(x, axis=-1, keepdims=True)
    xc = x - mu
    var = jnp.mean(xc * xc, axis=-1, keepdims=True)
    return xc * lax.rsqrt(var + LN_EPS) * g + b


def _dot(a, b):
    return jnp.dot(a.astype(BF16), b.astype(BF16), preferred_element_type=F32)


def _params(*semantics):
    return pltpu.CompilerParams(dimension_semantics=semantics, vmem_limit_bytes=VMEM_LIMIT_BYTES)


def _const_spec(shape):
    nd = len(shape)
    return pl.BlockSpec(shape, lambda *_: (0,) * nd)


def _mixer_ab_kernel(x_ref, hist_ref, w_in_ref, lnv_g_ref, lnv_b_ref, ws_ref, bs_ref, wpool_ref,
                     pscale_ref, w_out_ref, lng_ref, lnb_ref, *refs, seqs, tt, pos0, with_v):
    if with_v:
        h_out_ref, v_out_ref, hist_out_ref, zbuf = refs
    else:
        h_out_ref, hist_out_ref, zbuf = refs
    t_idx = pl.program_id(1)
    rows = tt * seqs
    hp = POOL_HIST_PAD

    @pl.when(t_idx == 0)
    def _():
        zbuf[pl.ds((hp - POOL_HIST) * seqs, POOL_HIST * seqs), :] = hist_ref[0]

    x = x_ref[0]
    h = _dot(x, w_in_ref[...])
    ua = jax.nn.gelu(h[:, :2 * D_A])
    u = ua[:, :D_A]
    v = _layer_norm(ua[:, D_A:], lnv_g_ref[...], lnv_b_ref[...])
    z = h[:, 2 * D_A:]
    if with_v:
        v_out_ref[0] = v
    zbuf[pl.ds(hp * seqs, rows), :] = z

    if seqs == 1:
        tri = (lax.broadcasted_iota(jnp.int32, (CHUNK, CHUNK), 0)
               >= lax.broadcasted_iota(jnp.int32, (CHUNK, CHUNK), 1))
        w_heads = [jnp.where(tri, ws_ref[hh], 0.0).astype(BF16) for hh in range(A_HEADS)]
        vb = v.astype(BF16)
        chunks = []
        for c in range(tt // CHUNK):
            heads = []
            for hh in range(A_HEADS):
                vc = vb[c * CHUNK:(c + 1) * CHUNK, hh * A_HEAD_DIM:(hh + 1) * A_HEAD_DIM]
                heads.append(jnp.dot(w_heads[hh], vc, preferred_element_type=F32))
            chunks.append(jnp.concatenate(heads, axis=-1) + bs_ref[...])
        mixed = jnp.concatenate(chunks, axis=0)
    else:
        parts = []
        for t in range(tt):
            acc = bs_ref[t:t + 1, :]
            for s in range(t + 1):
                acc = acc + ws_ref[t * tt + s:t * tt + s + 1, :] * v[s * seqs:(s + 1) * seqs, :]
            parts.append(acc)
        mixed = jnp.concatenate(parts, axis=0)
    a_out = u * mixed

    outs = []
    for gi, w in enumerate(POOL_WINDOWS):
        lo, hi = gi * B_GROUP_DIM, (gi + 1) * B_GROUP_DIM
        acc = zbuf[pl.ds(hp * seqs, rows), lo:hi]
        for j in range(1, w):
            acc = acc + zbuf[pl.ds((hp - j) * seqs, rows), lo:hi]
        if pos0 + 1 >= w:
            cnt = float(w)
        else:
            assert seqs == 1
            pos = pos0 + t_idx * tt + lax.broadcasted_iota(jnp.int32, (rows, 1), 0)
            cnt = jnp.minimum(pos + 1, w).astype(F32)
        pooled = acc / cnt - z[:, lo:hi]
        outs.append(_dot(pooled, wpool_ref[gi]))
    b_out = jnp.concatenate(outs, axis=-1) * pscale_ref[...]

    y = _dot(jnp.concatenate([a_out, b_out], axis=-1), w_out_ref[...])
    h_out_ref[0] = _layer_norm(DN_ALPHA * x + y, lng_ref[...], lnb_ref[...])

    hist_out_ref[0] = zbuf[pl.ds((tt + hp - POOL_HIST) * seqs, POOL_HIST * seqs), :]
    zbuf[pl.ds(0, hp * seqs), :] = zbuf[pl.ds(tt * seqs, hp * seqs), :]


def _mixer_ab(x, hist, w_in, lnv_g, lnv_b, ws, bs, wpool, pscale, w_out, lng, lnb, *, seqs, tt, pos0, with_v):
    nb, total_rows, _ = x.shape
    rows = tt * seqs
    n_t = total_rows // rows
    kern = functools.partial(_mixer_ab_kernel, seqs=seqs, tt=tt, pos0=pos0, with_v=with_v)
    h_spec = pl.BlockSpec((1, rows, D_MODEL), lambda n, t: (n, t, 0))
    v_spec = pl.BlockSpec((1, rows, D_A), lambda n, t: (n, t, 0))
    hist_spec = pl.BlockSpec((1, POOL_HIST * seqs, D_B), lambda n, t: (n, 0, 0))
    h_shape = jax.ShapeDtypeStruct(x.shape, F32)
    v_shape = jax.ShapeDtypeStruct((nb, total_rows, D_A), F32)
    hist_shape = jax.ShapeDtypeStruct((nb, POOL_HIST * seqs, D_B), F32)
    return pl.pallas_call(
        kern,
        grid=(nb, n_t),
        in_specs=[
            pl.BlockSpec((1, rows, D_MODEL), lambda n, t: (n, t, 0)),
            pl.BlockSpec((1, POOL_HIST * seqs, D_B), lambda n, t: (n, 0, 0)),
            _const_spec(w_in.shape), _const_spec(lnv_g.shape), _const_spec(lnv_b.shape),
            _const_spec(ws.shape), _const_spec(bs.shape), _const_spec(wpool.shape),
            _const_spec(pscale.shape), _const_spec(w_out.shape), _const_spec(lng.shape),
            _const_spec(lnb.shape),
        ],
        out_specs=[h_spec, v_spec, hist_spec] if with_v else [h_spec, hist_spec],
        out_shape=[h_shape, v_shape, hist_shape] if with_v else [h_shape, hist_shape],
        scratch_shapes=[pltpu.VMEM(((POOL_HIST_PAD + tt) * seqs, D_B), F32)],
        compiler_params=_params("parallel", "arbitrary"),
        name="mixer_ab",
    )(x, hist, w_in, lnv_g, lnv_b, ws, bs, wpool, pscale, w_out, lng, lnb)


def _dwconv(buf, w_ref, out, *, taps, hist_pad, seqs, rows, chunk):
    base = (hist_pad - (taps - 1)) * seqs

    def body(i, carry):
        r0 = i * chunk if isinstance(i, int) else pl.multiple_of(i * chunk, chunk)
        acc = w_ref[0:1, :] * buf[pl.ds(base + r0, chunk), :]
        for k in range(1, taps):
            acc = acc + w_ref[k:k + 1, :] * buf[pl.ds(base + k * seqs + r0, chunk), :]
        out[pl.ds(r0, chunk), :] = acc
        return carry

    if seqs % 8 == 0:
        lax.fori_loop(0, rows // chunk, body, 0)
    else:
        for i in range(rows // chunk):
            body(i, 0)


def _mixer_cd_kernel(x_ref, hist_c_ref, hist_d_ref, w_in_ref, ccw_ref, ccb_ref, lncg_ref, lncb_ref,
                     cdw_ref, w_out_ref, lng_ref, lnb_ref,
                     h_out_ref, hist_c_out_ref, hist_d_out_ref, cbuf, dbuf, cc_scr, cd_scr,
                     *, seqs, tt):
    t_idx = pl.program_id(1)
    rows = tt * seqs
    hc, hd = CONV_C_HIST_PAD, CONV_D_HIST_PAD
    nc, nd = CONV_C - 1, CONV_D - 1

    @pl.when(t_idx == 0)
    def _():
        cbuf[pl.ds((hc - nc) * seqs, nc * seqs), :] = hist_c_ref[0]
        dbuf[pl.ds((hd - nd) * seqs, nd * seqs), :] = hist_d_ref[0]

    x = x_ref[0]
    h = _dot(x, w_in_ref[...])
    glu = h[:, :D_C] * jax.nn.sigmoid(h[:, D_C:2 * D_C])
    o = 2 * D_C
    gate_b = h[:, o:o + D_D]
    gx = h[:, o + D_D:o + 2 * D_D] * h[:, o + 2 * D_D:]
    cbuf[pl.ds(hc * seqs, rows), :] = glu
    dbuf[pl.ds(hd * seqs, rows), :] = gx

    chunk = min(rows, 32)
    _dwconv(cbuf, ccw_ref, cc_scr, taps=CONV_C, hist_pad=hc, seqs=seqs, rows=rows, chunk=chunk)
    _dwconv(dbuf, cdw_ref, cd_scr, taps=CONV_D, hist_pad=hd, seqs=seqs, rows=rows, chunk=chunk)

    c_out = jax.nn.silu(_layer_norm(cc_scr[...] + ccb_ref[...], lncg_ref[...], lncb_ref[...]))
    d_out = gate_b * cd_scr[...]
    y = _dot(jnp.concatenate([c_out, d_out], axis=-1), w_out_ref[...])
    h_out_ref[0] = _layer_norm(DN_ALPHA * x + y, lng_ref[...], lnb_ref[...])

    hist_c_out_ref[0] = cbuf[pl.ds((tt + hc - nc) * seqs, nc * seqs), :]
    hist_d_out_ref[0] = dbuf[pl.ds((tt + hd - nd) * seqs, nd * seqs), :]
    cbuf[pl.ds(0, hc * seqs), :] = cbuf[pl.ds(tt * seqs, hc * seqs), :]
    dbuf[pl.ds(0, hd * seqs), :] = dbuf[pl.ds(tt * seqs, hd * seqs), :]


def _mixer_cd(x, hist_c, hist_d, w_in, ccw, ccb, lncg, lncb, cdw, w_out, lng, lnb, *, seqs, tt):
    nb, total_rows, _ = x.shape
    rows = tt * seqs
    n_t = total_rows // rows
    nc, nd = CONV_C - 1, CONV_D - 1
    kern = functools.partial(_mixer_cd_kernel, seqs=seqs, tt=tt)
    return pl.pallas_call(
        kern,
        grid=(nb, n_t),
        in_specs=[
            pl.BlockSpec((1, rows, D_MODEL), lambda n, t: (n, t, 0)),
            pl.BlockSpec((1, nc * seqs, D_C), lambda n, t: (n, 0, 0)),
            pl.BlockSpec((1, nd * seqs, D_D), lambda n, t: (n, 0, 0)),
            _const_spec(w_in.shape), _const_spec(ccw.shape), _const_spec(ccb.shape),
            _const_spec(lncg.shape), _const_spec(lncb.shape), _const_spec(cdw.shape),
            _const_spec(w_out.shape), _const_spec(lng.shape), _const_spec(lnb.shape),
        ],
        out_specs=[
            pl.BlockSpec((1, rows, D_MODEL), lambda n, t: (n, t, 0)),
            pl.BlockSpec((1, nc * seqs, D_C), lambda n, t: (n, 0, 0)),
            pl.BlockSpec((1, nd * seqs, D_D), lambda n, t: (n, 0, 0)),
        ],
        out_shape=[
            jax.ShapeDtypeStruct(x.shape, F32),
            jax.ShapeDtypeStruct((nb, nc * seqs, D_C), F32),
            jax.ShapeDtypeStruct((nb, nd * seqs, D_D), F32),
        ],
        scratch_shapes=[
            pltpu.VMEM(((CONV_C_HIST_PAD + tt) * seqs, D_C), F32),
            pltpu.VMEM(((CONV_D_HIST_PAD + tt) * seqs, D_D), F32),
            pltpu.VMEM((rows, D_C), F32),
            pltpu.VMEM((rows, D_D), F32),
        ],
        compiler_params=_params("parallel", "arbitrary"),
        name="mixer_cd",
    )(x, hist_c, hist_d, w_in, ccw, ccb, lncg, lncb, cdw, w_out, lng, lnb)


def _kv_proj_kernel(mem_ref, wk_ref, wv_ref, k_ref, v_ref):
    m = mem_ref[...].astype(BF16)
    k_ref[...] = jnp.dot(m, wk_ref[...], preferred_element_type=F32)
    v_ref[...] = jnp.dot(m, wv_ref[...], preferred_element_type=F32)


def _kv_proj(mem, wk, wv, *, tm):
    m = mem.shape[0]
    return pl.pallas_call(
        _kv_proj_kernel,
        grid=(m // tm,),
        in_specs=[pl.BlockSpec((tm, D_MODEL), lambda i: (i, 0)),
                  _const_spec(wk.shape), _const_spec(wv.shape)],
        out_specs=[pl.BlockSpec((tm, D_MODEL), lambda i: (i, 0))] * 2,
        out_shape=[jax.ShapeDtypeStruct((m, D_MODEL), F32)] * 2,
        compiler_params=_params("parallel"),
        name="kv_proj",
    )(mem, wk, wv)


def _attend(q, k, v):
    scale = MEM_HEAD_DIM ** -0.5
    qb = q.astype(BF16)
    outs = []
    for hh in range(MEM_HEADS):
        lo, hi = hh * MEM_HEAD_DIM, (hh + 1) * MEM_HEAD_DIM
        s = lax.dot_general(qb[:, lo:hi], k[:, lo:hi], (((1,), (1,)), ((), ())),
                            preferred_element_type=F32) * scale
        s = s - jnp.max(s, axis=-1, keepdims=True)
        p = jnp.exp(s)
        p = p / jnp.sum(p, axis=-1, keepdims=True)
        outs.append(jnp.dot(p.astype(BF16), v[:, lo:hi], preferred_element_type=F32))
    return jnp.concatenate(outs, axis=-1)


def _attn_prompt_kernel(x_ref, k_ref, v_ref, wq_ref, wo_ref, wr_ref, br_ref, lng_ref, lnb_ref, o_ref):
    x = x_ref[0]
    q = _dot(x, wq_ref[...])
    o = _attend(q, k_ref[0].astype(BF16), v_ref[0].astype(BF16))
    y = _dot(o, wo_ref[...])
    h2 = _layer_norm(DN_ALPHA * x + y, lng_ref[...], lnb_ref[...])
    o_ref[0, :, :D_MODEL] = h2
    o_ref[0, :, D_MODEL:] = _route(_dot(h2, wr_ref[...]) + br_ref[...])


def _attn_prompt(x, k, v, wq, wo, wr, br, lng, lnb, *, tq):
    nb, t, _ = x.shape
    return pl.pallas_call(
        _attn_prompt_kernel,
        grid=(nb, t // tq),
        in_specs=[
            pl.BlockSpec((1, tq, D_MODEL), lambda n, i: (n, i, 0)),
            pl.BlockSpec((1, N_MEM, D_MODEL), lambda n, i: (n, 0, 0)),
            pl.BlockSpec((1, N_MEM, D_MODEL), lambda n, i: (n, 0, 0)),
            _const_spec(wq.shape), _const_spec(wo.shape), _const_spec(wr.shape), _const_spec(br.shape),
            _const_spec(lng.shape), _const_spec(lnb.shape),
        ],
        out_specs=pl.BlockSpec((1, tq, D_WIDE), lambda n, i: (n, i, 0)),
        out_shape=jax.ShapeDtypeStruct((nb, t, D_WIDE), F32),
        compiler_params=_params("parallel", "arbitrary"),
        name="attn_prompt",
    )(x, k, v, wq, wo, wr, br, lng, lnb)


def _att_row(half, head, t):
    return (half * MEM_HEADS + head) * DEC_SEQ + t


def _attn_sample_kernel(x_ref, k_ref, v_ref, wq_ref, wo_ref, wr_ref, br_ref, lng_ref, lnb_ref, o_ref,
                        q_scr, qp_scr, op_scr, *, seq_blk, n_blk):
    i = pl.program_id(0)
    halves = MEM_HEAD_DIM // LANES
    blocks = [(t, hh, lt) for t in range(DEC_SEQ) for hh in range(MEM_HEADS) for lt in range(halves)]

    @pl.when(i == 0)
    def _():
        q_scr[...] = _dot(x_ref[...], wq_ref[...])
        for t, hh, lt in blocks:
            col = hh * MEM_HEAD_DIM + lt * LANES
            qp_scr[pl.ds(_att_row(lt, hh, t), DEC_BATCH, stride=ATT_ROWS), :] = (
                q_scr[t * DEC_BATCH:(t + 1) * DEC_BATCH, col:col + LANES])

    half_rows = ATT_ROWS // halves
    col = lax.broadcasted_iota(jnp.int32, (half_rows, KV_ROWS), 1)
    row_head = lax.shift_right_logical(lax.broadcasted_iota(jnp.int32, (half_rows, KV_ROWS), 0),
                                       DEC_SEQ.bit_length() - 1)
    col_cls = col & (halves * MEM_HEADS - 1)
    match0 = col_cls == row_head
    match1 = col_cls == row_head + MEM_HEADS
    scale = MEM_HEAD_DIM ** -0.5

    def body(j, carry):
        n = i * seq_blk + j
        r0 = pl.multiple_of(n * ATT_ROWS, ATT_ROWS)
        qp = qp_scr[pl.ds(r0, ATT_ROWS), :].astype(BF16)
        s = lax.dot_general(qp, k_ref[0, j].astype(BF16), (((1,), (1,)), ((), ())),
                            preferred_element_type=F32)
        part = jnp.where(match0, s[:half_rows], 0.0) + pltpu.roll(
            jnp.where(match1, s[half_rows:], 0.0), KV_ROWS - MEM_HEADS, axis=1)
        sv = jnp.where(match0, part * scale, ROUTER_MASKED)
        e = jnp.exp(sv - jnp.max(sv, axis=-1, keepdims=True))
        p = e / jnp.sum(e, axis=-1, keepdims=True)
        pp = jnp.concatenate([p, pltpu.roll(p, MEM_HEADS, axis=1)], axis=0).astype(BF16)
        op_scr[pl.ds(r0, ATT_ROWS), :] = jnp.dot(pp, v_ref[0, j].astype(BF16), preferred_element_type=F32)
        return carry

    lax.fori_loop(0, seq_blk, body, 0, unroll=True)

    @pl.when(i == n_blk - 1)
    def _():
        for t, hh, lt in blocks:
            col = hh * MEM_HEAD_DIM + lt * LANES
            q_scr[t * DEC_BATCH:(t + 1) * DEC_BATCH, col:col + LANES] = (
                op_scr[pl.ds(_att_row(lt, hh, t), DEC_BATCH, stride=ATT_ROWS), :])
        x = x_ref[...]
        y = _dot(q_scr[...], wo_ref[...])
        h2 = _layer_norm(DN_ALPHA * x + y, lng_ref[...], lnb_ref[...])
        o_ref[:, :D_MODEL] = h2
        o_ref[:, D_MODEL:] = _route(_dot(h2, wr_ref[...]) + br_ref[...])


def _attn_sample(x, k, v, layer, wq, wo, wr, br, lng, lnb, *, seq_blk):
    m = x.shape[0]
    n_blk = DEC_BATCH // seq_blk
    kern = functools.partial(_attn_sample_kernel, seq_blk=seq_blk, n_blk=n_blk)
    kv_spec = pl.BlockSpec((1, seq_blk, KV_ROWS, LANES), lambda i: (layer, i, 0, 0))
    return pl.pallas_call(
        kern,
        grid=(n_blk,),
        in_specs=[
            _const_spec(x.shape), kv_spec, kv_spec,
            _const_spec(wq.shape), _const_spec(wo.shape), _const_spec(wr.shape), _const_spec(br.shape),
            _const_spec(lng.shape), _const_spec(lnb.shape),
        ],
        out_specs=_const_spec((m, D_WIDE)),
        out_shape=jax.ShapeDtypeStruct((m, D_WIDE), F32),
        scratch_shapes=[pltpu.VMEM((m, D_MODEL), F32),
                        pltpu.VMEM((DEC_BATCH * ATT_ROWS, LANES), F32),
                        pltpu.VMEM((DEC_BATCH * ATT_ROWS, LANES), F32)],
        compiler_params=_params("arbitrary"),
        name="attn_sample",
    )(x, k, v, wq, wo, wr, br, lng, lnb)


def _route(logits):
    lane = lax.broadcasted_iota(jnp.int32, logits.shape, 1)
    lane_f = lane.astype(F32)
    neg = ROUTER_MASKED
    is_grp = lane < N_GROUPS
    gl = jnp.where(is_grp, logits, neg)
    gmax = jnp.max(gl, axis=-1, keepdims=True)
    gsel = jnp.min(jnp.where(gl == gmax, lane_f, float(ROUTER_LANES)), axis=-1, keepdims=True)
    g_w = 1.0 / jnp.sum(jnp.exp(gl - gmax), axis=-1, keepdims=True)
    e_idx = lane - N_GROUPS
    e_grp = lax.shift_right_arithmetic(e_idx, 2).astype(F32)
    in_grp = (e_idx >= 0) & (e_idx < N_EXPERTS) & (e_grp == gsel)
    el = jnp.where(in_grp, logits, neg)
    m1 = jnp.max(el, axis=-1, keepdims=True)
    i1 = jnp.min(jnp.where(el == m1, lane_f, float(ROUTER_LANES)), axis=-1, keepdims=True)
    el2 = jnp.where(lane_f == i1, neg, el)
    m2 = jnp.max(el2, axis=-1, keepdims=True)
    i2 = jnp.min(jnp.where(el2 == m2, lane_f, float(ROUTER_LANES)), axis=-1, keepdims=True)
    e2 = jnp.exp(m2 - m1)
    den = 1.0 + e2
    w1 = (1.0 / den) * g_w
    w2 = (e2 / den) * g_w
    first_lane = N_GROUPS + EXPERTS_PER_GROUP * gsel
    a = jnp.minimum(i1, i2) - first_lane
    b = jnp.maximum(i1, i2) - first_lane
    cls = gsel * len(PAIRS) + a * (2 * EXPERTS_PER_GROUP - 1 - a) * 0.5 + (b - a - 1.0)
    return (jnp.where(lane_f == i1, w1, 0.0) + jnp.where(lane_f == i2, w2, 0.0)
            + jnp.where(lane == CLASS_LANE, cls, 0.0))


def _gate_column(gates, expert):
    lane = lax.broadcasted_iota(jnp.int32, gates.shape, 1)
    return jnp.sum(jnp.where(lane == expert + N_GROUPS, gates, 0.0), axis=-1, keepdims=True)


def _expert_hidden(xb, wg, wu, gate):
    hg = jnp.dot(xb, wg, preferred_element_type=F32)
    hu = jnp.dot(xb, wu, preferred_element_type=F32)
    return (jax.nn.silu(hg) * hu * gate).astype(BF16)


def _moe_dense_kernel(x_ref, wg_ref, wu_ref, wd_ref, lng_ref, lnb_ref, o_ref, acc_scr):
    e = pl.program_id(1)

    @pl.when(e == 0)
    def _():
        acc_scr[...] = jnp.zeros_like(acc_scr)

    xb = x_ref[:, :D_MODEL].astype(BF16)
    hid = _expert_hidden(xb, wg_ref[0, 0].astype(BF16), wu_ref[0, 0].astype(BF16),
                         _gate_column(x_ref[:, D_MODEL:], e))
    acc_scr[...] += jnp.dot(hid, wd_ref[0, 0].astype(BF16), preferred_element_type=F32)

    @pl.when(e == N_EXPERTS - 1)
    def _():
        o_ref[...] = _layer_norm(DN_ALPHA * x_ref[:, :D_MODEL] + acc_scr[...], lng_ref[...], lnb_ref[...])


def _moe_dense(x, layer, wg, wu, wd, lng, lnb, *, tm):
    m = x.shape[0]
    return pl.pallas_call(
        _moe_dense_kernel,
        grid=(m // tm, N_EXPERTS),
        in_specs=[
            pl.BlockSpec((tm, D_WIDE), lambda i, e: (i, 0)),
            pl.BlockSpec((1, 1, D_MODEL, D_EXPERT), lambda i, e: (layer, e, 0, 0)),
            pl.BlockSpec((1, 1, D_MODEL, D_EXPERT), lambda i, e: (layer, e, 0, 0)),
            pl.BlockSpec((1, 1, D_EXPERT, D_MODEL), lambda i, e: (layer, e, 0, 0)),
            _const_spec(lng.shape), _const_spec(lnb.shape),
        ],
        out_specs=pl.BlockSpec((tm, D_MODEL), lambda i, e: (i, 0)),
        out_shape=jax.ShapeDtypeStruct((m, D_MODEL), F32),
        scratch_shapes=[pltpu.VMEM((tm, D_MODEL), F32)],
        compiler_params=_params("parallel", "arbitrary"),
        name="moe_dense",
    )(x, wg, wu, wd, lng, lnb)


FLAG_ACTIVE, FLAG_FIRST, FLAG_LAST, FLAG_NEW_A, FLAG_NEW_B = 1, 2, 4, 8, 16


def _moe_sparse_kernel(src_ref, tile_ref, ea_ref, eb_ref, lo_ref, hi_ref, flag_ref,
                       x_hbm, wga_ref, wua_ref, wda_ref, wgb_ref, wub_ref, wdb_ref, lng_ref, lnb_ref,
                       out_hbm, xbuf, obuf, acc, wg_s, wu_s, wd_s, gsem, ssem, *, tm, n_tiles, n_items):
    i = pl.program_id(0)
    t = tile_ref[i]
    slot = t % 2
    flags = flag_ref[i]

    def gather_row(tile, r, dst_slot):
        tok = src_ref[tile * tm + r]
        return pltpu.make_async_copy(x_hbm.at[pl.ds(tok, 1)], xbuf.at[dst_slot, pl.ds(r, 1)], gsem.at[dst_slot])

    def scatter_row(tile, r, src_slot):
        tok = src_ref[tile * tm + r]
        return pltpu.make_async_copy(obuf.at[src_slot, pl.ds(r, 1)], out_hbm.at[pl.ds(tok, 1)], ssem.at[src_slot])

    def start_rows(make, tile, s):
        def body(r, carry):
            make(tile, r, s).start()
            return carry
        lax.fori_loop(0, tm, body, 0, unroll=8)

    def wait_gather(s):
        pltpu.make_async_copy(x_hbm.at[pl.ds(0, tm)], xbuf.at[s], gsem.at[s]).wait()

    def wait_scatter(s):
        pltpu.make_async_copy(obuf.at[s], out_hbm.at[pl.ds(0, tm)], ssem.at[s]).wait()

    @pl.when(i == 0)
    def _():
        start_rows(gather_row, 0, 0)

    @pl.when((flags & FLAG_FIRST) != 0)
    def _():
        @pl.when(t + 1 < n_tiles)
        def _():
            start_rows(gather_row, t + 1, 1 - slot)
        wait_gather(slot)
        acc[...] = jnp.zeros_like(acc)

    @pl.when((flags & FLAG_NEW_A) != 0)
    def _():
        wg_s[0] = wga_ref[0, 0].astype(BF16)
        wu_s[0] = wua_ref[0, 0].astype(BF16)
        wd_s[0] = wda_ref[0, 0].astype(BF16)

    @pl.when((flags & FLAG_NEW_B) != 0)
    def _():
        wg_s[1] = wgb_ref[0, 0].astype(BF16)
        wu_s[1] = wub_ref[0, 0].astype(BF16)
        wd_s[1] = wdb_ref[0, 0].astype(BF16)

    @pl.when((flags & FLAG_ACTIVE) != 0)
    def _():
        xb = xbuf[slot, :, :D_MODEL].astype(BF16)
        gates = xbuf[slot, :, D_MODEL:]
        row = t * tm + lax.broadcasted_iota(jnp.int32, (tm, 1), 0)
        in_class = (row >= lo_ref[i]) & (row < hi_ref[i])
        gate_a = jnp.where(in_class, _gate_column(gates, ea_ref[i]), 0.0)
        gate_b = jnp.where(in_class, _gate_column(gates, eb_ref[i]), 0.0)
        hid_a = _expert_hidden(xb, wg_s[0], wu_s[0], gate_a)
        hid_b = _expert_hidden(xb, wg_s[1], wu_s[1], gate_b)
        acc[...] += (jnp.dot(hid_a, wd_s[0], preferred_element_type=F32)
                     + jnp.dot(hid_b, wd_s[1], preferred_element_type=F32))

    @pl.when((flags & FLAG_LAST) != 0)
    def _():
        @pl.when(t >= 2)
        def _():
            wait_scatter(slot)
        obuf[slot] = _layer_norm(DN_ALPHA * xbuf[slot, :, :D_MODEL] + acc[...], lng_ref[...], lnb_ref[...])
        start_rows(scatter_row, t, slot)

    @pl.when(i == n_items - 1)
    def _():
        wait_scatter(0)
        wait_scatter(1)


def _moe_schedule(cls, *, tm):
    m = cls.shape[0]
    n_tiles = m // tm
    n_items = n_tiles + N_CLASSES - 1
    i32 = jnp.int32
    src = jnp.argsort(cls, stable=True).astype(i32)
    counts = jnp.sum((cls[:, None] == jnp.arange(N_CLASSES, dtype=i32)[None, :]).astype(i32), axis=0)
    ends = jnp.cumsum(counts)
    starts = ends - counts
    tile_lo = jnp.arange(n_tiles, dtype=i32) * tm
    first_c = jnp.searchsorted(ends, tile_lo, side="right").astype(i32)
    last_c = jnp.searchsorted(starts, tile_lo + tm, side="left").astype(i32) - 1
    per_tile = last_c - first_c + 1
    item_end = jnp.cumsum(per_tile)
    idx = jnp.arange(n_items, dtype=i32)
    tile = jnp.minimum(jnp.searchsorted(item_end, idx, side="right").astype(i32), n_tiles - 1)
    begin = item_end[tile] - per_tile[tile]
    active = idx < item_end[-1]
    c = jnp.where(active, first_c[tile] + idx - begin, last_c[n_tiles - 1])
    pair = jnp.asarray(PAIRS, dtype=i32)[c % len(PAIRS)]
    ea = (c // len(PAIRS)) * EXPERTS_PER_GROUP + pair[:, 0]
    eb = (c // len(PAIRS)) * EXPERTS_PER_GROUP + pair[:, 1]
    lo = jnp.where(active, starts[c], 0).astype(i32)
    hi = jnp.where(active, ends[c], 0).astype(i32)
    new_a = jnp.concatenate([jnp.ones((1,), bool), ea[1:] != ea[:-1]])
    new_b = jnp.concatenate([jnp.ones((1,), bool), eb[1:] != eb[:-1]])
    flags = (active * FLAG_ACTIVE + (active & (idx == begin)) * FLAG_FIRST
             + (active & (idx == item_end[tile] - 1)) * FLAG_LAST + new_a * FLAG_NEW_A + new_b * FLAG_NEW_B)
    return src, tile, ea.astype(i32), eb.astype(i32), lo, hi, flags.astype(i32)


def _moe_sparse(x, layer, wg, wu, wd, lng, lnb, *, tm):
    m = x.shape[0]
    n_tiles = m // tm
    n_items = n_tiles + N_CLASSES - 1
    cls = x[:, D_MODEL + CLASS_LANE].astype(jnp.int32)
    tables = _moe_schedule(cls, tm=tm)
    kern = functools.partial(_moe_sparse_kernel, tm=tm, n_tiles=n_tiles, n_items=n_items)

    def w_spec(shape, which):
        return pl.BlockSpec((1, 1) + shape, lambda i, src, tile, ea, eb, lo, hi, fl: (layer, (ea, eb)[which][i], 0, 0))

    up, down = (D_MODEL, D_EXPERT), (D_EXPERT, D_MODEL)
    grid_spec = pltpu.PrefetchScalarGridSpec(
        num_scalar_prefetch=len(tables),
        grid=(n_items,),
        in_specs=[
            pl.BlockSpec(memory_space=pl.ANY),
            w_spec(up, 0), w_spec(up, 0), w_spec(down, 0), w_spec(up, 1), w_spec(up, 1), w_spec(down, 1),
            _const_spec(lng.shape), _const_spec(lnb.shape),
        ],
        out_specs=pl.BlockSpec(memory_space=pl.ANY),
        scratch_shapes=[
            pltpu.VMEM((2, tm, D_WIDE), F32), pltpu.VMEM((2, tm, D_MODEL), F32), pltpu.VMEM((tm, D_MODEL), F32),
            pltpu.VMEM((2,) + up, BF16), pltpu.VMEM((2,) + up, BF16), pltpu.VMEM((2,) + down, BF16),
            pltpu.SemaphoreType.DMA((2,)), pltpu.SemaphoreType.DMA((2,)),
        ],
    )
    return pl.pallas_call(
        kern,
        grid_spec=grid_spec,
        out_shape=jax.ShapeDtypeStruct((m, D_MODEL), F32),
        compiler_params=_params("arbitrary"),
        name="moe_sparse",
    )(*tables, x, wg, wu, wd, wg, wu, wd, lng, lnb)


def _row(v):
    return v.reshape(1, -1)


def _to_time_major(s):
    return jnp.transpose(s, (1, 0, 2)).reshape(1, -1, s.shape[-1])


def _from_time_major(s, steps):
    return jnp.transpose(s.reshape(steps, DEC_BATCH, s.shape[-1]), (1, 0, 2))


def _kv_lane_view(cache):
    d, n = cache.shape[:2]
    halves = MEM_HEAD_DIM // LANES
    v = cache.reshape(d, n, N_MEM, MEM_HEADS, halves, LANES)
    return jnp.transpose(v, (0, 1, 2, 4, 3, 5)).reshape(d, n, KV_ROWS, LANES)


def kernel(x_prompt, x_sample, cache_mem_k, cache_mem_v, state_pool, state_conv_c, state_conv_d, mem_prompt,
           w_in_ab, ln_v_g, ln_v_b, w_spatial, b_spatial, w_pool, pool_scale, w_out_ab,
           w_in_cd, conv_c_w, conv_c_b, ln_c_g, ln_c_b, conv_d_w, w_out_cd,
           w_q, w_k, w_v, w_o, w_group, b_group, w_router, b_router, w_gate, w_up, w_down, ln_g, ln_b):
    hp = x_prompt
    hs = _to_time_major(x_sample)
    mem_flat = mem_prompt.reshape(BATCH * N_MEM, D_MODEL)
    cache_k = _kv_lane_view(cache_mem_k)
    cache_v = _kv_lane_view(cache_mem_v)

    mem_k_p, mem_v_p = [], []
    pool_p, pool_s, chunk_v_s = [], [], []
    conv_c_p, conv_c_s, conv_d_p, conv_d_s = [], [], [], []

    for l in range(DEPTH):
        i = l // 2
        lng0, lnb0 = _row(ln_g[l, 0]), _row(ln_b[l, 0])
        if l % 2 == 0:
            bias = jnp.repeat(b_spatial[i].T, A_HEAD_DIM, axis=1)
            small = jnp.repeat(
                jnp.transpose(w_spatial[i][:, :DEC_SEQ, :DEC_SEQ], (1, 2, 0)).reshape(DEC_SEQ * DEC_SEQ, A_HEADS),
                A_HEAD_DIM, axis=1)
            common = (w_in_ab[i].astype(BF16), _row(ln_v_g[i]), _row(ln_v_b[i]))
            tail = (w_pool[i].astype(BF16), _row(pool_scale[i]), w_out_ab[i].astype(BF16), lng0, lnb0)
            hp, hist_p = _mixer_ab(hp, jnp.zeros((BATCH, POOL_HIST, D_B), F32), *common, w_spatial[i], bias,
                                   *tail, seqs=1, tt=512, pos0=0, with_v=False)
            hs, v_rows, hist_s = _mixer_ab(hs, _to_time_major(state_pool[i]), *common, small, bias[:DEC_SEQ],
                                           *tail, seqs=DEC_BATCH, tt=DEC_SEQ, pos0=PAST_LEN, with_v=True)
            pool_p.append(hist_p)
            pool_s.append(_from_time_major(hist_s, POOL_HIST))
            chunk_v_s.append(_from_time_major(v_rows, DEC_SEQ))
        else:
            wts = (w_in_cd[i].astype(BF16), conv_c_w[i], _row(conv_c_b[i]), _row(ln_c_g[i]), _row(ln_c_b[i]),
                   conv_d_w[i], w_out_cd[i].astype(BF16), lng0, lnb0)
            hp, hc_p, hd_p = _mixer_cd(hp, jnp.zeros((BATCH, CONV_C - 1, D_C), F32),
                                       jnp.zeros((BATCH, CONV_D - 1, D_D), F32), *wts, seqs=1, tt=512)
            hs, hc_s, hd_s = _mixer_cd(hs, _to_time_major(state_conv_c[i]), _to_time_major(state_conv_d[i]),
                                       *wts, seqs=DEC_BATCH, tt=DEC_SEQ)
            conv_c_p.append(hc_p)
            conv_d_p.append(hd_p)
            conv_c_s.append(_from_time_major(hc_s, CONV_C - 1))
            conv_d_s.append(_from_time_major(hd_s, CONV_D - 1))

        lng1, lnb1 = _row(ln_g[l, 1]), _row(ln_b[l, 1])
        wq, wo = w_q[l].astype(BF16), w_o[l].astype(BF16)
        wr = jnp.zeros((D_MODEL, ROUTER_LANES), F32)
        wr = wr.at[:, :N_GROUPS].set(w_group[l]).at[:, N_GROUPS:N_GROUPS + N_EXPERTS].set(w_router[l])
        br = jnp.zeros((1, ROUTER_LANES), F32)
        br = br.at[0, :N_GROUPS].set(b_group[l]).at[0, N_GROUPS:N_GROUPS + N_EXPERTS].set(b_router[l])
        wr = wr.astype(BF16)
        kp, vp = _kv_proj(mem_flat, w_k[l].astype(BF16), w_v[l].astype(BF16), tm=512)
        mem_k_p.append(kp.reshape(BATCH, N_MEM, MEM_HEADS, MEM_HEAD_DIM))
        mem_v_p.append(vp.reshape(BATCH, N_MEM, MEM_HEADS, MEM_HEAD_DIM))
        hp_wide = _attn_prompt(hp, kp.reshape(BATCH, N_MEM, D_MODEL), vp.reshape(BATCH, N_MEM, D_MODEL),
                               wq, wo, wr, br, lng1, lnb1, tq=512)
        hs_wide = _attn_sample(hs[0], cache_k, cache_v, l, wq, wo, wr, br, lng1, lnb1, seq_blk=4)

        moe_w = (l, w_gate, w_up, w_down, _row(ln_g[l, 2]), _row(ln_b[l, 2]))
        hp = _moe_sparse(hp_wide.reshape(BATCH * SEQ, D_WIDE), *moe_w, tm=256).reshape(BATCH, SEQ, D_MODEL)
        hs = _moe_dense(hs_wide, *moe_w, tm=DEC_SEQ * DEC_BATCH)[None]

    y_sample = _from_time_major(hs, DEC_SEQ)
    return (hp, y_sample, jnp.stack(mem_k_p), jnp.stack(mem_v_p), jnp.stack(pool_p), jnp.stack(conv_c_p),
            jnp.stack(conv_d_p), jnp.stack(chunk_v_s), jnp.stack(pool_s), jnp.stack(conv_c_s),
            jnp.stack(conv_d_s))
```

```python
import functools

import jax
import jax.numpy as jnp
from jax import lax
from jax.experimental import pallas as pl
from jax.experimental.pallas import tpu as pltpu

D_MODEL = 1024
BATCH = 8
SEQ = 2048
DEPTH = 4
DEC_BATCH = 128
DEC_SEQ = 4
PAST_LEN = 16384

CHUNK = 128
A_HEADS = 4
D_A = D_MODEL // 2
A_HEAD_DIM = D_A // A_HEADS
POOL_WINDOWS = (2, 4, 8, 16)
B_GROUPS = len(POOL_WINDOWS)
D_B = D_MODEL // 2
B_GROUP_DIM = D_B // B_GROUPS
POOL_HIST = max(POOL_WINDOWS) - 1
D_C = D_MODEL // 2
CONV_C = 31
D_D = D_MODEL // 2
CONV_D = 3
N_MEM = 256
MEM_HEADS = 4
MEM_HEAD_DIM = D_MODEL // MEM_HEADS
N_GROUPS = 4
EXPERTS_PER_GROUP = 4
N_EXPERTS = N_GROUPS * EXPERTS_PER_GROUP
D_EXPERT = 512
DN_ALPHA = (2 * DEPTH) ** 0.25
LN_EPS = 1e-5

LANES = 128
SUBLANES = 8
POOL_HIST_PAD = 16
CONV_C_HIST_PAD = 32
CONV_D_HIST_PAD = 8
ROUTER_LANES = 128
ROUTER_MASKED = -1e30
CLASS_LANE = N_GROUPS + N_EXPERTS
PAIRS = ((0, 1), (0, 2), (1, 2), (1, 3), (0, 3), (2, 3))
N_CLASSES = N_GROUPS * len(PAIRS)
D_WIDE = D_MODEL + ROUTER_LANES
KV_ROWS = N_MEM * MEM_HEADS * MEM_HEAD_DIM // LANES
ATT_ROWS = KV_ROWS // N_MEM * DEC_SEQ
VMEM_LIMIT_BYTES = 52 * 1024 * 1024

F32 = jnp.float32
BF16 = jnp.bfloat16


def _layer_norm(x, g, b):
    mu = jnp.mean(x, axis=-1, keepdims=True)
    xc = x - mu
    var = jnp.mean(xc * xc, axis=-1, keepdims=True)
    return xc * lax.rsqrt(var + LN_EPS) * g + b


def _dot(a, b):
    return jnp.dot(a.astype(BF16), b.astype(BF16), preferred_element_type=F32)


def _params(*semantics):
    return pltpu.CompilerParams(dimension_semantics=semantics, vmem_limit_bytes=VMEM_LIMIT_BYTES)


def _const_spec(shape):
    nd = len(shape)
    return pl.BlockSpec(shape, lambda *_: (0,) * nd)


def _mixer_ab_kernel(x_ref, hist_ref, w_in_ref, lnv_g_ref, lnv_b_ref, ws_ref, bs_ref, wpool_ref,
                     pscale_ref, w_out_ref, lng_ref, lnb_ref, *refs, seqs, tt, pos0, with_v):
    if with_v:
        h_out_ref, v_out_ref, hist_out_ref, zbuf = refs
    else:
        h_out_ref, hist_out_ref, zbuf = refs
    t_idx = pl.program_id(1)
    rows = tt * seqs
    hp = POOL_HIST_PAD

    @pl.when(t_idx == 0)
    def _():
        zbuf[pl.ds((hp - POOL_HIST) * seqs, POOL_HIST * seqs), :] = hist_ref[0]

    x = x_ref[0]
    h = _dot(x, w_in_ref[...])
    ua = jax.nn.gelu(h[:, :2 * D_A])
    u = ua[:, :D_A]
    v = _layer_norm(ua[:, D_A:], lnv_g_ref[...], lnv_b_ref[...])
    z = h[:, 2 * D_A:]
    if with_v:
        v_out_ref[0] = v
    zbuf[pl.ds(hp * seqs, rows), :] = z

    if seqs == 1:
        tri = (lax.broadcasted_iota(jnp.int32, (CHUNK, CHUNK), 0)
               >= lax.broadcasted_iota(jnp.int32, (CHUNK, CHUNK), 1))
        w_heads = [jnp.where(tri, ws_ref[hh], 0.0).astype(BF16) for hh in range(A_HEADS)]
        vb = v.astype(BF16)
        chunks = []
        for c in range(tt // CHUNK):
            heads = []
            for hh in range(A_HEADS):
                vc = vb[c * CHUNK:(c + 1) * CHUNK, hh * A_HEAD_DIM:(hh + 1) * A_HEAD_DIM]
                heads.append(jnp.dot(w_heads[hh], vc, preferred_element_type=F32))
            chunks.append(jnp.concatenate(heads, axis=-1) + bs_ref[...])
        mixed = jnp.concatenate(chunks, axis=0)
    else:
        parts = []
        for t in range(tt):
            acc = bs_ref[t:t + 1, :]
            for s in range(t + 1):
                acc = acc + ws_ref[t * tt + s:t * tt + s + 1, :] * v[s * seqs:(s + 1) * seqs, :]
            parts.append(acc)
        mixed = jnp.concatenate(parts, axis=0)
    a_out = u * mixed

    outs = []
    for gi, w in enumerate(POOL_WINDOWS):
        lo, hi = gi * B_GROUP_DIM, (gi + 1) * B_GROUP_DIM
        acc = zbuf[pl.ds(hp * seqs, rows), lo:hi]
        for j in range(1, w):
            acc = acc + zbuf[pl.ds((hp - j) * seqs, rows), lo:hi]
        if pos0 + 1 >= w:
            cnt = float(w)
        else:
            assert seqs == 1
            pos = pos0 + t_idx * tt + lax.broadcasted_iota(jnp.int32, (rows, 1), 0)
            cnt = jnp.minimum(pos + 1, w).astype(F32)
        pooled = acc / cnt - z[:, lo:hi]
        outs.append(_dot(pooled, wpool_ref[gi]))
    b_out = jnp.concatenate(outs, axis=-1) * pscale_ref[...]

    y = _dot(jnp.concatenate([a_out, b_out], axis=-1), w_out_ref[...])
    h_out_ref[0] = _layer_norm(DN_ALPHA * x + y, lng_ref[...], lnb_ref[...])

    hist_out_ref[0] = zbuf[pl.ds((tt + hp - POOL_HIST) * seqs, POOL_HIST * seqs), :]
    zbuf[pl.ds(0, hp * seqs), :] = zbuf[pl.ds(tt * seqs, hp * seqs), :]


def _mixer_ab(x, hist, w_in, lnv_g, lnv_b, ws, bs, wpool, pscale, w_out, lng, lnb, *, seqs, tt, pos0, with_v):
    nb, total_rows, _ = x.shape
    rows = tt * seqs
    n_t = total_rows // rows
    kern = functools.partial(_mixer_ab_kernel, seqs=seqs, tt=tt, pos0=pos0, with_v=with_v)
    h_spec = pl.BlockSpec((1, rows, D_MODEL), lambda n, t: (n, t, 0))
    v_spec = pl.BlockSpec((1, rows, D_A), lambda n, t: (n, t, 0))
    hist_spec = pl.BlockSpec((1, POOL_HIST * seqs, D_B), lambda n, t: (n, 0, 0))
    h_shape = jax.ShapeDtypeStruct(x.shape, F32)
    v_shape = jax.ShapeDtypeStruct((nb, total_rows, D_A), F32)
    hist_shape = jax.ShapeDtypeStruct((nb, POOL_HIST * seqs, D_B), F32)
    return pl.pallas_call(
        kern,
        grid=(nb, n_t),
        in_specs=[
            pl.BlockSpec((1, rows, D_MODEL), lambda n, t: (n, t, 0)),
            pl.BlockSpec((1, POOL_HIST * seqs, D_B), lambda n, t: (n, 0, 0)),
            _const_spec(w_in.shape), _const_spec(lnv_g.shape), _const_spec(lnv_b.shape),
            _const_spec(ws.shape), _const_spec(bs.shape), _const_spec(wpool.shape),
            _const_spec(pscale.shape), _const_spec(w_out.shape), _const_spec(lng.shape),
            _const_spec(lnb.shape),
        ],
        out_specs=[h_spec, v_spec, hist_spec] if with_v else [h_spec, hist_spec],
        out_shape=[h_shape, v_shape, hist_shape] if with_v else [h_shape, hist_shape],
        scratch_shapes=[pltpu.VMEM(((POOL_HIST_PAD + tt) * seqs, D_B), F32)],
        compiler_params=_params("parallel", "arbitrary"),
        name="mixer_ab",
    )(x, hist, w_in, lnv_g, lnv_b, ws, bs, wpool, pscale, w_out, lng, lnb)


def _shift_rows(taps, hist_pad):
    base = hist_pad - (taps - 1)
    return max((base + k) // SUBLANES * SUBLANES for k in range(taps) if (base + k) % SUBLANES)


def _dwconv(buf, w_ref, out, shifted, *, taps, hist_pad, seqs, rows, chunk):
    base = (hist_pad - (taps - 1)) * seqs
    if shifted is not None:
        span = rows + _shift_rows(taps, hist_pad)
        for b in range(1, SUBLANES):
            shifted[b - 1] = buf[pl.ds(b, span), :]

    def tap(k, r0):
        off = base + k * seqs
        phase = off % SUBLANES
        if shifted is None or phase == 0:
            return buf[pl.ds(off + r0, chunk), :]
        return shifted[phase - 1, pl.ds(off - phase + r0, chunk), :]

    def body(i, carry):
        r0 = i * chunk if isinstance(i, int) else pl.multiple_of(i * chunk, chunk)
        acc = w_ref[0:1, :] * tap(0, r0)
        for k in range(1, taps):
            acc = acc + w_ref[k:k + 1, :] * tap(k, r0)
        out[pl.ds(r0, chunk), :] = acc
        return carry

    if seqs % SUBLANES == 0 or shifted is not None:
        lax.fori_loop(0, rows // chunk, body, 0)
    else:
        for i in range(rows // chunk):
            body(i, 0)


def _mixer_cd_kernel(x_ref, hist_c_ref, hist_d_ref, w_in_ref, ccw_ref, ccb_ref, lncg_ref, lncb_ref,
                     cdw_ref, w_out_ref, lng_ref, lnb_ref,
                     h_out_ref, hist_c_out_ref, hist_d_out_ref, cbuf, dbuf, cc_scr, cd_scr, *maybe_shifted,
                     seqs, tt):
    shifted = maybe_shifted[0] if maybe_shifted else None
    t_idx = pl.program_id(1)
    rows = tt * seqs
    hc, hd = CONV_C_HIST_PAD, CONV_D_HIST_PAD
    nc, nd = CONV_C - 1, CONV_D - 1

    @pl.when(t_idx == 0)
    def _():
        cbuf[pl.ds((hc - nc) * seqs, nc * seqs), :] = hist_c_ref[0]
        dbuf[pl.ds((hd - nd) * seqs, nd * seqs), :] = hist_d_ref[0]

    x = x_ref[0]
    h = _dot(x, w_in_ref[...])
    glu = h[:, :D_C] * jax.nn.sigmoid(h[:, D_C:2 * D_C])
    o = 2 * D_C
    gate_b = h[:, o:o + D_D]
    gx = h[:, o + D_D:o + 2 * D_D] * h[:, o + 2 * D_D:]
    cbuf[pl.ds(hc * seqs, rows), :] = glu
    dbuf[pl.ds(hd * seqs, rows), :] = gx

    chunk = min(rows, 32)
    _dwconv(cbuf, ccw_ref, cc_scr, shifted, taps=CONV_C, hist_pad=hc, seqs=seqs, rows=rows, chunk=chunk)
    _dwconv(dbuf, cdw_ref, cd_scr, None, taps=CONV_D, hist_pad=hd, seqs=seqs, rows=rows, chunk=chunk)

    c_out = jax.nn.silu(_layer_norm(cc_scr[...] + ccb_ref[...], lncg_ref[...], lncb_ref[...]))
    d_out = gate_b * cd_scr[...]
    y = _dot(jnp.concatenate([c_out, d_out], axis=-1), w_out_ref[...])
    h_out_ref[0] = _layer_norm(DN_ALPHA * x + y, lng_ref[...], lnb_ref[...])

    hist_c_out_ref[0] = cbuf[pl.ds((tt + hc - nc) * seqs, nc * seqs), :]
    hist_d_out_ref[0] = dbuf[pl.ds((tt + hd - nd) * seqs, nd * seqs), :]
    cbuf[pl.ds(0, hc * seqs), :] = cbuf[pl.ds(tt * seqs, hc * seqs), :]
    dbuf[pl.ds(0, hd * seqs), :] = dbuf[pl.ds(tt * seqs, hd * seqs), :]


def _mixer_cd(x, hist_c, hist_d, w_in, ccw, ccb, lncg, lncb, cdw, w_out, lng, lnb, *, seqs, tt):
    nb, total_rows, _ = x.shape
    rows = tt * seqs
    n_t = total_rows // rows
    nc, nd = CONV_C - 1, CONV_D - 1
    kern = functools.partial(_mixer_cd_kernel, seqs=seqs, tt=tt)
    return pl.pallas_call(
        kern,
        grid=(nb, n_t),
        in_specs=[
            pl.BlockSpec((1, rows, D_MODEL), lambda n, t: (n, t, 0)),
            pl.BlockSpec((1, nc * seqs, D_C), lambda n, t: (n, 0, 0)),
            pl.BlockSpec((1, nd * seqs, D_D), lambda n, t: (n, 0, 0)),
            _const_spec(w_in.shape), _const_spec(ccw.shape), _const_spec(ccb.shape),
            _const_spec(lncg.shape), _const_spec(lncb.shape), _const_spec(cdw.shape),
            _const_spec(w_out.shape), _const_spec(lng.shape), _const_spec(lnb.shape),
        ],
        out_specs=[
            pl.BlockSpec((1, rows, D_MODEL), lambda n, t: (n, t, 0)),
            pl.BlockSpec((1, nc * seqs, D_C), lambda n, t: (n, 0, 0)),
            pl.BlockSpec((1, nd * seqs, D_D), lambda n, t: (n, 0, 0)),
        ],
        out_shape=[
            jax.ShapeDtypeStruct(x.shape, F32),
            jax.ShapeDtypeStruct((nb, nc * seqs, D_C), F32),
            jax.ShapeDtypeStruct((nb, nd * seqs, D_D), F32),
        ],
        scratch_shapes=[
            pltpu.VMEM(((CONV_C_HIST_PAD + tt) * seqs, D_C), F32),
            pltpu.VMEM(((CONV_D_HIST_PAD + tt) * seqs, D_D), F32),
            pltpu.VMEM((rows, D_C), F32),
            pltpu.VMEM((rows, D_D), F32),
        ] + ([pltpu.VMEM((SUBLANES - 1, rows + _shift_rows(CONV_C, CONV_C_HIST_PAD), D_C), F32)]
             if seqs % SUBLANES else []),
        compiler_params=_params("parallel", "arbitrary"),
        name="mixer_cd",
    )(x, hist_c, hist_d, w_in, ccw, ccb, lncg, lncb, cdw, w_out, lng, lnb)


def _kv_proj_kernel(mem_ref, wk_ref, wv_ref, k_ref, v_ref, kview_ref, vview_ref):
    m = mem_ref[...].astype(BF16)
    halves = MEM_HEAD_DIM // LANES
    for w_ref, o_ref, view_ref in ((wk_ref, k_ref, kview_ref), (wv_ref, v_ref, vview_ref)):
        y = jnp.dot(m, w_ref[0], preferred_element_type=F32)
        o_ref[0] = y
        for hh in range(MEM_HEADS):
            for lt in range(halves):
                col = hh * MEM_HEAD_DIM + lt * LANES
                view_ref[0, 0, pl.ds(lt * MEM_HEADS + hh, N_MEM, stride=halves * MEM_HEADS), :] = (
                    y[:, col:col + LANES])


def _kv_proj(mem, wk, wv):
    rows = jax.ShapeDtypeStruct((DEPTH, BATCH * N_MEM, D_MODEL), F32)
    view = jax.ShapeDtypeStruct((DEPTH, BATCH, KV_ROWS, LANES), F32)
    w_spec = pl.BlockSpec((1, D_MODEL, D_MODEL), lambda l, n: (l, 0, 0))
    return pl.pallas_call(
        _kv_proj_kernel,
        grid=(DEPTH, BATCH),
        in_specs=[pl.BlockSpec((N_MEM, D_MODEL), lambda l, n: (n, 0)), w_spec, w_spec],
        out_specs=[pl.BlockSpec((1, N_MEM, D_MODEL), lambda l, n: (l, n, 0))] * 2
        + [pl.BlockSpec((1, 1, KV_ROWS, LANES), lambda l, n: (l, n, 0, 0))] * 2,
        out_shape=[rows, rows, view, view],
        compiler_params=_params("parallel", "arbitrary"),
        name="kv_proj",
    )(mem, wk, wv)


def _attend(q, k, v):
    scale = MEM_HEAD_DIM ** -0.5
    qb = q.astype(BF16)
    outs = []
    for hh in range(MEM_HEADS):
        lo, hi = hh * MEM_HEAD_DIM, (hh + 1) * MEM_HEAD_DIM
        s = lax.dot_general(qb[:, lo:hi], k[:, lo:hi], (((1,), (1,)), ((), ())),
                            preferred_element_type=F32) * scale
        s = s - jnp.max(s, axis=-1, keepdims=True)
        p = jnp.exp(s)
        p = p / jnp.sum(p, axis=-1, keepdims=True)
        outs.append(jnp.dot(p.astype(BF16), v[:, lo:hi], preferred_element_type=F32))
    return jnp.concatenate(outs, axis=-1)


def _attn_prompt_kernel(x_ref, k_ref, v_ref, wq_ref, wo_ref, wr_ref, br_ref, lng_ref, lnb_ref, o_ref):
    x = x_ref[0]
    q = _dot(x, wq_ref[...])
    o = _attend(q, k_ref[0].astype(BF16), v_ref[0].astype(BF16))
    y = _dot(o, wo_ref[...])
    h2 = _layer_norm(DN_ALPHA * x + y, lng_ref[...], lnb_ref[...])
    o_ref[0, :, :D_MODEL] = h2
    o_ref[0, :, D_MODEL:] = _route(_dot(h2, wr_ref[...]) + br_ref[...])


def _attn_prompt(x, k, v, layer, wq, wo, wr, br, lng, lnb, *, tq):
    nb, t, _ = x.shape
    return pl.pallas_call(
        _attn_prompt_kernel,
        grid=(nb, t // tq),
        in_specs=[
            pl.BlockSpec((1, tq, D_MODEL), lambda n, i: (n, i, 0)),
            pl.BlockSpec((1, N_MEM, D_MODEL), lambda n, i: (layer, n, 0)),
            pl.BlockSpec((1, N_MEM, D_MODEL), lambda n, i: (layer, n, 0)),
            _const_spec(wq.shape), _const_spec(wo.shape), _const_spec(wr.shape), _const_spec(br.shape),
            _const_spec(lng.shape), _const_spec(lnb.shape),
        ],
        out_specs=pl.BlockSpec((1, tq, D_WIDE), lambda n, i: (n, i, 0)),
        out_shape=jax.ShapeDtypeStruct((nb, t, D_WIDE), F32),
        compiler_params=_params("parallel", "arbitrary"),
        name="attn_prompt",
    )(x, k, v, wq, wo, wr, br, lng, lnb)


def _att_row(half, head, t):
    return (half * MEM_HEADS + head) * DEC_SEQ + t


def _attn_sample_kernel(x_ref, k_ref, v_ref, wq_ref, wo_ref, wr_ref, br_ref, lng_ref, lnb_ref, o_ref,
                        q_scr, qp_scr, op_scr, *, seq_blk, n_blk):
    i = pl.program_id(0)
    halves = MEM_HEAD_DIM // LANES
    blocks = [(t, hh, lt) for t in range(DEC_SEQ) for hh in range(MEM_HEADS) for lt in range(halves)]

    @pl.when(i == 0)
    def _():
        q_scr[...] = _dot(x_ref[...], wq_ref[...])
        for t, hh, lt in blocks:
            col = hh * MEM_HEAD_DIM + lt * LANES
            qp_scr[pl.ds(_att_row(lt, hh, t), DEC_BATCH, stride=ATT_ROWS), :] = (
                q_scr[t * DEC_BATCH:(t + 1) * DEC_BATCH, col:col + LANES])

    half_rows = ATT_ROWS // halves
    col = lax.broadcasted_iota(jnp.int32, (half_rows, KV_ROWS), 1)
    row_head = lax.shift_right_logical(lax.broadcasted_iota(jnp.int32, (half_rows, KV_ROWS), 0),
                                       DEC_SEQ.bit_length() - 1)
    col_cls = col & (halves * MEM_HEADS - 1)
    match0 = col_cls == row_head
    match1 = col_cls == row_head + MEM_HEADS
    scale = MEM_HEAD_DIM ** -0.5

    def body(j, carry):
        n = i * seq_blk + j
        r0 = pl.multiple_of(n * ATT_ROWS, ATT_ROWS)
        qp = qp_scr[pl.ds(r0, ATT_ROWS), :].astype(BF16)
        s = lax.dot_general(qp, k_ref[0, j].astype(BF16), (((1,), (1,)), ((), ())),
                            preferred_element_type=F32)
        part = jnp.where(match0, s[:half_rows], 0.0) + pltpu.roll(
            jnp.where(match1, s[half_rows:], 0.0), KV_ROWS - MEM_HEADS, axis=1)
        sv = jnp.where(match0, part * scale, ROUTER_MASKED)
        e = jnp.exp(sv - jnp.max(sv, axis=-1, keepdims=True))
        p = e / jnp.sum(e, axis=-1, keepdims=True)
        pp = jnp.concatenate([p, pltpu.roll(p, MEM_HEADS, axis=1)], axis=0).astype(BF16)
        op_scr[pl.ds(r0, ATT_ROWS), :] = jnp.dot(pp, v_ref[0, j].astype(BF16), preferred_element_type=F32)
        return carry

    lax.fori_loop(0, seq_blk, body, 0, unroll=True)

    @pl.when(i == n_blk - 1)
    def _():
        for t, hh, lt in blocks:
            col = hh * MEM_HEAD_DIM + lt * LANES
            q_scr[t * DEC_BATCH:(t + 1) * DEC_BATCH, col:col + LANES] = (
                op_scr[pl.ds(_att_row(lt, hh, t), DEC_BATCH, stride=ATT_ROWS), :])
        x = x_ref[...]
        y = _dot(q_scr[...], wo_ref[...])
        h2 = _layer_norm(DN_ALPHA * x + y, lng_ref[...], lnb_ref[...])
        o_ref[:, :D_MODEL] = h2
        o_ref[:, D_MODEL:] = _route(_dot(h2, wr_ref[...]) + br_ref[...])


def _attn_sample(x, k, v, layer, wq, wo, wr, br, lng, lnb, *, seq_blk):
    m = x.shape[0]
    n_blk = DEC_BATCH // seq_blk
    kern = functools.partial(_attn_sample_kernel, seq_blk=seq_blk, n_blk=n_blk)
    kv_spec = pl.BlockSpec((1, seq_blk, KV_ROWS, LANES), lambda i: (layer, i, 0, 0))
    return pl.pallas_call(
        kern,
        grid=(n_blk,),
        in_specs=[
            _const_spec(x.shape), kv_spec, kv_spec,
            _const_spec(wq.shape), _const_spec(wo.shape), _const_spec(wr.shape), _const_spec(br.shape),
            _const_spec(lng.shape), _const_spec(lnb.shape),
        ],
        out_specs=_const_spec((m, D_WIDE)),
        out_shape=jax.ShapeDtypeStruct((m, D_WIDE), F32),
        scratch_shapes=[pltpu.VMEM((m, D_MODEL), F32),
                        pltpu.VMEM((DEC_BATCH * ATT_ROWS, LANES), F32),
                        pltpu.VMEM((DEC_BATCH * ATT_ROWS, LANES), F32)],
        compiler_params=_params("arbitrary"),
        name="attn_sample",
    )(x, k, v, wq, wo, wr, br, lng, lnb)


def _route(logits):
    lane = lax.broadcasted_iota(jnp.int32, logits.shape, 1)
    lane_f = lane.astype(F32)
    neg = ROUTER_MASKED
    is_grp = lane < N_GROUPS
    gl = jnp.where(is_grp, logits, neg)
    gmax = jnp.max(gl, axis=-1, keepdims=True)
    gsel = jnp.min(jnp.where(gl == gmax, lane_f, float(ROUTER_LANES)), axis=-1, keepdims=True)
    g_w = 1.0 / jnp.sum(jnp.exp(gl - gmax), axis=-1, keepdims=True)
    e_idx = lane - N_GROUPS
    e_grp = lax.shift_right_arithmetic(e_idx, 2).astype(F32)
    in_grp = (e_idx >= 0) & (e_idx < N_EXPERTS) & (e_grp == gsel)
    el = jnp.where(in_grp, logits, neg)
    m1 = jnp.max(el, axis=-1, keepdims=True)
    i1 = jnp.min(jnp.where(el == m1, lane_f, float(ROUTER_LANES)), axis=-1, keepdims=True)
    el2 = jnp.where(lane_f == i1, neg, el)
    m2 = jnp.max(el2, axis=-1, keepdims=True)
    i2 = jnp.min(jnp.where(el2 == m2, lane_f, float(ROUTER_LANES)), axis=-1, keepdims=True)
    e2 = jnp.exp(m2 - m1)
    den = 1.0 + e2
    w1 = (1.0 / den) * g_w
    w2 = (e2 / den) * g_w
    first_lane = N_GROUPS + EXPERTS_PER_GROUP * gsel
    a = jnp.minimum(i1, i2) - first_lane
    b = jnp.maximum(i1, i2) - first_lane
    cls = gsel * len(PAIRS)
    for j, (pa, pb) in enumerate(PAIRS):
        cls = cls + jnp.where((a == pa) & (b == pb), float(j), 0.0)
    return (jnp.where(lane_f == i1, w1, 0.0) + jnp.where(lane_f == i2, w2, 0.0)
            + jnp.where(lane == CLASS_LANE, cls, 0.0))


def _gate_column(gates, expert):
    lane = lax.broadcasted_iota(jnp.int32, gates.shape, 1)
    return jnp.sum(jnp.where(lane == expert + N_GROUPS, gates, 0.0), axis=-1, keepdims=True)


def _expert_hidden(xb, wg, wu, gate):
    hg = jnp.dot(xb, wg, preferred_element_type=F32)
    hu = jnp.dot(xb, wu, preferred_element_type=F32)
    return (jax.nn.silu(hg) * hu * gate).astype(BF16)


def _moe_dense_kernel(x_ref, wg_ref, wu_ref, wd_ref, lng_ref, lnb_ref, o_ref, acc_scr):
    e = pl.program_id(1)

    @pl.when(e == 0)
    def _():
        acc_scr[...] = jnp.zeros_like(acc_scr)

    xb = x_ref[:, :D_MODEL].astype(BF16)
    hid = _expert_hidden(xb, wg_ref[0, 0].astype(BF16), wu_ref[0, 0].astype(BF16),
                         _gate_column(x_ref[:, D_MODEL:], e))
    acc_scr[...] += jnp.dot(hid, wd_ref[0, 0].astype(BF16), preferred_element_type=F32)

    @pl.when(e == N_EXPERTS - 1)
    def _():
        o_ref[...] = _layer_norm(DN_ALPHA * x_ref[:, :D_MODEL] + acc_scr[...], lng_ref[...], lnb_ref[...])


def _moe_dense(x, layer, wg, wu, wd, lng, lnb, *, tm):
    m = x.shape[0]
    return pl.pallas_call(
        _moe_dense_kernel,
        grid=(m // tm, N_EXPERTS),
        in_specs=[
            pl.BlockSpec((tm, D_WIDE), lambda i, e: (i, 0)),
            pl.BlockSpec((1, 1, D_MODEL, D_EXPERT), lambda i, e: (layer, e, 0, 0)),
            pl.BlockSpec((1, 1, D_MODEL, D_EXPERT), lambda i, e: (layer, e, 0, 0)),
            pl.BlockSpec((1, 1, D_EXPERT, D_MODEL), lambda i, e: (layer, e, 0, 0)),
            _const_spec(lng.shape), _const_spec(lnb.shape),
        ],
        out_specs=pl.BlockSpec((tm, D_MODEL), lambda i, e: (i, 0)),
        out_shape=jax.ShapeDtypeStruct((m, D_MODEL), F32),
        scratch_shapes=[pltpu.VMEM((tm, D_MODEL), F32)],
        compiler_params=_params("parallel", "arbitrary"),
        name="moe_dense",
    )(x, wg, wu, wd, lng, lnb)


FLAG_ACTIVE, FLAG_FIRST, FLAG_LAST, FLAG_NEW_A, FLAG_NEW_B = 1, 2, 4, 8, 16


def _moe_sparse_kernel(src_ref, tile_ref, ea_ref, eb_ref, lo_ref, hi_ref, flag_ref,
                       x_hbm, wga_ref, wua_ref, wda_ref, wgb_ref, wub_ref, wdb_ref, lng_ref, lnb_ref,
                       out_hbm, xbuf, obuf, acc, wg_s, wu_s, wd_s, gsem, ssem, *, tm, n_tiles, n_items):
    i = pl.program_id(0)
    t = tile_ref[i]
    slot = t % 2
    flags = flag_ref[i]

    def start_gather(tile, s):
        for r in range(tm):
            tok = src_ref[tile * tm + r]
            pltpu.make_async_copy(x_hbm.at[pl.ds(tok, 1)], xbuf.at[s, pl.ds(r, 1)], gsem.at[s]).start()

    def start_scatter(tile, s):
        for r in range(tm):
            tok = src_ref[tile * tm + r]
            pltpu.make_async_copy(obuf.at[s, pl.ds(r, 1)], out_hbm.at[pl.ds(tok, 1)],
                                  ssem.at[s]).start(priority=r % 2)

    def for_slot(value, fn):
        for s in (0, 1):
            pl.when(value == s)(functools.partial(fn, s))

    def wait_gather(s):
        pltpu.make_async_copy(x_hbm.at[pl.ds(0, tm)], xbuf.at[s], gsem.at[s]).wait()

    def wait_scatter(s):
        pltpu.make_async_copy(obuf.at[s], out_hbm.at[pl.ds(0, tm)], ssem.at[s]).wait()

    @pl.when(i == 0)
    def _():
        start_gather(0, 0)

    @pl.when((flags & FLAG_FIRST) != 0)
    def _():
        @pl.when(t + 1 < n_tiles)
        def _():
            for_slot(1 - slot, lambda s: start_gather(t + 1, s))
        wait_gather(slot)
        acc[...] = jnp.zeros_like(acc)

    @pl.when((flags & FLAG_NEW_A) != 0)
    def _():
        wg_s[0] = wga_ref[0, 0].astype(BF16)
        wu_s[0] = wua_ref[0, 0].astype(BF16)
        wd_s[0] = wda_ref[0, 0].astype(BF16)

    @pl.when((flags & FLAG_NEW_B) != 0)
    def _():
        wg_s[1] = wgb_ref[0, 0].astype(BF16)
        wu_s[1] = wub_ref[0, 0].astype(BF16)
        wd_s[1] = wdb_ref[0, 0].astype(BF16)

    @pl.when((flags & FLAG_ACTIVE) != 0)
    def _():
        xb = xbuf[slot, :, :D_MODEL].astype(BF16)
        gates = xbuf[slot, :, D_MODEL:]
        row = t * tm + lax.broadcasted_iota(jnp.int32, (tm, 1), 0)
        in_class = (row >= lo_ref[i]) & (row < hi_ref[i])
        gate_a = jnp.where(in_class, _gate_column(gates, ea_ref[i]), 0.0)
        gate_b = jnp.where(in_class, _gate_column(gates, eb_ref[i]), 0.0)
        hid_a = _expert_hidden(xb, wg_s[0], wu_s[0], gate_a)
        hid_b = _expert_hidden(xb, wg_s[1], wu_s[1], gate_b)
        acc[...] += (jnp.dot(hid_a, wd_s[0], preferred_element_type=F32)
                     + jnp.dot(hid_b, wd_s[1], preferred_element_type=F32))

    @pl.when((flags & FLAG_LAST) != 0)
    def _():
        @pl.when(t >= 2)
        def _():
            wait_scatter(slot)
        obuf[slot] = _layer_norm(DN_ALPHA * xbuf[slot, :, :D_MODEL] + acc[...], lng_ref[...], lnb_ref[...])
        for_slot(slot, lambda s: start_scatter(t, s))

    @pl.when(i == n_items - 1)
    def _():
        wait_scatter(0)
        wait_scatter(1)


def _moe_schedule(cls, *, tm):
    m = cls.shape[0]
    n_tiles = m // tm
    n_items = n_tiles + N_CLASSES - 1
    i32 = jnp.int32
    src = jnp.argsort(cls, stable=True).astype(i32)
    counts = jnp.sum((cls[:, None] == jnp.arange(N_CLASSES, dtype=i32)[None, :]).astype(i32), axis=0)
    ends = jnp.cumsum(counts)
    starts = ends - counts
    def count(mask):
        return jnp.sum(mask.astype(i32), axis=-1)

    def pick(onehot, values):
        return jnp.sum(jnp.where(onehot, values[None, :], 0), axis=-1)

    tile_lo = jnp.arange(n_tiles, dtype=i32) * tm
    first_c = count(ends[None, :] <= tile_lo[:, None])
    last_c = count(starts[None, :] < (tile_lo + tm)[:, None]) - 1
    per_tile = last_c - first_c + 1
    item_end = jnp.cumsum(per_tile)
    idx = jnp.arange(n_items, dtype=i32)
    tile = jnp.minimum(count(item_end[None, :] <= idx[:, None]), n_tiles - 1)
    in_tile = tile[:, None] == jnp.arange(n_tiles, dtype=i32)[None, :]
    end = pick(in_tile, item_end)
    begin = end - pick(in_tile, per_tile)
    active = idx < item_end[-1]
    c = jnp.where(active, pick(in_tile, first_c) + idx - begin, last_c[n_tiles - 1])
    in_class = c[:, None] == jnp.arange(N_CLASSES, dtype=i32)[None, :]
    class_a = [g * EXPERTS_PER_GROUP + pa for g in range(N_GROUPS) for pa, _ in PAIRS]
    class_b = [g * EXPERTS_PER_GROUP + pb for g in range(N_GROUPS) for _, pb in PAIRS]
    ea = pick(in_class, jnp.asarray(class_a, dtype=i32))
    eb = pick(in_class, jnp.asarray(class_b, dtype=i32))
    lo = jnp.where(active, pick(in_class, starts), 0)
    hi = jnp.where(active, pick(in_class, ends), 0)
    new_a = jnp.concatenate([jnp.ones((1,), bool), ea[1:] != ea[:-1]])
    new_b = jnp.concatenate([jnp.ones((1,), bool), eb[1:] != eb[:-1]])
    flags = (active * FLAG_ACTIVE + (active & (idx == begin)) * FLAG_FIRST
             + (active & (idx == end - 1)) * FLAG_LAST + new_a * FLAG_NEW_A + new_b * FLAG_NEW_B)
    return src, tile, ea, eb, lo.astype(i32), hi.astype(i32), flags.astype(i32)


def _moe_sparse(x, layer, wg, wu, wd, lng, lnb, *, tm):
    m = x.shape[0]
    n_tiles = m // tm
    n_items = n_tiles + N_CLASSES - 1
    cls = x[:, D_MODEL + CLASS_LANE].astype(jnp.int32)
    tables = _moe_schedule(cls, tm=tm)
    kern = functools.partial(_moe_sparse_kernel, tm=tm, n_tiles=n_tiles, n_items=n_items)

    def w_spec(shape, which):
        return pl.BlockSpec((1, 1) + shape, lambda i, src, tile, ea, eb, lo, hi, fl: (layer, (ea, eb)[which][i], 0, 0))

    up, down = (D_MODEL, D_EXPERT), (D_EXPERT, D_MODEL)
    grid_spec = pltpu.PrefetchScalarGridSpec(
        num_scalar_prefetch=len(tables),
        grid=(n_items,),
        in_specs=[
            pl.BlockSpec(memory_space=pl.ANY),
            w_spec(up, 0), w_spec(up, 0), w_spec(down, 0), w_spec(up, 1), w_spec(up, 1), w_spec(down, 1),
            _const_spec(lng.shape), _const_spec(lnb.shape),
        ],
        out_specs=pl.BlockSpec(memory_space=pl.ANY),
        scratch_shapes=[
            pltpu.VMEM((2, tm, D_WIDE), F32), pltpu.VMEM((2, tm, D_MODEL), F32), pltpu.VMEM((tm, D_MODEL), F32),
            pltpu.VMEM((2,) + up, BF16), pltpu.VMEM((2,) + up, BF16), pltpu.VMEM((2,) + down, BF16),
            pltpu.SemaphoreType.DMA((2,)), pltpu.SemaphoreType.DMA((2,)),
        ],
    )
    return pl.pallas_call(
        kern,
        grid_spec=grid_spec,
        out_shape=jax.ShapeDtypeStruct((m, D_MODEL), F32),
        compiler_params=_params("arbitrary"),
        name="moe_sparse",
    )(*tables, x, wg, wu, wd, wg, wu, wd, lng, lnb)


def _row(v):
    return v.reshape(1, -1)


def _to_time_major(s):
    return jnp.transpose(s, (1, 0, 2)).reshape(1, -1, s.shape[-1])


def _from_time_major(s, steps):
    return jnp.transpose(s.reshape(steps, DEC_BATCH, s.shape[-1]), (1, 0, 2))


def _kv_lane_view(cache):
    d, n = cache.shape[:2]
    halves = MEM_HEAD_DIM // LANES
    v = cache.reshape(d, n, N_MEM, MEM_HEADS, halves, LANES)
    return jnp.transpose(v, (0, 1, 2, 4, 3, 5)).reshape(d, n, KV_ROWS, LANES)


def _kv_from_lane_view(view):
    d, n = view.shape[:2]
    halves = MEM_HEAD_DIM // LANES
    v = view.reshape(d, n, N_MEM, halves, MEM_HEADS, LANES)
    return jnp.transpose(v, (0, 1, 2, 4, 3, 5)).reshape(d, n, N_MEM, MEM_HEADS, MEM_HEAD_DIM)


def kernel(x_prompt, x_sample, cache_mem_k, cache_mem_v, state_pool, state_conv_c, state_conv_d, mem_prompt,
           w_in_ab, ln_v_g, ln_v_b, w_spatial, b_spatial, w_pool, pool_scale, w_out_ab,
           w_in_cd, conv_c_w, conv_c_b, ln_c_g, ln_c_b, conv_d_w, w_out_cd,
           w_q, w_k, w_v, w_o, w_group, b_group, w_router, b_router, w_gate, w_up, w_down, ln_g, ln_b):
    hp = x_prompt
    hs = _to_time_major(x_sample)
    mem_flat = mem_prompt.reshape(BATCH * N_MEM, D_MODEL)
    cache_k = _kv_lane_view(cache_mem_k)
    cache_v = _kv_lane_view(cache_mem_v)
    kp, vp, kp_view, vp_view = _kv_proj(mem_flat, w_k.astype(BF16), w_v.astype(BF16))

    pool_p, pool_s, chunk_v_s = [], [], []
    conv_c_p, conv_c_s, conv_d_p, conv_d_s = [], [], [], []

    for l in range(DEPTH):
        i = l // 2
        lng0, lnb0 = _row(ln_g[l, 0]), _row(ln_b[l, 0])
        if l % 2 == 0:
            bias = jnp.repeat(b_spatial[i].T, A_HEAD_DIM, axis=1)
            small = jnp.repeat(
                jnp.transpose(w_spatial[i][:, :DEC_SEQ, :DEC_SEQ], (1, 2, 0)).reshape(DEC_SEQ * DEC_SEQ, A_HEADS),
                A_HEAD_DIM, axis=1)
            common = (w_in_ab[i].astype(BF16), _row(ln_v_g[i]), _row(ln_v_b[i]))
            tail = (w_pool[i].astype(BF16), _row(pool_scale[i]), w_out_ab[i].astype(BF16), lng0, lnb0)
            hp, hist_p = _mixer_ab(hp, jnp.zeros((BATCH, POOL_HIST, D_B), F32), *common, w_spatial[i], bias,
                                   *tail, seqs=1, tt=512, pos0=0, with_v=False)
            hs, v_rows, hist_s = _mixer_ab(hs, _to_time_major(state_pool[i]), *common, small, bias[:DEC_SEQ],
                                           *tail, seqs=DEC_BATCH, tt=DEC_SEQ, pos0=PAST_LEN, with_v=True)
            pool_p.append(hist_p)
            pool_s.append(_from_time_major(hist_s, POOL_HIST))
            chunk_v_s.append(_from_time_major(v_rows, DEC_SEQ))
        else:
            wts = (w_in_cd[i].astype(BF16), conv_c_w[i], _row(conv_c_b[i]), _row(ln_c_g[i]), _row(ln_c_b[i]),
                   conv_d_w[i], w_out_cd[i].astype(BF16), lng0, lnb0)
            hp, hc_p, hd_p = _mixer_cd(hp, jnp.zeros((BATCH, CONV_C - 1, D_C), F32),
                                       jnp.zeros((BATCH, CONV_D - 1, D_D), F32), *wts, seqs=1, tt=512)
            hs, hc_s, hd_s = _mixer_cd(hs, _to_time_major(state_conv_c[i]), _to_time_major(state_conv_d[i]),
                                       *wts, seqs=DEC_BATCH, tt=DEC_SEQ)
            conv_c_p.append(hc_p)
            conv_d_p.append(hd_p)
            conv_c_s.append(_from_time_major(hc_s, CONV_C - 1))
            conv_d_s.append(_from_time_major(hd_s, CONV_D - 1))

        lng1, lnb1 = _row(ln_g[l, 1]), _row(ln_b[l, 1])
        wq, wo = w_q[l].astype(BF16), w_o[l].astype(BF16)
        wr = jnp.zeros((D_MODEL, ROUTER_LANES), F32)
        wr = wr.at[:, :N_GROUPS].set(w_group[l]).at[:, N_GROUPS:N_GROUPS + N_EXPERTS].set(w_router[l])
        br = jnp.zeros((1, ROUTER_LANES), F32)
        br = br.at[0, :N_GROUPS].set(b_group[l]).at[0, N_GROUPS:N_GROUPS + N_EXPERTS].set(b_router[l])
        wr = wr.astype(BF16)
        hp_wide = _attn_prompt(hp, kp, vp, l, wq, wo, wr, br, lng1, lnb1, tq=512)
        hs_wide = _attn_sample(hs[0], cache_k, cache_v, l, wq, wo, wr, br, lng1, lnb1, seq_blk=4)

        moe_w = (l, w_gate, w_up, w_down, _row(ln_g[l, 2]), _row(ln_b[l, 2]))
        hp = _moe_sparse(hp_wide.reshape(BATCH * SEQ, D_WIDE), *moe_w, tm=256).reshape(BATCH, SEQ, D_MODEL)
        hs = _moe_dense(hs_wide, *moe_w, tm=DEC_SEQ * DEC_BATCH)[None]

    y_sample = _from_time_major(hs, DEC_SEQ)
    return (hp, y_sample, _kv_from_lane_view(kp_view), _kv_from_lane_view(vp_view), jnp.stack(pool_p),
            jnp.stack(conv_c_p),
            jnp.stack(conv_d_p), jnp.stack(chunk_v_s), jnp.stack(pool_s), jnp.stack(conv_c_s),
            jnp.stack(conv_d_s))
```

```python
import functools
from typing import Callable, NamedTuple, Optional

import jax
import jax.numpy as jnp
from jax import lax
from jax.experimental import pallas as pl
from jax.experimental.pallas import tpu as pltpu

D_MODEL = 1024
BATCH = 8
SEQ = 2048
DEPTH = 4
DEC_BATCH = 128
DEC_SEQ = 4
PAST_LEN = 16384

CHUNK = 128
A_HEADS = 4
D_A = D_MODEL // 2
A_HEAD_DIM = D_A // A_HEADS
POOL_WINDOWS = (2, 4, 8, 16)
B_GROUPS = len(POOL_WINDOWS)
D_B = D_MODEL // 2
B_GROUP_DIM = D_B // B_GROUPS
POOL_HIST = max(POOL_WINDOWS) - 1
D_C = D_MODEL // 2
CONV_C = 31
D_D = D_MODEL // 2
CONV_D = 3
N_MEM = 256
MEM_HEADS = 4
MEM_HEAD_DIM = D_MODEL // MEM_HEADS
N_GROUPS = 4
EXPERTS_PER_GROUP = 4
N_EXPERTS = N_GROUPS * EXPERTS_PER_GROUP
D_EXPERT = 512
DN_ALPHA = (2 * DEPTH) ** 0.25
LN_EPS = 1e-5
PROMPT_ROWS = BATCH * SEQ
SAMPLE_ROWS = DEC_BATCH * DEC_SEQ
TOTAL_ROWS = PROMPT_ROWS + SAMPLE_ROWS

LANES = 128
SUBLANES = 8
POOL_HIST_PAD = 16
CONV_C_HIST_PAD = 32
CONV_D_HIST_PAD = 8
ROUTER_LANES = 128
ROUTER_MASKED = -1e30
CLASS_LANE = N_GROUPS + N_EXPERTS
PAIRS = ((0, 1), (0, 2), (1, 2), (1, 3), (0, 3), (2, 3))
N_CLASSES = N_GROUPS * len(PAIRS)
D_WIDE = D_MODEL + ROUTER_LANES
KV_ROWS = N_MEM * MEM_HEADS * MEM_HEAD_DIM // LANES
ATT_ROWS = KV_ROWS // N_MEM * DEC_SEQ
VMEM_LIMIT_BYTES = 52 * 1024 * 1024

F32 = jnp.float32
BF16 = jnp.bfloat16


def _layer_norm(x, g, b):
    mu = jnp.mean(x, axis=-1, keepdims=True)
    xc = x - mu
    var = jnp.mean(xc * xc, axis=-1, keepdims=True)
    return xc * lax.rsqrt(var + LN_EPS) * g + b


def _dot(a, b):
    return jnp.dot(a.astype(BF16), b.astype(BF16), preferred_element_type=F32)


def _params(*semantics):
    return pltpu.CompilerParams(dimension_semantics=semantics, vmem_limit_bytes=VMEM_LIMIT_BYTES)


def _const_spec(shape):
    nd = len(shape)
    return pl.BlockSpec(shape, lambda *_: (0,) * nd)


class _Layer(NamedTuple):
    stack: jax.Array
    layer: int


def _param_spec(p):
    if isinstance(p, _Layer):
        nd = p.stack.ndim
        return pl.BlockSpec((1,) + p.stack.shape[1:], lambda *_: (p.layer,) + (0,) * (nd - 1))
    return _const_spec(p.shape)


def _param_arg(p):
    return p.stack if isinstance(p, _Layer) else p


class _Slab(NamedTuple):
    array: jax.Array
    row0: int
    n_rows: int
    out_row0: int
    dst: Optional[jax.Array]


class _SlabIO(NamedTuple):
    in_spec: pl.BlockSpec
    out_spec: pl.BlockSpec
    out_shape: jax.ShapeDtypeStruct
    alias_spec: list
    alias_arg: tuple
    aliases: dict
    adapt: Callable


def _slab_io(src, rows, out_width, block_index, *, n_inputs):
    assert src.row0 % rows == 0 and src.out_row0 % rows == 0 and src.n_rows % rows == 0
    in0, out0 = src.row0 // rows, src.out_row0 // rows
    in_spec = pl.BlockSpec((rows, src.array.shape[1]), lambda *g: (in0 + block_index(*g), 0))
    out_spec = pl.BlockSpec((rows, out_width), lambda *g: (out0 + block_index(*g), 0))
    out_shape = jax.ShapeDtypeStruct((TOTAL_ROWS, out_width), F32)
    if src.dst is None:
        return _SlabIO(in_spec, out_spec, out_shape, [], (), {}, lambda kernel: kernel)

    def adapt(kernel):
        return lambda *refs: kernel(*refs[:n_inputs], *refs[n_inputs + 1:])

    return _SlabIO(in_spec, out_spec, out_shape, [pl.BlockSpec(memory_space=pl.ANY)], (src.dst,),
                   {n_inputs: 0}, adapt)


def _mixer_ab_kernel(x_ref, hist_ref, w_in_ref, lnv_g_ref, lnv_b_ref, ws_ref, bs_ref, wpool_ref,
                     pscale_ref, w_out_ref, lng_ref, lnb_ref, *refs, seqs, tt, pos0, with_v):
    if with_v:
        h_out_ref, v_out_ref, hist_out_ref, zbuf = refs
    else:
        h_out_ref, hist_out_ref, zbuf = refs
    t_idx = pl.program_id(1)
    rows = tt * seqs
    hp = POOL_HIST_PAD

    @pl.when(t_idx == 0)
    def _():
        zbuf[pl.ds((hp - POOL_HIST) * seqs, POOL_HIST * seqs), :] = hist_ref[0]

    x = x_ref[...]
    h = _dot(x, w_in_ref[0])
    ua = jax.nn.gelu(h[:, :2 * D_A])
    u = ua[:, :D_A]
    v = _layer_norm(ua[:, D_A:], lnv_g_ref[...], lnv_b_ref[...])
    z = h[:, 2 * D_A:]
    if with_v:
        v_out_ref[0] = v
    zbuf[pl.ds(hp * seqs, rows), :] = z

    if seqs == 1:
        tri = (lax.broadcasted_iota(jnp.int32, (CHUNK, CHUNK), 0)
               >= lax.broadcasted_iota(jnp.int32, (CHUNK, CHUNK), 1))
        w_heads = [jnp.where(tri, ws_ref[hh], 0.0).astype(BF16) for hh in range(A_HEADS)]
        vb = v.astype(BF16)
        chunks = []
        for c in range(tt // CHUNK):
            heads = []
            for hh in range(A_HEADS):
                vc = vb[c * CHUNK:(c + 1) * CHUNK, hh * A_HEAD_DIM:(hh + 1) * A_HEAD_DIM]
                heads.append(jnp.dot(w_heads[hh], vc, preferred_element_type=F32))
            chunks.append(jnp.concatenate(heads, axis=-1) + bs_ref[...])
        mixed = jnp.concatenate(chunks, axis=0)
    else:
        parts = []
        for t in range(tt):
            acc = bs_ref[t:t + 1, :]
            for s in range(t + 1):
                acc = acc + ws_ref[t * tt + s:t * tt + s + 1, :] * v[s * seqs:(s + 1) * seqs, :]
            parts.append(acc)
        mixed = jnp.concatenate(parts, axis=0)
    a_out = u * mixed

    outs = []
    for gi, w in enumerate(POOL_WINDOWS):
        lo, hi = gi * B_GROUP_DIM, (gi + 1) * B_GROUP_DIM
        acc = zbuf[pl.ds(hp * seqs, rows), lo:hi]
        for j in range(1, w):
            acc = acc + zbuf[pl.ds((hp - j) * seqs, rows), lo:hi]
        if pos0 + 1 >= w:
            cnt = float(w)
        else:
            assert seqs == 1
            pos = pos0 + t_idx * tt + lax.broadcasted_iota(jnp.int32, (rows, 1), 0)
            cnt = jnp.minimum(pos + 1, w).astype(F32)
        pooled = acc / cnt - z[:, lo:hi]
        outs.append(_dot(pooled, wpool_ref[0, gi]))
    b_out = jnp.concatenate(outs, axis=-1) * pscale_ref[...]

    y = _dot(jnp.concatenate([a_out, b_out], axis=-1), w_out_ref[0])
    h_out_ref[...] = _layer_norm(DN_ALPHA * x + y, lng_ref[...], lnb_ref[...])

    hist_out_ref[0] = zbuf[pl.ds((tt + hp - POOL_HIST) * seqs, POOL_HIST * seqs), :]
    zbuf[pl.ds(0, hp * seqs), :] = zbuf[pl.ds(tt * seqs, hp * seqs), :]


def _mixer_ab(src, hist, w_in, lnv_g, lnv_b, ws, bs, wpool, pscale, w_out, lng, lnb, *, seqs, tt, pos0, with_v):
    nb = hist.shape[0]
    rows = tt * seqs
    n_t = src.n_rows // (nb * rows)
    kern = functools.partial(_mixer_ab_kernel, seqs=seqs, tt=tt, pos0=pos0, with_v=with_v)
    weights = (w_in, lnv_g, lnv_b, ws, bs, wpool, pscale, w_out, lng, lnb)
    io = _slab_io(src, rows, D_MODEL, lambda n, t: n * n_t + t, n_inputs=2 + len(weights))
    v_spec = pl.BlockSpec((1, rows, D_A), lambda n, t: (n, t, 0))
    hist_spec = pl.BlockSpec((1, POOL_HIST * seqs, D_B), lambda n, t: (n, 0, 0))
    v_shape = jax.ShapeDtypeStruct((nb, n_t * rows, D_A), F32)
    hist_shape = jax.ShapeDtypeStruct((nb, POOL_HIST * seqs, D_B), F32)
    return pl.pallas_call(
        io.adapt(kern),
        grid=(nb, n_t),
        in_specs=[io.in_spec, hist_spec] + [_param_spec(w) for w in weights] + io.alias_spec,
        out_specs=[io.out_spec, v_spec, hist_spec] if with_v else [io.out_spec, hist_spec],
        out_shape=[io.out_shape, v_shape, hist_shape] if with_v else [io.out_shape, hist_shape],
        input_output_aliases=io.aliases,
        scratch_shapes=[pltpu.VMEM(((POOL_HIST_PAD + tt) * seqs, D_B), F32)],
        compiler_params=_params("parallel", "arbitrary"),
        name="mixer_ab",
    )(src.array, hist, *map(_param_arg, weights), *io.alias_arg)


def _shift_rows(taps, hist_pad):
    base = hist_pad - (taps - 1)
    return max((base + k) // SUBLANES * SUBLANES for k in range(taps) if (base + k) % SUBLANES)


def _dwconv(buf, w_ref, out, shifted, *, taps, hist_pad, seqs, rows, chunk):
    base = (hist_pad - (taps - 1)) * seqs
    if shifted is not None:
        span = rows + _shift_rows(taps, hist_pad)
        for b in range(1, SUBLANES):
            shifted[b - 1] = buf[pl.ds(b, span), :]

    def tap(k, r0, n, lanes):
        off = base + k * seqs
        phase = off % SUBLANES
        if shifted is None or phase == 0:
            return buf[pl.ds(off + r0, n), lanes]
        return shifted[phase - 1, pl.ds(off - phase + r0, n), lanes]

    if shifted is None:
        def full_width(i, carry):
            r0 = i * chunk if isinstance(i, int) else pl.multiple_of(i * chunk, chunk)
            acc = w_ref[0:1, :] * tap(0, r0, chunk, slice(None))
            for k in range(1, taps):
                acc = acc + w_ref[k:k + 1, :] * tap(k, r0, chunk, slice(None))
            out[pl.ds(r0, chunk), :] = acc
            return carry

        if seqs % SUBLANES == 0:
            lax.fori_loop(0, rows // chunk, full_width, 0)
        else:
            for i in range(rows // chunk):
                full_width(i, 0)
        return

    for lt in range(buf.shape[1] // LANES):
        lanes = slice(lt * LANES, (lt + 1) * LANES)
        wk = [jnp.broadcast_to(w_ref[k:k + 1, lanes], (SUBLANES, LANES)) for k in range(taps)]

        def body(i, carry, lanes=lanes, wk=wk):
            r0 = pl.multiple_of(i * chunk, chunk)
            for g in range(chunk // SUBLANES):
                rg = r0 + g * SUBLANES
                acc = wk[0] * tap(0, rg, SUBLANES, lanes)
                for k in range(1, taps):
                    acc = acc + wk[k] * tap(k, rg, SUBLANES, lanes)
                out[pl.ds(rg, SUBLANES), lanes] = acc
            return carry

        lax.fori_loop(0, rows // chunk, body, 0)


def _mixer_cd_kernel(x_ref, hist_c_ref, hist_d_ref, w_in_ref, ccw_ref, ccb_ref, lncg_ref, lncb_ref,
                     cdw_ref, w_out_ref, lng_ref, lnb_ref,
                     h_out_ref, hist_c_out_ref, hist_d_out_ref, cbuf, dbuf, cc_scr, cd_scr, *maybe_shifted,
                     seqs, tt):
    shifted = maybe_shifted[0] if maybe_shifted else None
    t_idx = pl.program_id(1)
    rows = tt * seqs
    hc, hd = CONV_C_HIST_PAD, CONV_D_HIST_PAD
    nc, nd = CONV_C - 1, CONV_D - 1

    @pl.when(t_idx == 0)
    def _():
        cbuf[pl.ds((hc - nc) * seqs, nc * seqs), :] = hist_c_ref[0]
        dbuf[pl.ds((hd - nd) * seqs, nd * seqs), :] = hist_d_ref[0]

    x = x_ref[...]
    h = _dot(x, w_in_ref[0])
    glu = h[:, :D_C] * jax.nn.sigmoid(h[:, D_C:2 * D_C])
    o = 2 * D_C
    gate_b = h[:, o:o + D_D]
    gx = h[:, o + D_D:o + 2 * D_D] * h[:, o + 2 * D_D:]
    cbuf[pl.ds(hc * seqs, rows), :] = glu
    dbuf[pl.ds(hd * seqs, rows), :] = gx

    chunk = min(rows, 32)
    _dwconv(cbuf, ccw_ref, cc_scr, shifted, taps=CONV_C, hist_pad=hc, seqs=seqs, rows=rows, chunk=chunk)
    _dwconv(dbuf, cdw_ref, cd_scr, None, taps=CONV_D, hist_pad=hd, seqs=seqs, rows=rows, chunk=chunk)

    c_out = jax.nn.silu(_layer_norm(cc_scr[...] + ccb_ref[...], lncg_ref[...], lncb_ref[...]))
    d_out = gate_b * cd_scr[...]
    y = _dot(jnp.concatenate([c_out, d_out], axis=-1), w_out_ref[0])
    h_out_ref[...] = _layer_norm(DN_ALPHA * x + y, lng_ref[...], lnb_ref[...])

    hist_c_out_ref[0] = cbuf[pl.ds((tt + hc - nc) * seqs, nc * seqs), :]
    hist_d_out_ref[0] = dbuf[pl.ds((tt + hd - nd) * seqs, nd * seqs), :]
    cbuf[pl.ds(0, hc * seqs), :] = cbuf[pl.ds(tt * seqs, hc * seqs), :]
    dbuf[pl.ds(0, hd * seqs), :] = dbuf[pl.ds(tt * seqs, hd * seqs), :]


def _mixer_cd(src, hist_c, hist_d, w_in, ccw, ccb, lncg, lncb, cdw, w_out, lng, lnb, *, seqs, tt):
    nb = hist_c.shape[0]
    rows = tt * seqs
    n_t = src.n_rows // (nb * rows)
    nc, nd = CONV_C - 1, CONV_D - 1
    kern = functools.partial(_mixer_cd_kernel, seqs=seqs, tt=tt)
    weights = (w_in, ccw, ccb, lncg, lncb, cdw, w_out, lng, lnb)
    io = _slab_io(src, rows, D_MODEL, lambda n, t: n * n_t + t, n_inputs=3 + len(weights))
    hist_c_spec = pl.BlockSpec((1, nc * seqs, D_C), lambda n, t: (n, 0, 0))
    hist_d_spec = pl.BlockSpec((1, nd * seqs, D_D), lambda n, t: (n, 0, 0))
    return pl.pallas_call(
        io.adapt(kern),
        grid=(nb, n_t),
        in_specs=[io.in_spec, hist_c_spec, hist_d_spec] + [_param_spec(w) for w in weights] + io.alias_spec,
        out_specs=[io.out_spec, hist_c_spec, hist_d_spec],
        out_shape=[
            io.out_shape,
            jax.ShapeDtypeStruct((nb, nc * seqs, D_C), F32),
            jax.ShapeDtypeStruct((nb, nd * seqs, D_D), F32),
        ],
        input_output_aliases=io.aliases,
        scratch_shapes=[
            pltpu.VMEM(((CONV_C_HIST_PAD + tt) * seqs, D_C), F32),
            pltpu.VMEM(((CONV_D_HIST_PAD + tt) * seqs, D_D), F32),
            pltpu.VMEM((rows, D_C), F32),
            pltpu.VMEM((rows, D_D), F32),
        ] + ([pltpu.VMEM((SUBLANES - 1, rows + _shift_rows(CONV_C, CONV_C_HIST_PAD), D_C), F32)]
             if seqs % SUBLANES else []),
        compiler_params=_params("parallel", "arbitrary"),
        name="mixer_cd",
    )(src.array, hist_c, hist_d, *map(_param_arg, weights), *io.alias_arg)


def _kv_proj_kernel(mem_ref, wk_ref, wv_ref, k_ref, v_ref, kview_ref, vview_ref):
    m = mem_ref[...].astype(BF16)
    halves = MEM_HEAD_DIM // LANES
    for w_ref, o_ref, view_ref in ((wk_ref, k_ref, kview_ref), (wv_ref, v_ref, vview_ref)):
        y = jnp.dot(m, w_ref[0], preferred_element_type=F32)
        o_ref[0] = y
        for hh in range(MEM_HEADS):
            for lt in range(halves):
                col = hh * MEM_HEAD_DIM + lt * LANES
                view_ref[0, 0, pl.ds(lt * MEM_HEADS + hh, N_MEM, stride=halves * MEM_HEADS), :] = (
                    y[:, col:col + LANES])


def _kv_proj(mem, wk, wv):
    rows = jax.ShapeDtypeStruct((DEPTH, BATCH * N_MEM, D_MODEL), F32)
    view = jax.ShapeDtypeStruct((DEPTH, BATCH, KV_ROWS, LANES), F32)
    w_spec = pl.BlockSpec((1, D_MODEL, D_MODEL), lambda l, n: (l, 0, 0))
    return pl.pallas_call(
        _kv_proj_kernel,
        grid=(DEPTH, BATCH),
        in_specs=[pl.BlockSpec((N_MEM, D_MODEL), lambda l, n: (n, 0)), w_spec, w_spec],
        out_specs=[pl.BlockSpec((1, N_MEM, D_MODEL), lambda l, n: (l, n, 0))] * 2
        + [pl.BlockSpec((1, 1, KV_ROWS, LANES), lambda l, n: (l, n, 0, 0))] * 2,
        out_shape=[rows, rows, view, view],
        compiler_params=_params("parallel", "arbitrary"),
        name="kv_proj",
    )(mem, wk, wv)


def _attend(q, k, v):
    scale = MEM_HEAD_DIM ** -0.5
    qb = q.astype(BF16)
    outs = []
    for hh in range(MEM_HEADS):
        lo, hi = hh * MEM_HEAD_DIM, (hh + 1) * MEM_HEAD_DIM
        s = lax.dot_general(qb[:, lo:hi], k[:, lo:hi], (((1,), (1,)), ((), ())),
                            preferred_element_type=F32) * scale
        s = s - jnp.max(s, axis=-1, keepdims=True)
        p = jnp.exp(s)
        p = p / jnp.sum(p, axis=-1, keepdims=True)
        outs.append(jnp.dot(p.astype(BF16), v[:, lo:hi], preferred_element_type=F32))
    return jnp.concatenate(outs, axis=-1)


def _attn_prompt_kernel(x_ref, k_ref, v_ref, wq_ref, wo_ref, wr_ref, br_ref, lng_ref, lnb_ref, o_ref):
    x = x_ref[...]
    q = _dot(x, wq_ref[0])
    o = _attend(q, k_ref[0].astype(BF16), v_ref[0].astype(BF16))
    y = _dot(o, wo_ref[0])
    h2 = _layer_norm(DN_ALPHA * x + y, lng_ref[...], lnb_ref[...])
    o_ref[:, :D_MODEL] = h2
    o_ref[:, D_MODEL:] = _route(_dot(h2, wr_ref[0]) + br_ref[0])


def _attn_prompt(src, k, v, layer, wq, wo, wr, br, lng, lnb, *, tq):
    n_t = src.n_rows // (BATCH * tq)
    weights = (wq, wo, wr, br, lng, lnb)
    io = _slab_io(src, tq, D_WIDE, lambda n, i: n * n_t + i, n_inputs=3 + len(weights))
    kv_spec = pl.BlockSpec((1, N_MEM, D_MODEL), lambda n, i: (layer, n, 0))
    return pl.pallas_call(
        io.adapt(_attn_prompt_kernel),
        grid=(BATCH, n_t),
        in_specs=[io.in_spec, kv_spec, kv_spec] + [_param_spec(w) for w in weights] + io.alias_spec,
        out_specs=io.out_spec,
        out_shape=io.out_shape,
        input_output_aliases=io.aliases,
        compiler_params=_params("parallel", "arbitrary"),
        name="attn_prompt",
    )(src.array, k, v, *map(_param_arg, weights), *io.alias_arg)


def _att_row(half, head, t):
    return (half * MEM_HEADS + head) * DEC_SEQ + t


def _attn_sample_kernel(x_ref, k_ref, v_ref, wq_ref, wo_ref, wr_ref, br_ref, lng_ref, lnb_ref, o_ref,
                        q_scr, qp_scr, op_scr, *, seq_blk, n_blk):
    i = pl.program_id(0)
    halves = MEM_HEAD_DIM // LANES
    blocks = [(t, hh, lt) for t in range(DEC_SEQ) for hh in range(MEM_HEADS) for lt in range(halves)]

    @pl.when(i == 0)
    def _():
        q_scr[...] = _dot(x_ref[...], wq_ref[0])
        for t, hh, lt in blocks:
            col = hh * MEM_HEAD_DIM + lt * LANES
            qp_scr[pl.ds(_att_row(lt, hh, t), DEC_BATCH, stride=ATT_ROWS), :] = (
                q_scr[t * DEC_BATCH:(t + 1) * DEC_BATCH, col:col + LANES])

    half_rows = ATT_ROWS // halves
    col = lax.broadcasted_iota(jnp.int32, (half_rows, KV_ROWS), 1)
    row_head = lax.shift_right_logical(lax.broadcasted_iota(jnp.int32, (half_rows, KV_ROWS), 0),
                                       DEC_SEQ.bit_length() - 1)
    col_cls = col & (halves * MEM_HEADS - 1)
    match0 = col_cls == row_head
    match1 = col_cls == row_head + MEM_HEADS
    scale = MEM_HEAD_DIM ** -0.5

    def body(j, carry):
        n = i * seq_blk + j
        r0 = pl.multiple_of(n * ATT_ROWS, ATT_ROWS)
        qp = qp_scr[pl.ds(r0, ATT_ROWS), :].astype(BF16)
        s = lax.dot_general(qp, k_ref[0, j].astype(BF16), (((1,), (1,)), ((), ())),
                            preferred_element_type=F32)
        part = jnp.where(match0, s[:half_rows], 0.0) + pltpu.roll(
            jnp.where(match1, s[half_rows:], 0.0), KV_ROWS - MEM_HEADS, axis=1)
        sv = jnp.where(match0, part * scale, ROUTER_MASKED)
        e = jnp.exp(sv - jnp.max(sv, axis=-1, keepdims=True))
        p = e / jnp.sum(e, axis=-1, keepdims=True)
        pp = jnp.concatenate([p, pltpu.roll(p, MEM_HEADS, axis=1)], axis=0).astype(BF16)
        op_scr[pl.ds(r0, ATT_ROWS), :] = jnp.dot(pp, v_ref[0, j].astype(BF16), preferred_element_type=F32)
        return carry

    lax.fori_loop(0, seq_blk, body, 0, unroll=True)

    @pl.when(i == n_blk - 1)
    def _():
        for t, hh, lt in blocks:
            col = hh * MEM_HEAD_DIM + lt * LANES
            q_scr[t * DEC_BATCH:(t + 1) * DEC_BATCH, col:col + LANES] = (
                op_scr[pl.ds(_att_row(lt, hh, t), DEC_BATCH, stride=ATT_ROWS), :])
        x = x_ref[...]
        y = _dot(q_scr[...], wo_ref[0])
        h2 = _layer_norm(DN_ALPHA * x + y, lng_ref[...], lnb_ref[...])
        o_ref[:, :D_MODEL] = h2
        o_ref[:, D_MODEL:] = _route(_dot(h2, wr_ref[0]) + br_ref[0])


def _attn_sample(src, k, v, layer, wq, wo, wr, br, lng, lnb, *, seq_blk):
    m = src.n_rows
    n_blk = DEC_BATCH // seq_blk
    kern = functools.partial(_attn_sample_kernel, seq_blk=seq_blk, n_blk=n_blk)
    weights = (wq, wo, wr, br, lng, lnb)
    io = _slab_io(src, m, D_WIDE, lambda i: 0, n_inputs=3 + len(weights))
    kv_spec = pl.BlockSpec((1, seq_blk, KV_ROWS, LANES), lambda i: (layer, i, 0, 0))
    return pl.pallas_call(
        io.adapt(kern),
        grid=(n_blk,),
        in_specs=[io.in_spec, kv_spec, kv_spec] + [_param_spec(w) for w in weights] + io.alias_spec,
        out_specs=io.out_spec,
        out_shape=io.out_shape,
        input_output_aliases=io.aliases,
        scratch_shapes=[pltpu.VMEM((m, D_MODEL), F32),
                        pltpu.VMEM((DEC_BATCH * ATT_ROWS, LANES), F32),
                        pltpu.VMEM((DEC_BATCH * ATT_ROWS, LANES), F32)],
        compiler_params=_params("arbitrary"),
        name="attn_sample",
    )(src.array, k, v, *map(_param_arg, weights), *io.alias_arg)


def _route(logits):
    lane = lax.broadcasted_iota(jnp.int32, logits.shape, 1)
    lane_f = lane.astype(F32)
    neg = ROUTER_MASKED
    is_grp = lane < N_GROUPS
    gl = jnp.where(is_grp, logits, neg)
    gmax = jnp.max(gl, axis=-1, keepdims=True)
    gsel = jnp.min(jnp.where(gl == gmax, lane_f, float(ROUTER_LANES)), axis=-1, keepdims=True)
    g_w = 1.0 / jnp.sum(jnp.exp(gl - gmax), axis=-1, keepdims=True)
    e_idx = lane - N_GROUPS
    e_grp = lax.shift_right_arithmetic(e_idx, 2).astype(F32)
    in_grp = (e_idx >= 0) & (e_idx < N_EXPERTS) & (e_grp == gsel)
    el = jnp.where(in_grp, logits, neg)
    m1 = jnp.max(el, axis=-1, keepdims=True)
    i1 = jnp.min(jnp.where(el == m1, lane_f, float(ROUTER_LANES)), axis=-1, keepdims=True)
    el2 = jnp.where(lane_f == i1, neg, el)
    m2 = jnp.max(el2, axis=-1, keepdims=True)
    i2 = jnp.min(jnp.where(el2 == m2, lane_f, float(ROUTER_LANES)), axis=-1, keepdims=True)
    e2 = jnp.exp(m2 - m1)
    den = 1.0 + e2
    w1 = (1.0 / den) * g_w
    w2 = (e2 / den) * g_w
    first_lane = N_GROUPS + EXPERTS_PER_GROUP * gsel
    a = jnp.minimum(i1, i2) - first_lane
    b = jnp.maximum(i1, i2) - first_lane
    cls = gsel * len(PAIRS)
    for j, (pa, pb) in enumerate(PAIRS):
        cls = cls + jnp.where((a == pa) & (b == pb), float(j), 0.0)
    return (jnp.where(lane_f == i1, w1, 0.0) + jnp.where(lane_f == i2, w2, 0.0)
            + jnp.where(lane == CLASS_LANE, cls, 0.0))


def _gate_column(gates, expert):
    lane = lax.broadcasted_iota(jnp.int32, gates.shape, 1)
    return jnp.sum(jnp.where(lane == expert + N_GROUPS, gates, 0.0), axis=-1, keepdims=True)


def _expert_hidden(xb, wg, wu, gate):
    hg = jnp.dot(xb, wg, preferred_element_type=F32)
    hu = jnp.dot(xb, wu, preferred_element_type=F32)
    return (jax.nn.silu(hg) * hu * gate).astype(BF16)


FLAG_ACTIVE, FLAG_FIRST, FLAG_LAST, FLAG_NEW_A, FLAG_NEW_B = 1, 2, 4, 8, 16


def _moe_sparse_kernel(src_ref, tile_ref, ea_ref, eb_ref, lo_ref, hi_ref, flag_ref,
                       x_hbm, wga_ref, wua_ref, wda_ref, wgb_ref, wub_ref, wdb_ref, lng_ref, lnb_ref,
                       out_hbm, xbuf, obuf, acc, wg_s, wu_s, wd_s, gsem, ssem, *, tm, n_tiles, n_items):
    i = pl.program_id(0)
    t = tile_ref[i]
    slot = t % 2
    flags = flag_ref[i]

    def start_gather(tile, s):
        for r in range(tm):
            tok = src_ref[tile * tm + r]
            pltpu.make_async_copy(x_hbm.at[pl.ds(tok, 1)], xbuf.at[s, pl.ds(r, 1)], gsem.at[s]).start()

    def start_scatter(tile, s):
        for r in range(tm):
            tok = src_ref[tile * tm + r]
            pltpu.make_async_copy(obuf.at[s, pl.ds(r, 1)], out_hbm.at[pl.ds(tok, 1)],
                                  ssem.at[s]).start(priority=r % 2)

    def for_slot(value, fn):
        for s in (0, 1):
            pl.when(value == s)(functools.partial(fn, s))

    def wait_gather(s):
        pltpu.make_async_copy(x_hbm.at[pl.ds(0, tm)], xbuf.at[s], gsem.at[s]).wait()

    def wait_scatter(s):
        pltpu.make_async_copy(obuf.at[s], out_hbm.at[pl.ds(0, tm)], ssem.at[s]).wait()

    @pl.when(i == 0)
    def _():
        start_gather(0, 0)

    @pl.when((flags & FLAG_FIRST) != 0)
    def _():
        @pl.when(t + 1 < n_tiles)
        def _():
            for_slot(1 - slot, lambda s: start_gather(t + 1, s))
        wait_gather(slot)
        acc[...] = jnp.zeros_like(acc)

    @pl.when((flags & FLAG_NEW_A) != 0)
    def _():
        wg_s[0] = wga_ref[0, 0].astype(BF16)
        wu_s[0] = wua_ref[0, 0].astype(BF16)
        wd_s[0] = wda_ref[0, 0].astype(BF16)

    @pl.when((flags & FLAG_NEW_B) != 0)
    def _():
        wg_s[1] = wgb_ref[0, 0].astype(BF16)
        wu_s[1] = wub_ref[0, 0].astype(BF16)
        wd_s[1] = wdb_ref[0, 0].astype(BF16)

    @pl.when((flags & FLAG_ACTIVE) != 0)
    def _():
        xb = xbuf[slot, :, :D_MODEL].astype(BF16)
        gates = xbuf[slot, :, D_MODEL:]
        row = t * tm + lax.broadcasted_iota(jnp.int32, (tm, 1), 0)
        in_class = (row >= lo_ref[i]) & (row < hi_ref[i])
        gate_a = jnp.where(in_class, _gate_column(gates, ea_ref[i]), 0.0)
        gate_b = jnp.where(in_class, _gate_column(gates, eb_ref[i]), 0.0)
        hid_a = _expert_hidden(xb, wg_s[0], wu_s[0], gate_a)
        hid_b = _expert_hidden(xb, wg_s[1], wu_s[1], gate_b)
        acc[...] += (jnp.dot(hid_a, wd_s[0], preferred_element_type=F32)
                     + jnp.dot(hid_b, wd_s[1], preferred_element_type=F32))

    @pl.when((flags & FLAG_LAST) != 0)
    def _():
        @pl.when(t >= 2)
        def _():
            wait_scatter(slot)
        obuf[slot] = _layer_norm(DN_ALPHA * xbuf[slot, :, :D_MODEL] + acc[...], lng_ref[...], lnb_ref[...])
        for_slot(slot, lambda s: start_scatter(t, s))

    @pl.when(i == n_items - 1)
    def _():
        wait_scatter(0)
        wait_scatter(1)


def _moe_schedule(cls, *, tm):
    m = cls.shape[0]
    n_tiles = m // tm
    n_items = n_tiles + N_CLASSES - 1
    i32 = jnp.int32
    src = jnp.argsort(cls, stable=True).astype(i32)
    counts = jnp.sum((cls[:, None] == jnp.arange(N_CLASSES, dtype=i32)[None, :]).astype(i32), axis=0)
    ends = jnp.cumsum(counts)
    starts = ends - counts
    def count(mask):
        return jnp.sum(mask.astype(i32), axis=-1)

    def pick(onehot, values):
        return jnp.sum(jnp.where(onehot, values[None, :], 0), axis=-1)

    tile_lo = jnp.arange(n_tiles, dtype=i32) * tm
    first_c = count(ends[None, :] <= tile_lo[:, None])
    last_c = count(starts[None, :] < (tile_lo + tm)[:, None]) - 1
    per_tile = last_c - first_c + 1
    item_end = jnp.cumsum(per_tile)
    idx = jnp.arange(n_items, dtype=i32)
    tile = jnp.minimum(count(item_end[None, :] <= idx[:, None]), n_tiles - 1)
    in_tile = tile[:, None] == jnp.arange(n_tiles, dtype=i32)[None, :]
    end = pick(in_tile, item_end)
    begin = end - pick(in_tile, per_tile)
    active = idx < item_end[-1]
    c = jnp.where(active, pick(in_tile, first_c) + idx - begin, last_c[n_tiles - 1])
    in_class = c[:, None] == jnp.arange(N_CLASSES, dtype=i32)[None, :]
    class_a = [g * EXPERTS_PER_GROUP + pa for g in range(N_GROUPS) for pa, _ in PAIRS]
    class_b = [g * EXPERTS_PER_GROUP + pb for g in range(N_GROUPS) for _, pb in PAIRS]
    ea = pick(in_class, jnp.asarray(class_a, dtype=i32))
    eb = pick(in_class, jnp.asarray(class_b, dtype=i32))
    lo = jnp.where(active, pick(in_class, starts), 0)
    hi = jnp.where(active, pick(in_class, ends), 0)
    new_a = jnp.concatenate([jnp.ones((1,), bool), ea[1:] != ea[:-1]])
    new_b = jnp.concatenate([jnp.ones((1,), bool), eb[1:] != eb[:-1]])
    flags = (active * FLAG_ACTIVE + (active & (idx == begin)) * FLAG_FIRST
             + (active & (idx == end - 1)) * FLAG_LAST + new_a * FLAG_NEW_A + new_b * FLAG_NEW_B)
    return src, tile, ea, eb, lo.astype(i32), hi.astype(i32), flags.astype(i32)


def _moe_sparse(x, layer, wg, wu, wd, lng, lnb, *, tm):
    m = x.shape[0]
    n_tiles = m // tm
    n_items = n_tiles + N_CLASSES - 1
    cls = x[:, D_MODEL + CLASS_LANE].astype(jnp.int32)
    tables = _moe_schedule(cls, tm=tm)
    kern = functools.partial(_moe_sparse_kernel, tm=tm, n_tiles=n_tiles, n_items=n_items)

    def w_spec(shape, which):
        return pl.BlockSpec((1, 1) + shape, lambda i, src, tile, ea, eb, lo, hi, fl: (layer, (ea, eb)[which][i], 0, 0))

    up, down = (D_MODEL, D_EXPERT), (D_EXPERT, D_MODEL)
    grid_spec = pltpu.PrefetchScalarGridSpec(
        num_scalar_prefetch=len(tables),
        grid=(n_items,),
        in_specs=[
            pl.BlockSpec(memory_space=pl.ANY),
            w_spec(up, 0), w_spec(up, 0), w_spec(down, 0), w_spec(up, 1), w_spec(up, 1), w_spec(down, 1),
            _const_spec(lng.shape), _const_spec(lnb.shape),
        ],
        out_specs=pl.BlockSpec(memory_space=pl.ANY),
        scratch_shapes=[
            pltpu.VMEM((2, tm, D_WIDE), F32), pltpu.VMEM((2, tm, D_MODEL), F32), pltpu.VMEM((tm, D_MODEL), F32),
            pltpu.VMEM((2,) + up, BF16), pltpu.VMEM((2,) + up, BF16), pltpu.VMEM((2,) + down, BF16),
            pltpu.SemaphoreType.DMA((2,)), pltpu.SemaphoreType.DMA((2,)),
        ],
    )
    return pl.pallas_call(
        kern,
        grid_spec=grid_spec,
        out_shape=jax.ShapeDtypeStruct((m, D_MODEL), F32),
        compiler_params=_params("arbitrary"),
        name="moe_sparse",
    )(*tables, x, wg, wu, wd, wg, wu, wd, lng, lnb)


def _row(v):
    return v.reshape(1, -1)


def _to_time_major(s):
    return jnp.transpose(s, (1, 0, 2)).reshape(1, -1, s.shape[-1])


def _from_time_major(s, steps):
    return jnp.transpose(s.reshape(steps, DEC_BATCH, s.shape[-1]), (1, 0, 2))


def _kv_lane_view(cache):
    d, n = cache.shape[:2]
    halves = MEM_HEAD_DIM // LANES
    v = cache.reshape(d, n, N_MEM, MEM_HEADS, halves, LANES)
    return jnp.transpose(v, (0, 1, 2, 4, 3, 5)).reshape(d, n, KV_ROWS, LANES)


def _kv_from_lane_view(view):
    d, n = view.shape[:2]
    halves = MEM_HEAD_DIM // LANES
    v = view.reshape(d, n, N_MEM, halves, MEM_HEADS, LANES)
    return jnp.transpose(v, (0, 1, 2, 4, 3, 5)).reshape(d, n, N_MEM, MEM_HEADS, MEM_HEAD_DIM)


def kernel(x_prompt, x_sample, cache_mem_k, cache_mem_v, state_pool, state_conv_c, state_conv_d, mem_prompt,
           w_in_ab, ln_v_g, ln_v_b, w_spatial, b_spatial, w_pool, pool_scale, w_out_ab,
           w_in_cd, conv_c_w, conv_c_b, ln_c_g, ln_c_b, conv_d_w, w_out_cd,
           w_q, w_k, w_v, w_o, w_group, b_group, w_router, b_router, w_gate, w_up, w_down, ln_g, ln_b):
    mem_flat = mem_prompt.reshape(BATCH * N_MEM, D_MODEL)
    cache_k = _kv_lane_view(cache_mem_k)
    cache_v = _kv_lane_view(cache_mem_v)
    kp, vp, kp_view, vp_view = _kv_proj(mem_flat, w_k.astype(BF16), w_v.astype(BF16))

    w_in_ab, w_out_ab, w_pool = w_in_ab.astype(BF16), w_out_ab.astype(BF16), w_pool.astype(BF16)
    w_in_cd, w_out_cd = w_in_cd.astype(BF16), w_out_cd.astype(BF16)
    w_q, w_o = w_q.astype(BF16), w_o.astype(BF16)
    pad = ROUTER_LANES - N_GROUPS - N_EXPERTS
    w_route = jnp.pad(jnp.concatenate([w_group, w_router], axis=-1), ((0, 0), (0, 0), (0, pad))).astype(BF16)
    b_route = jnp.pad(jnp.concatenate([b_group, b_router], axis=-1), ((0, 0), (0, pad)))[:, None, :]

    prompt = _Slab(x_prompt.reshape(PROMPT_ROWS, D_MODEL), 0, PROMPT_ROWS, 0, None)
    sample = _Slab(_to_time_major(x_sample)[0], 0, SAMPLE_ROWS, PROMPT_ROWS, None)

    def both(h):
        return (_Slab(h, 0, PROMPT_ROWS, 0, None), _Slab(h, PROMPT_ROWS, SAMPLE_ROWS, PROMPT_ROWS, None))

    pool_p, pool_s, chunk_v_s = [], [], []
    conv_c_p, conv_c_s, conv_d_p, conv_d_s = [], [], [], []

    for l in range(DEPTH):
        i = l // 2
        lng0, lnb0 = _row(ln_g[l, 0]), _row(ln_b[l, 0])
        if l % 2 == 0:
            bias = jnp.repeat(b_spatial[i].T, A_HEAD_DIM, axis=1)
            small = jnp.repeat(
                jnp.transpose(w_spatial[i][:, :DEC_SEQ, :DEC_SEQ], (1, 2, 0)).reshape(DEC_SEQ * DEC_SEQ, A_HEADS),
                A_HEAD_DIM, axis=1)
            common = (_Layer(w_in_ab, i), _row(ln_v_g[i]), _row(ln_v_b[i]))
            tail = (_Layer(w_pool, i), _row(pool_scale[i]), _Layer(w_out_ab, i), lng0, lnb0)
            h, hist_p = _mixer_ab(prompt, jnp.zeros((BATCH, POOL_HIST, D_B), F32), *common, w_spatial[i], bias,
                                  *tail, seqs=1, tt=512, pos0=0, with_v=False)
            h, v_rows, hist_s = _mixer_ab(sample._replace(dst=h), _to_time_major(state_pool[i]), *common, small,
                                          bias[:DEC_SEQ], *tail, seqs=DEC_BATCH, tt=DEC_SEQ, pos0=PAST_LEN,
                                          with_v=True)
            pool_p.append(hist_p)
            pool_s.append(_from_time_major(hist_s, POOL_HIST))
            chunk_v_s.append(_from_time_major(v_rows, DEC_SEQ))
        else:
            wts = (_Layer(w_in_cd, i), conv_c_w[i], _row(conv_c_b[i]), _row(ln_c_g[i]), _row(ln_c_b[i]),
                   conv_d_w[i], _Layer(w_out_cd, i), lng0, lnb0)
            h, hc_p, hd_p = _mixer_cd(prompt, jnp.zeros((BATCH, CONV_C - 1, D_C), F32),
                                      jnp.zeros((BATCH, CONV_D - 1, D_D), F32), *wts, seqs=1, tt=512)
            h, hc_s, hd_s = _mixer_cd(sample._replace(dst=h), _to_time_major(state_conv_c[i]),
                                      _to_time_major(state_conv_d[i]), *wts, seqs=DEC_BATCH, tt=DEC_SEQ)
            conv_c_p.append(hc_p)
            conv_d_p.append(hd_p)
            conv_c_s.append(_from_time_major(hc_s, CONV_C - 1))
            conv_d_s.append(_from_time_major(hd_s, CONV_D - 1))
        prompt, sample = both(h)

        att_w = (_Layer(w_q, l), _Layer(w_o, l), _Layer(w_route, l), _Layer(b_route, l),
                 _row(ln_g[l, 1]), _row(ln_b[l, 1]))
        h_wide = _attn_prompt(prompt, kp, vp, l, *att_w, tq=512)
        h_wide = _attn_sample(sample._replace(dst=h_wide), cache_k, cache_v, l, *att_w, seq_blk=4)

        h = _moe_sparse(h_wide, l, w_gate, w_up, w_down, _row(ln_g[l, 2]), _row(ln_b[l, 2]), tm=256)
        prompt, sample = both(h)

    y_prompt = h[:PROMPT_ROWS].reshape(BATCH, SEQ, D_MODEL)
    y_sample = _from_time_major(h[PROMPT_ROWS:], DEC_SEQ)
    return (y_prompt, y_sample, _kv_from_lane_view(kp_view), _kv_from_lane_view(vp_view), jnp.stack(pool_p),
            jnp.stack(conv_c_p), jnp.stack(conv_d_p), jnp.stack(chunk_v_s), jnp.stack(pool_s),
            jnp.stack(conv_c_s), jnp.stack(conv_d_s))
```

```python
import functools
from typing import Callable, NamedTuple, Optional

import jax
import jax.numpy as jnp
from jax import lax
from jax.experimental import pallas as pl
from jax.experimental.pallas import tpu as pltpu

D_MODEL = 1024
BATCH = 8
SEQ = 2048
DEPTH = 4
DEC_BATCH = 128
DEC_SEQ = 4
PAST_LEN = 16384

CHUNK = 128
A_HEADS = 4
D_A = D_MODEL // 2
A_HEAD_DIM = D_A // A_HEADS
POOL_WINDOWS = (2, 4, 8, 16)
B_GROUPS = len(POOL_WINDOWS)
D_B = D_MODEL // 2
B_GROUP_DIM = D_B // B_GROUPS
POOL_HIST = max(POOL_WINDOWS) - 1
D_C = D_MODEL // 2
CONV_C = 31
D_D = D_MODEL // 2
CONV_D = 3
N_MEM = 256
MEM_HEADS = 4
MEM_HEAD_DIM = D_MODEL // MEM_HEADS
N_GROUPS = 4
EXPERTS_PER_GROUP = 4
N_EXPERTS = N_GROUPS * EXPERTS_PER_GROUP
D_EXPERT = 512
DN_ALPHA = (2 * DEPTH) ** 0.25
LN_EPS = 1e-5
PROMPT_ROWS = BATCH * SEQ
SAMPLE_ROWS = DEC_BATCH * DEC_SEQ
TOTAL_ROWS = PROMPT_ROWS + SAMPLE_ROWS

LANES = 128
SUBLANES = 8
POOL_HIST_PAD = 16
CONV_C_HIST_PAD = 32
CONV_D_HIST_PAD = 8
ROUTER_LANES = 128
ROUTER_MASKED = -1e30
CLASS_LANE = N_GROUPS + N_EXPERTS
PAIRS = ((0, 1), (0, 2), (1, 2), (1, 3), (0, 3), (2, 3))
N_CLASSES = N_GROUPS * len(PAIRS)
D_WIDE = D_MODEL + ROUTER_LANES
KV_ROWS = N_MEM * MEM_HEADS * MEM_HEAD_DIM // LANES
ATT_ROWS = KV_ROWS // N_MEM * DEC_SEQ
VMEM_LIMIT_BYTES = 52 * 1024 * 1024

F32 = jnp.float32
BF16 = jnp.bfloat16


def _layer_norm(x, g, b):
    mu = jnp.mean(x, axis=-1, keepdims=True)
    xc = x - mu
    var = jnp.mean(xc * xc, axis=-1, keepdims=True)
    return xc * lax.rsqrt(var + LN_EPS) * g + b


def _dot(a, b):
    return jnp.dot(a.astype(BF16), b.astype(BF16), preferred_element_type=F32)


def _params(*semantics):
    return pltpu.CompilerParams(dimension_semantics=semantics, vmem_limit_bytes=VMEM_LIMIT_BYTES)


def _const_spec(shape):
    nd = len(shape)
    return pl.BlockSpec(shape, lambda *_: (0,) * nd)


class _Layer(NamedTuple):
    stack: jax.Array
    layer: int


def _param_spec(p):
    if isinstance(p, _Layer):
        nd = p.stack.ndim
        return pl.BlockSpec((1,) + p.stack.shape[1:], lambda *_: (p.layer,) + (0,) * (nd - 1))
    return _const_spec(p.shape)


def _param_arg(p):
    return p.stack if isinstance(p, _Layer) else p


class _Slab(NamedTuple):
    array: jax.Array
    row0: int
    n_rows: int
    out_row0: int
    dst: Optional[jax.Array]


class _SlabIO(NamedTuple):
    in_spec: pl.BlockSpec
    out_spec: pl.BlockSpec
    out_shape: jax.ShapeDtypeStruct
    alias_spec: list
    alias_arg: tuple
    aliases: dict
    adapt: Callable


def _slab_io(src, rows, out_width, block_index, *, n_inputs):
    assert src.row0 % rows == 0 and src.out_row0 % rows == 0 and src.n_rows % rows == 0
    in0, out0 = src.row0 // rows, src.out_row0 // rows
    in_spec = pl.BlockSpec((rows, src.array.shape[1]), lambda *g: (in0 + block_index(*g), 0))
    out_spec = pl.BlockSpec((rows, out_width), lambda *g: (out0 + block_index(*g), 0))
    out_shape = jax.ShapeDtypeStruct((TOTAL_ROWS, out_width), F32)
    if src.dst is None:
        return _SlabIO(in_spec, out_spec, out_shape, [], (), {}, lambda kernel: kernel)

    def adapt(kernel):
        return lambda *refs: kernel(*refs[:n_inputs], *refs[n_inputs + 1:])

    return _SlabIO(in_spec, out_spec, out_shape, [pl.BlockSpec(memory_space=pl.ANY)], (src.dst,),
                   {n_inputs: 0}, adapt)


def _mixer_ab_kernel(x_ref, hist_ref, w_in_ref, lnv_g_ref, lnv_b_ref, ws_ref, bs_ref, wpool_ref,
                     pscale_ref, w_out_ref, lng_ref, lnb_ref, *refs, seqs, tt, pos0, with_v):
    if with_v:
        h_out_ref, v_out_ref, hist_out_ref, zbuf = refs
    else:
        h_out_ref, hist_out_ref, zbuf = refs
    t_idx = pl.program_id(1)
    rows = tt * seqs
    hp = POOL_HIST_PAD

    @pl.when(t_idx == 0)
    def _():
        zbuf[pl.ds((hp - POOL_HIST) * seqs, POOL_HIST * seqs), :] = hist_ref[0]

    x = x_ref[...]
    h = _dot(x, w_in_ref[0])
    ua = jax.nn.gelu(h[:, :2 * D_A])
    u = ua[:, :D_A]
    v = _layer_norm(ua[:, D_A:], lnv_g_ref[...], lnv_b_ref[...])
    z = h[:, 2 * D_A:]
    if with_v:
        v_out_ref[0] = v
    zbuf[pl.ds(hp * seqs, rows), :] = z

    if seqs == 1:
        tri = (lax.broadcasted_iota(jnp.int32, (CHUNK, CHUNK), 0)
               >= lax.broadcasted_iota(jnp.int32, (CHUNK, CHUNK), 1))
        w_heads = [jnp.where(tri, ws_ref[hh], 0.0).astype(BF16) for hh in range(A_HEADS)]
        vb = v.astype(BF16)
        chunks = []
        for c in range(tt // CHUNK):
            heads = []
            for hh in range(A_HEADS):
                vc = vb[c * CHUNK:(c + 1) * CHUNK, hh * A_HEAD_DIM:(hh + 1) * A_HEAD_DIM]
                heads.append(jnp.dot(w_heads[hh], vc, preferred_element_type=F32))
            chunks.append(jnp.concatenate(heads, axis=-1) + bs_ref[...])
        mixed = jnp.concatenate(chunks, axis=0)
    else:
        parts = []
        for t in range(tt):
            acc = bs_ref[t:t + 1, :]
            for s in range(t + 1):
                acc = acc + ws_ref[t * tt + s:t * tt + s + 1, :] * v[s * seqs:(s + 1) * seqs, :]
            parts.append(acc)
        mixed = jnp.concatenate(parts, axis=0)
    a_out = u * mixed

    outs = []
    for gi, w in enumerate(POOL_WINDOWS):
        lo, hi = gi * B_GROUP_DIM, (gi + 1) * B_GROUP_DIM
        acc = zbuf[pl.ds(hp * seqs, rows), lo:hi]
        for j in range(1, w):
            acc = acc + zbuf[pl.ds((hp - j) * seqs, rows), lo:hi]
        if pos0 + 1 >= w:
            cnt = float(w)
        else:
            assert seqs == 1
            pos = pos0 + t_idx * tt + lax.broadcasted_iota(jnp.int32, (rows, 1), 0)
            cnt = jnp.minimum(pos + 1, w).astype(F32)
        pooled = acc / cnt - z[:, lo:hi]
        outs.append(_dot(pooled, wpool_ref[0, gi]))
    b_out = jnp.concatenate(outs, axis=-1) * pscale_ref[...]

    y = _dot(jnp.concatenate([a_out, b_out], axis=-1), w_out_ref[0])
    h_out_ref[...] = _layer_norm(DN_ALPHA * x + y, lng_ref[...], lnb_ref[...])

    hist_out_ref[0] = zbuf[pl.ds((tt + hp - POOL_HIST) * seqs, POOL_HIST * seqs), :]
    zbuf[pl.ds(0, hp * seqs), :] = zbuf[pl.ds(tt * seqs, hp * seqs), :]


def _mixer_ab(src, hist, w_in, lnv_g, lnv_b, ws, bs, wpool, pscale, w_out, lng, lnb, *, seqs, tt, pos0, with_v):
    nb = hist.shape[0]
    rows = tt * seqs
    n_t = src.n_rows // (nb * rows)
    kern = functools.partial(_mixer_ab_kernel, seqs=seqs, tt=tt, pos0=pos0, with_v=with_v)
    weights = (w_in, lnv_g, lnv_b, ws, bs, wpool, pscale, w_out, lng, lnb)
    io = _slab_io(src, rows, D_MODEL, lambda n, t: n * n_t + t, n_inputs=2 + len(weights))
    v_spec = pl.BlockSpec((1, rows, D_A), lambda n, t: (n, t, 0))
    hist_spec = pl.BlockSpec((1, POOL_HIST * seqs, D_B), lambda n, t: (n, 0, 0))
    v_shape = jax.ShapeDtypeStruct((nb, n_t * rows, D_A), F32)
    hist_shape = jax.ShapeDtypeStruct((nb, POOL_HIST * seqs, D_B), F32)
    return pl.pallas_call(
        io.adapt(kern),
        grid=(nb, n_t),
        in_specs=[io.in_spec, hist_spec] + [_param_spec(w) for w in weights] + io.alias_spec,
        out_specs=[io.out_spec, v_spec, hist_spec] if with_v else [io.out_spec, hist_spec],
        out_shape=[io.out_shape, v_shape, hist_shape] if with_v else [io.out_shape, hist_shape],
        input_output_aliases=io.aliases,
        scratch_shapes=[pltpu.VMEM(((POOL_HIST_PAD + tt) * seqs, D_B), F32)],
        compiler_params=_params("parallel", "arbitrary"),
        name="mixer_ab",
    )(src.array, hist, *map(_param_arg, weights), *io.alias_arg)


def _shift_rows(taps, hist_pad):
    base = hist_pad - (taps - 1)
    return max((base + k) // SUBLANES * SUBLANES for k in range(taps) if (base + k) % SUBLANES)


def _dwconv(buf, w_ref, out, shifted, *, taps, hist_pad, seqs, rows, chunk):
    base = (hist_pad - (taps - 1)) * seqs
    if shifted is not None:
        span = rows + _shift_rows(taps, hist_pad)
        for b in range(1, SUBLANES):
            shifted[b - 1] = buf[pl.ds(b, span), :]

    def tap(k, r0, n, lanes):
        off = base + k * seqs
        phase = off % SUBLANES
        if shifted is None or phase == 0:
            return buf[pl.ds(off + r0, n), lanes]
        return shifted[phase - 1, pl.ds(off - phase + r0, n), lanes]

    if shifted is None:
        def full_width(i, carry):
            r0 = i * chunk if isinstance(i, int) else pl.multiple_of(i * chunk, chunk)
            acc = w_ref[0:1, :] * tap(0, r0, chunk, slice(None))
            for k in range(1, taps):
                acc = acc + w_ref[k:k + 1, :] * tap(k, r0, chunk, slice(None))
            out[pl.ds(r0, chunk), :] = acc
            return carry

        if seqs % SUBLANES == 0:
            lax.fori_loop(0, rows // chunk, full_width, 0)
        else:
            for i in range(rows // chunk):
                full_width(i, 0)
        return

    for lt in range(buf.shape[1] // LANES):
        lanes = slice(lt * LANES, (lt + 1) * LANES)
        wk = [jnp.broadcast_to(w_ref[k:k + 1, lanes], (SUBLANES, LANES)) for k in range(taps)]
        for rg in range(0, rows, SUBLANES):
            acc = wk[0] * tap(0, rg, SUBLANES, lanes)
            for k in range(1, taps):
                acc = acc + wk[k] * tap(k, rg, SUBLANES, lanes)
            out[pl.ds(rg, SUBLANES), lanes] = acc


def _mixer_cd_kernel(x_ref, hist_c_ref, hist_d_ref, w_in_ref, ccw_ref, ccb_ref, lncg_ref, lncb_ref,
                     cdw_ref, w_out_ref, lng_ref, lnb_ref,
                     h_out_ref, hist_c_out_ref, hist_d_out_ref, cbuf, dbuf, cc_scr, cd_scr, *maybe_shifted,
                     seqs, tt):
    shifted = maybe_shifted[0] if maybe_shifted else None
    t_idx = pl.program_id(1)
    rows = tt * seqs
    hc, hd = CONV_C_HIST_PAD, CONV_D_HIST_PAD
    nc, nd = CONV_C - 1, CONV_D - 1

    @pl.when(t_idx == 0)
    def _():
        cbuf[pl.ds((hc - nc) * seqs, nc * seqs), :] = hist_c_ref[0]
        dbuf[pl.ds((hd - nd) * seqs, nd * seqs), :] = hist_d_ref[0]

    x = x_ref[...]
    h = _dot(x, w_in_ref[0])
    glu = h[:, :D_C] * jax.nn.sigmoid(h[:, D_C:2 * D_C])
    o = 2 * D_C
    gate_b = h[:, o:o + D_D]
    gx = h[:, o + D_D:o + 2 * D_D] * h[:, o + 2 * D_D:]
    cbuf[pl.ds(hc * seqs, rows), :] = glu
    dbuf[pl.ds(hd * seqs, rows), :] = gx

    chunk = min(rows, 32)
    _dwconv(cbuf, ccw_ref, cc_scr, shifted, taps=CONV_C, hist_pad=hc, seqs=seqs, rows=rows, chunk=chunk)
    _dwconv(dbuf, cdw_ref, cd_scr, None, taps=CONV_D, hist_pad=hd, seqs=seqs, rows=rows, chunk=chunk)

    c_out = jax.nn.silu(_layer_norm(cc_scr[...] + ccb_ref[...], lncg_ref[...], lncb_ref[...]))
    d_out = gate_b * cd_scr[...]
    y = _dot(jnp.concatenate([c_out, d_out], axis=-1), w_out_ref[0])
    h_out_ref[...] = _layer_norm(DN_ALPHA * x + y, lng_ref[...], lnb_ref[...])

    hist_c_out_ref[0] = cbuf[pl.ds((tt + hc - nc) * seqs, nc * seqs), :]
    hist_d_out_ref[0] = dbuf[pl.ds((tt + hd - nd) * seqs, nd * seqs), :]
    cbuf[pl.ds(0, hc * seqs), :] = cbuf[pl.ds(tt * seqs, hc * seqs), :]
    dbuf[pl.ds(0, hd * seqs), :] = dbuf[pl.ds(tt * seqs, hd * seqs), :]


def _mixer_cd(src, hist_c, hist_d, w_in, ccw, ccb, lncg, lncb, cdw, w_out, lng, lnb, *, seqs, tt):
    nb = hist_c.shape[0]
    rows = tt * seqs
    n_t = src.n_rows // (nb * rows)
    nc, nd = CONV_C - 1, CONV_D - 1
    kern = functools.partial(_mixer_cd_kernel, seqs=seqs, tt=tt)
    weights = (w_in, ccw, ccb, lncg, lncb, cdw, w_out, lng, lnb)
    io = _slab_io(src, rows, D_MODEL, lambda n, t: n * n_t + t, n_inputs=3 + len(weights))
    hist_c_spec = pl.BlockSpec((1, nc * seqs, D_C), lambda n, t: (n, 0, 0))
    hist_d_spec = pl.BlockSpec((1, nd * seqs, D_D), lambda n, t: (n, 0, 0))
    return pl.pallas_call(
        io.adapt(kern),
        grid=(nb, n_t),
        in_specs=[io.in_spec, hist_c_spec, hist_d_spec] + [_param_spec(w) for w in weights] + io.alias_spec,
        out_specs=[io.out_spec, hist_c_spec, hist_d_spec],
        out_shape=[
            io.out_shape,
            jax.ShapeDtypeStruct((nb, nc * seqs, D_C), F32),
            jax.ShapeDtypeStruct((nb, nd * seqs, D_D), F32),
        ],
        input_output_aliases=io.aliases,
        scratch_shapes=[
            pltpu.VMEM(((CONV_C_HIST_PAD + tt) * seqs, D_C), F32),
            pltpu.VMEM(((CONV_D_HIST_PAD + tt) * seqs, D_D), F32),
            pltpu.VMEM((rows, D_C), F32),
            pltpu.VMEM((rows, D_D), F32),
        ] + ([pltpu.VMEM((SUBLANES - 1, rows + _shift_rows(CONV_C, CONV_C_HIST_PAD), D_C), F32)]
             if seqs % SUBLANES else []),
        compiler_params=_params("parallel", "arbitrary"),
        name="mixer_cd",
    )(src.array, hist_c, hist_d, *map(_param_arg, weights), *io.alias_arg)


def _kv_proj_kernel(mem_ref, wk_ref, wv_ref, k_ref, v_ref, kview_ref, vview_ref):
    m = mem_ref[...].astype(BF16)
    halves = MEM_HEAD_DIM // LANES
    for w_ref, o_ref, view_ref in ((wk_ref, k_ref, kview_ref), (wv_ref, v_ref, vview_ref)):
        y = jnp.dot(m, w_ref[0], preferred_element_type=F32)
        o_ref[0] = y
        for hh in range(MEM_HEADS):
            for lt in range(halves):
                col = hh * MEM_HEAD_DIM + lt * LANES
                view_ref[0, 0, pl.ds(lt * MEM_HEADS + hh, N_MEM, stride=halves * MEM_HEADS), :] = (
                    y[:, col:col + LANES])


def _kv_proj(mem, wk, wv):
    rows = jax.ShapeDtypeStruct((DEPTH, BATCH * N_MEM, D_MODEL), F32)
    view = jax.ShapeDtypeStruct((DEPTH, BATCH, KV_ROWS, LANES), F32)
    w_spec = pl.BlockSpec((1, D_MODEL, D_MODEL), lambda l, n: (l, 0, 0))
    return pl.pallas_call(
        _kv_proj_kernel,
        grid=(DEPTH, BATCH),
        in_specs=[pl.BlockSpec((N_MEM, D_MODEL), lambda l, n: (n, 0)), w_spec, w_spec],
        out_specs=[pl.BlockSpec((1, N_MEM, D_MODEL), lambda l, n: (l, n, 0))] * 2
        + [pl.BlockSpec((1, 1, KV_ROWS, LANES), lambda l, n: (l, n, 0, 0))] * 2,
        out_shape=[rows, rows, view, view],
        compiler_params=_params("parallel", "arbitrary"),
        name="kv_proj",
    )(mem, wk, wv)


def _attend(q, k, v):
    scale = MEM_HEAD_DIM ** -0.5
    qb = q.astype(BF16)
    outs = []
    for hh in range(MEM_HEADS):
        lo, hi = hh * MEM_HEAD_DIM, (hh + 1) * MEM_HEAD_DIM
        s = lax.dot_general(qb[:, lo:hi], k[:, lo:hi], (((1,), (1,)), ((), ())),
                            preferred_element_type=F32) * scale
        s = s - jnp.max(s, axis=-1, keepdims=True)
        p = jnp.exp(s)
        p = p / jnp.sum(p, axis=-1, keepdims=True)
        outs.append(jnp.dot(p.astype(BF16), v[:, lo:hi], preferred_element_type=F32))
    return jnp.concatenate(outs, axis=-1)


def _attn_prompt_kernel(x_ref, k_ref, v_ref, wq_ref, wo_ref, wr_ref, br_ref, lng_ref, lnb_ref, o_ref):
    x = x_ref[...]
    q = _dot(x, wq_ref[0])
    o = _attend(q, k_ref[0].astype(BF16), v_ref[0].astype(BF16))
    y = _dot(o, wo_ref[0])
    h2 = _layer_norm(DN_ALPHA * x + y, lng_ref[...], lnb_ref[...])
    o_ref[:, :D_MODEL] = h2
    o_ref[:, D_MODEL:] = _route(_router_logits(h2, wr_ref[0]) + br_ref[0])


def _attn_prompt(src, k, v, layer, wq, wo, wr, br, lng, lnb, *, tq):
    n_t = src.n_rows // (BATCH * tq)
    weights = (wq, wo, wr, br, lng, lnb)
    io = _slab_io(src, tq, D_WIDE, lambda n, i: n * n_t + i, n_inputs=3 + len(weights))
    kv_spec = pl.BlockSpec((1, N_MEM, D_MODEL), lambda n, i: (layer, n, 0))
    return pl.pallas_call(
        io.adapt(_attn_prompt_kernel),
        grid=(BATCH, n_t),
        in_specs=[io.in_spec, kv_spec, kv_spec] + [_param_spec(w) for w in weights] + io.alias_spec,
        out_specs=io.out_spec,
        out_shape=io.out_shape,
        input_output_aliases=io.aliases,
        compiler_params=_params("parallel", "arbitrary"),
        name="attn_prompt",
    )(src.array, k, v, *map(_param_arg, weights), *io.alias_arg)


def _att_row(half, head, t):
    return (half * MEM_HEADS + head) * DEC_SEQ + t


def _attn_sample_kernel(x_ref, k_ref, v_ref, wq_ref, wo_ref, wr_ref, br_ref, lng_ref, lnb_ref, o_ref,
                        q_scr, qp_scr, op_scr, *, seq_blk, n_blk):
    i = pl.program_id(0)
    halves = MEM_HEAD_DIM // LANES
    blocks = [(t, hh, lt) for t in range(DEC_SEQ) for hh in range(MEM_HEADS) for lt in range(halves)]

    @pl.when(i == 0)
    def _():
        q_scr[...] = _dot(x_ref[...], wq_ref[0])
        for t, hh, lt in blocks:
            col = hh * MEM_HEAD_DIM + lt * LANES
            qp_scr[pl.ds(_att_row(lt, hh, t), DEC_BATCH, stride=ATT_ROWS), :] = (
                q_scr[t * DEC_BATCH:(t + 1) * DEC_BATCH, col:col + LANES])

    half_rows = ATT_ROWS // halves
    col = lax.broadcasted_iota(jnp.int32, (half_rows, KV_ROWS), 1)
    row_head = lax.shift_right_logical(lax.broadcasted_iota(jnp.int32, (half_rows, KV_ROWS), 0),
                                       DEC_SEQ.bit_length() - 1)
    col_cls = col & (halves * MEM_HEADS - 1)
    match0 = col_cls == row_head
    match1 = col_cls == row_head + MEM_HEADS
    scale = MEM_HEAD_DIM ** -0.5

    def body(j, carry):
        n = i * seq_blk + j
        r0 = pl.multiple_of(n * ATT_ROWS, ATT_ROWS)
        qp = qp_scr[pl.ds(r0, ATT_ROWS), :].astype(BF16)
        s = lax.dot_general(qp, k_ref[0, j].astype(BF16), (((1,), (1,)), ((), ())),
                            preferred_element_type=F32)
        part = jnp.where(match0, s[:half_rows], 0.0) + pltpu.roll(
            jnp.where(match1, s[half_rows:], 0.0), KV_ROWS - MEM_HEADS, axis=1)
        sv = jnp.where(match0, part * scale, ROUTER_MASKED)
        e = jnp.exp(sv - jnp.max(sv, axis=-1, keepdims=True))
        p = e / jnp.sum(e, axis=-1, keepdims=True)
        pp = jnp.concatenate([p, pltpu.roll(p, MEM_HEADS, axis=1)], axis=0).astype(BF16)
        op_scr[pl.ds(r0, ATT_ROWS), :] = jnp.dot(pp, v_ref[0, j].astype(BF16), preferred_element_type=F32)
        return carry

    lax.fori_loop(0, seq_blk, body, 0, unroll=True)

    @pl.when(i == n_blk - 1)
    def _():
        for t, hh, lt in blocks:
            col = hh * MEM_HEAD_DIM + lt * LANES
            q_scr[t * DEC_BATCH:(t + 1) * DEC_BATCH, col:col + LANES] = (
                op_scr[pl.ds(_att_row(lt, hh, t), DEC_BATCH, stride=ATT_ROWS), :])
        x = x_ref[...]
        y = _dot(q_scr[...], wo_ref[0])
        h2 = _layer_norm(DN_ALPHA * x + y, lng_ref[...], lnb_ref[...])
        o_ref[:, :D_MODEL] = h2
        o_ref[:, D_MODEL:] = _route(_router_logits(h2, wr_ref[0]) + br_ref[0])


def _attn_sample(src, k, v, layer, wq, wo, wr, br, lng, lnb, *, seq_blk):
    m = src.n_rows
    n_blk = DEC_BATCH // seq_blk
    kern = functools.partial(_attn_sample_kernel, seq_blk=seq_blk, n_blk=n_blk)
    weights = (wq, wo, wr, br, lng, lnb)
    io = _slab_io(src, m, D_WIDE, lambda i: 0, n_inputs=3 + len(weights))
    kv_spec = pl.BlockSpec((1, seq_blk, KV_ROWS, LANES), lambda i: (layer, i, 0, 0))
    return pl.pallas_call(
        io.adapt(kern),
        grid=(n_blk,),
        in_specs=[io.in_spec, kv_spec, kv_spec] + [_param_spec(w) for w in weights] + io.alias_spec,
        out_specs=io.out_spec,
        out_shape=io.out_shape,
        input_output_aliases=io.aliases,
        scratch_shapes=[pltpu.VMEM((m, D_MODEL), F32),
                        pltpu.VMEM((DEC_BATCH * ATT_ROWS, LANES), F32),
                        pltpu.VMEM((DEC_BATCH * ATT_ROWS, LANES), F32)],
        compiler_params=_params("arbitrary"),
        name="attn_sample",
    )(src.array, k, v, *map(_param_arg, weights), *io.alias_arg)


def _split_bf16(x):
    hi = x.astype(BF16)
    return hi, (x - hi.astype(F32)).astype(BF16)


def _router_logits(h, w):
    h_hi, h_lo = _split_bf16(h)
    w_hi, w_lo = _split_bf16(w)
    a = jnp.dot(h_hi, jnp.concatenate([w_hi, w_lo], axis=-1), preferred_element_type=F32)
    b = jnp.dot(h_lo, w_hi, preferred_element_type=F32)
    return a[:, :ROUTER_LANES] + a[:, ROUTER_LANES:] + b


def _route(logits):
    lane = lax.broadcasted_iota(jnp.int32, logits.shape, 1)
    lane_f = lane.astype(F32)
    neg = ROUTER_MASKED
    is_grp = lane < N_GROUPS
    gl = jnp.where(is_grp, logits, neg)
    gmax = jnp.max(gl, axis=-1, keepdims=True)
    gsel = jnp.min(jnp.where(gl == gmax, lane_f, float(ROUTER_LANES)), axis=-1, keepdims=True)
    g_w = 1.0 / jnp.sum(jnp.exp(gl - gmax), axis=-1, keepdims=True)
    e_idx = lane - N_GROUPS
    e_grp = lax.shift_right_arithmetic(e_idx, 2).astype(F32)
    in_grp = (e_idx >= 0) & (e_idx < N_EXPERTS) & (e_grp == gsel)
    el = jnp.where(in_grp, logits, neg)
    m1 = jnp.max(el, axis=-1, keepdims=True)
    i1 = jnp.min(jnp.where(el == m1, lane_f, float(ROUTER_LANES)), axis=-1, keepdims=True)
    el2 = jnp.where(lane_f == i1, neg, el)
    m2 = jnp.max(el2, axis=-1, keepdims=True)
    i2 = jnp.min(jnp.where(el2 == m2, lane_f, float(ROUTER_LANES)), axis=-1, keepdims=True)
    e2 = jnp.exp(m2 - m1)
    den = 1.0 + e2
    w1 = (1.0 / den) * g_w
    w2 = (e2 / den) * g_w
    first_lane = N_GROUPS + EXPERTS_PER_GROUP * gsel
    a = jnp.minimum(i1, i2) - first_lane
    b = jnp.maximum(i1, i2) - first_lane
    cls = gsel * len(PAIRS)
    for j, (pa, pb) in enumerate(PAIRS):
        cls = cls + jnp.where((a == pa) & (b == pb), float(j), 0.0)
    return (jnp.where(lane_f == i1, w1, 0.0) + jnp.where(lane_f == i2, w2, 0.0)
            + jnp.where(lane == CLASS_LANE, cls, 0.0))


def _gate_column(gates, expert):
    lane = lax.broadcasted_iota(jnp.int32, gates.shape, 1)
    return jnp.sum(jnp.where(lane == expert + N_GROUPS, gates, 0.0), axis=-1, keepdims=True)


def _expert_hidden(xb, wg, wu, gate):
    hg = jnp.dot(xb, wg, preferred_element_type=F32)
    hu = jnp.dot(xb, wu, preferred_element_type=F32)
    return (jax.nn.silu(hg) * hu * gate).astype(BF16)


FLAG_ACTIVE, FLAG_FIRST, FLAG_LAST, FLAG_NEW_A, FLAG_NEW_B = 1, 2, 4, 8, 16


def _moe_sparse_kernel(src_ref, tile_ref, ea_ref, eb_ref, lo_ref, hi_ref, flag_ref,
                       x_hbm, wga_ref, wua_ref, wda_ref, wgb_ref, wub_ref, wdb_ref, lng_ref, lnb_ref,
                       out_hbm, xbuf, obuf, acc, wg_s, wu_s, wd_s, gsem, ssem, *, tm, n_tiles, n_items):
    i = pl.program_id(0)
    t = tile_ref[i]
    slot = t % 2
    flags = flag_ref[i]

    def start_gather(tile, s):
        for r in range(tm):
            tok = src_ref[tile * tm + r]
            pltpu.make_async_copy(x_hbm.at[pl.ds(tok, 1)], xbuf.at[s, pl.ds(r, 1)], gsem.at[s]).start()

    def start_scatter(tile, s):
        for r in range(tm):
            tok = src_ref[tile * tm + r]
            pltpu.make_async_copy(obuf.at[s, pl.ds(r, 1)], out_hbm.at[pl.ds(tok, 1)],
                                  ssem.at[s]).start(priority=r % 2)

    def for_slot(value, fn):
        for s in (0, 1):
            pl.when(value == s)(functools.partial(fn, s))

    def wait_gather(s):
        pltpu.make_async_copy(x_hbm.at[pl.ds(0, tm)], xbuf.at[s], gsem.at[s]).wait()

    def wait_scatter(s):
        pltpu.make_async_copy(obuf.at[s], out_hbm.at[pl.ds(0, tm)], ssem.at[s]).wait()

    @pl.when(i == 0)
    def _():
        start_gather(0, 0)

    @pl.when((flags & FLAG_FIRST) != 0)
    def _():
        @pl.when(t + 1 < n_tiles)
        def _():
            for_slot(1 - slot, lambda s: start_gather(t + 1, s))
        wait_gather(slot)
        acc[...] = jnp.zeros_like(acc)

    @pl.when((flags & FLAG_NEW_A) != 0)
    def _():
        wg_s[0] = wga_ref[0, 0].astype(BF16)
        wu_s[0] = wua_ref[0, 0].astype(BF16)
        wd_s[0] = wda_ref[0, 0].astype(BF16)

    @pl.when((flags & FLAG_NEW_B) != 0)
    def _():
        wg_s[1] = wgb_ref[0, 0].astype(BF16)
        wu_s[1] = wub_ref[0, 0].astype(BF16)
        wd_s[1] = wdb_ref[0, 0].astype(BF16)

    @pl.when((flags & FLAG_ACTIVE) != 0)
    def _():
        xb = xbuf[slot, :, :D_MODEL].astype(BF16)
        gates = xbuf[slot, :, D_MODEL:]
        row = t * tm + lax.broadcasted_iota(jnp.int32, (tm, 1), 0)
        in_class = (row >= lo_ref[i]) & (row < hi_ref[i])
        gate_a = jnp.where(in_class, _gate_column(gates, ea_ref[i]), 0.0)
        gate_b = jnp.where(in_class, _gate_column(gates, eb_ref[i]), 0.0)
        hid_a = _expert_hidden(xb, wg_s[0], wu_s[0], gate_a)
        hid_b = _expert_hidden(xb, wg_s[1], wu_s[1], gate_b)
        acc[...] += (jnp.dot(hid_a, wd_s[0], preferred_element_type=F32)
                     + jnp.dot(hid_b, wd_s[1], preferred_element_type=F32))

    @pl.when((flags & FLAG_LAST) != 0)
    def _():
        @pl.when(t >= 2)
        def _():
            wait_scatter(slot)
        obuf[slot] = _layer_norm(DN_ALPHA * xbuf[slot, :, :D_MODEL] + acc[...], lng_ref[...], lnb_ref[...])
        for_slot(slot, lambda s: start_scatter(t, s))

    @pl.when(i == n_items - 1)
    def _():
        wait_scatter(0)
        wait_scatter(1)


def _moe_schedule(cls, *, tm):
    m = cls.shape[0]
    n_tiles = m // tm
    n_items = n_tiles + N_CLASSES - 1
    i32 = jnp.int32
    src = jnp.argsort(cls, stable=True).astype(i32)
    counts = jnp.sum((cls[:, None] == jnp.arange(N_CLASSES, dtype=i32)[None, :]).astype(i32), axis=0)
    ends = jnp.cumsum(counts)
    starts = ends - counts
    def count(mask):
        return jnp.sum(mask.astype(i32), axis=-1)

    def pick(onehot, values):
        return jnp.sum(jnp.where(onehot, values[None, :], 0), axis=-1)

    tile_lo = jnp.arange(n_tiles, dtype=i32) * tm
    first_c = count(ends[None, :] <= tile_lo[:, None])
    last_c = count(starts[None, :] < (tile_lo + tm)[:, None]) - 1
    per_tile = last_c - first_c + 1
    item_end = jnp.cumsum(per_tile)
    idx = jnp.arange(n_items, dtype=i32)
    tile = jnp.minimum(count(item_end[None, :] <= idx[:, None]), n_tiles - 1)
    in_tile = tile[:, None] == jnp.arange(n_tiles, dtype=i32)[None, :]
    end = pick(in_tile, item_end)
    begin = end - pick(in_tile, per_tile)
    active = idx < item_end[-1]
    c = jnp.where(active, pick(in_tile, first_c) + idx - begin, last_c[n_tiles - 1])
    in_class = c[:, None] == jnp.arange(N_CLASSES, dtype=i32)[None, :]
    class_a = [g * EXPERTS_PER_GROUP + pa for g in range(N_GROUPS) for pa, _ in PAIRS]
    class_b = [g * EXPERTS_PER_GROUP + pb for g in range(N_GROUPS) for _, pb in PAIRS]
    ea = pick(in_class, jnp.asarray(class_a, dtype=i32))
    eb = pick(in_class, jnp.asarray(class_b, dtype=i32))
    lo = jnp.where(active, pick(in_class, starts), 0)
    hi = jnp.where(active, pick(in_class, ends), 0)
    new_a = jnp.concatenate([jnp.ones((1,), bool), ea[1:] != ea[:-1]])
    new_b = jnp.concatenate([jnp.ones((1,), bool), eb[1:] != eb[:-1]])
    flags = (active * FLAG_ACTIVE + (active & (idx == begin)) * FLAG_FIRST
             + (active & (idx == end - 1)) * FLAG_LAST + new_a * FLAG_NEW_A + new_b * FLAG_NEW_B)
    return src, tile, ea, eb, lo.astype(i32), hi.astype(i32), flags.astype(i32)


def _moe_sparse(x, layer, wg, wu, wd, lng, lnb, *, tm):
    m = x.shape[0]
    n_tiles = m // tm
    n_items = n_tiles + N_CLASSES - 1
    cls = x[:, D_MODEL + CLASS_LANE].astype(jnp.int32)
    tables = _moe_schedule(cls, tm=tm)
    kern = functools.partial(_moe_sparse_kernel, tm=tm, n_tiles=n_tiles, n_items=n_items)

    def w_spec(shape, which):
        return pl.BlockSpec((1, 1) + shape, lambda i, src, tile, ea, eb, lo, hi, fl: (layer, (ea, eb)[which][i], 0, 0))

    up, down = (D_MODEL, D_EXPERT), (D_EXPERT, D_MODEL)
    grid_spec = pltpu.PrefetchScalarGridSpec(
        num_scalar_prefetch=len(tables),
        grid=(n_items,),
        in_specs=[
            pl.BlockSpec(memory_space=pl.ANY),
            w_spec(up, 0), w_spec(up, 0), w_spec(down, 0), w_spec(up, 1), w_spec(up, 1), w_spec(down, 1),
            _const_spec(lng.shape), _const_spec(lnb.shape),
        ],
        out_specs=pl.BlockSpec(memory_space=pl.ANY),
        scratch_shapes=[
            pltpu.VMEM((2, tm, D_WIDE), F32), pltpu.VMEM((2, tm, D_MODEL), F32), pltpu.VMEM((tm, D_MODEL), F32),
            pltpu.VMEM((2,) + up, BF16), pltpu.VMEM((2,) + up, BF16), pltpu.VMEM((2,) + down, BF16),
            pltpu.SemaphoreType.DMA((2,)), pltpu.SemaphoreType.DMA((2,)),
        ],
    )
    return pl.pallas_call(
        kern,
        grid_spec=grid_spec,
        out_shape=jax.ShapeDtypeStruct((m, D_MODEL), F32),
        compiler_params=_params("arbitrary"),
        name="moe_sparse",
    )(*tables, x, wg, wu, wd, wg, wu, wd, lng, lnb)


def _row(v):
    return v.reshape(1, -1)


def _to_time_major(s):
    return jnp.transpose(s, (1, 0, 2)).reshape(1, -1, s.shape[-1])


def _from_time_major(s, steps):
    return jnp.transpose(s.reshape(steps, DEC_BATCH, s.shape[-1]), (1, 0, 2))


def _kv_lane_view(cache):
    d, n = cache.shape[:2]
    halves = MEM_HEAD_DIM // LANES
    v = cache.reshape(d, n, N_MEM, MEM_HEADS, halves, LANES)
    return jnp.transpose(v, (0, 1, 2, 4, 3, 5)).reshape(d, n, KV_ROWS, LANES)


def _kv_from_lane_view(view):
    d, n = view.shape[:2]
    halves = MEM_HEAD_DIM // LANES
    v = view.reshape(d, n, N_MEM, halves, MEM_HEADS, LANES)
    return jnp.transpose(v, (0, 1, 2, 4, 3, 5)).reshape(d, n, N_MEM, MEM_HEADS, MEM_HEAD_DIM)


def kernel(x_prompt, x_sample, cache_mem_k, cache_mem_v, state_pool, state_conv_c, state_conv_d, mem_prompt,
           w_in_ab, ln_v_g, ln_v_b, w_spatial, b_spatial, w_pool, pool_scale, w_out_ab,
           w_in_cd, conv_c_w, conv_c_b, ln_c_g, ln_c_b, conv_d_w, w_out_cd,
           w_q, w_k, w_v, w_o, w_group, b_group, w_router, b_router, w_gate, w_up, w_down, ln_g, ln_b):
    mem_flat = mem_prompt.reshape(BATCH * N_MEM, D_MODEL)
    cache_k = _kv_lane_view(cache_mem_k)
    cache_v = _kv_lane_view(cache_mem_v)
    kp, vp, kp_view, vp_view = _kv_proj(mem_flat, w_k.astype(BF16), w_v.astype(BF16))

    w_in_ab, w_out_ab, w_pool = w_in_ab.astype(BF16), w_out_ab.astype(BF16), w_pool.astype(BF16)
    w_in_cd, w_out_cd = w_in_cd.astype(BF16), w_out_cd.astype(BF16)
    w_q, w_o = w_q.astype(BF16), w_o.astype(BF16)
    pad = ROUTER_LANES - N_GROUPS - N_EXPERTS
    w_route = jnp.pad(jnp.concatenate([w_group, w_router], axis=-1), ((0, 0), (0, 0), (0, pad)))
    b_route = jnp.pad(jnp.concatenate([b_group, b_router], axis=-1), ((0, 0), (0, pad)))[:, None, :]

    prompt = _Slab(x_prompt.reshape(PROMPT_ROWS, D_MODEL), 0, PROMPT_ROWS, 0, None)
    sample = _Slab(_to_time_major(x_sample)[0], 0, SAMPLE_ROWS, PROMPT_ROWS, None)

    def both(h):
        return (_Slab(h, 0, PROMPT_ROWS, 0, None), _Slab(h, PROMPT_ROWS, SAMPLE_ROWS, PROMPT_ROWS, None))

    pool_p, pool_s, chunk_v_s = [], [], []
    conv_c_p, conv_c_s, conv_d_p, conv_d_s = [], [], [], []

    for l in range(DEPTH):
        i = l // 2
        lng0, lnb0 = _row(ln_g[l, 0]), _row(ln_b[l, 0])
        if l % 2 == 0:
            bias = jnp.repeat(b_spatial[i].T, A_HEAD_DIM, axis=1)
            small = jnp.repeat(
                jnp.transpose(w_spatial[i][:, :DEC_SEQ, :DEC_SEQ], (1, 2, 0)).reshape(DEC_SEQ * DEC_SEQ, A_HEADS),
                A_HEAD_DIM, axis=1)
            common = (_Layer(w_in_ab, i), _row(ln_v_g[i]), _row(ln_v_b[i]))
            tail = (_Layer(w_pool, i), _row(pool_scale[i]), _Layer(w_out_ab, i), lng0, lnb0)
            h, hist_p = _mixer_ab(prompt, jnp.zeros((BATCH, POOL_HIST, D_B), F32), *common, w_spatial[i], bias,
                                  *tail, seqs=1, tt=512, pos0=0, with_v=False)
            h, v_rows, hist_s = _mixer_ab(sample._replace(dst=h), _to_time_major(state_pool[i]), *common, small,
                                          bias[:DEC_SEQ], *tail, seqs=DEC_BATCH, tt=DEC_SEQ, pos0=PAST_LEN,
                                          with_v=True)
            pool_p.append(hist_p)
            pool_s.append(_from_time_major(hist_s, POOL_HIST))
            chunk_v_s.append(_from_time_major(v_rows, DEC_SEQ))
        else:
            wts = (_Layer(w_in_cd, i), conv_c_w[i], _row(conv_c_b[i]), _row(ln_c_g[i]), _row(ln_c_b[i]),
                   conv_d_w[i], _Layer(w_out_cd, i), lng0, lnb0)
            h, hc_p, hd_p = _mixer_cd(prompt, jnp.zeros((BATCH, CONV_C - 1, D_C), F32),
                                      jnp.zeros((BATCH, CONV_D - 1, D_D), F32), *wts, seqs=1, tt=512)
            h, hc_s, hd_s = _mixer_cd(sample._replace(dst=h), _to_time_major(state_conv_c[i]),
                                      _to_time_major(state_conv_d[i]), *wts, seqs=DEC_BATCH, tt=DEC_SEQ)
            conv_c_p.append(hc_p)
            conv_d_p.append(hd_p)
            conv_c_s.append(_from_time_major(hc_s, CONV_C - 1))
            conv_d_s.append(_from_time_major(hd_s, CONV_D - 1))
        prompt, sample = both(h)

        att_w = (_Layer(w_q, l), _Layer(w_o, l), _Layer(w_route, l), _Layer(b_route, l),
                 _row(ln_g[l, 1]), _row(ln_b[l, 1]))
        h_wide = _attn_prompt(prompt, kp, vp, l, *att_w, tq=512)
        h_wide = _attn_sample(sample._replace(dst=h_wide), cache_k, cache_v, l, *att_w, seq_blk=4)

        h = _moe_sparse(h_wide, l, w_gate, w_up, w_down, _row(ln_g[l, 2]), _row(ln_b[l, 2]), tm=256)
        prompt, sample = both(h)

    y_prompt = h[:PROMPT_ROWS].reshape(BATCH, SEQ, D_MODEL)
    y_sample = _from_time_major(h[PROMPT_ROWS:], DEC_SEQ)
    return (y_prompt, y_sample, _kv_from_lane_view(kp_view), _kv_from_lane_view(vp_view), jnp.stack(pool_p),
            jnp.stack(conv_c_p), jnp.stack(conv_d_p), jnp.stack(chunk_v_s), jnp.stack(pool_s),
            jnp.stack(conv_c_s), jnp.stack(conv_d_s))
```

```python
import functools
from typing import Callable, NamedTuple, Optional

import jax
import jax.numpy as jnp
from jax import lax
from jax.experimental import pallas as pl
from jax.experimental.pallas import tpu as pltpu

D_MODEL = 1024
BATCH = 8
SEQ = 2048
DEPTH = 4
DEC_BATCH = 128
DEC_SEQ = 4
PAST_LEN = 16384

CHUNK = 128
A_HEADS = 4
D_A = D_MODEL // 2
A_HEAD_DIM = D_A // A_HEADS
POOL_WINDOWS = (2, 4, 8, 16)
B_GROUPS = len(POOL_WINDOWS)
D_B = D_MODEL // 2
B_GROUP_DIM = D_B // B_GROUPS
POOL_HIST = max(POOL_WINDOWS) - 1
D_C = D_MODEL // 2
CONV_C = 31
D_D = D_MODEL // 2
CONV_D = 3
N_MEM = 256
MEM_HEADS = 4
MEM_HEAD_DIM = D_MODEL // MEM_HEADS
N_GROUPS = 4
EXPERTS_PER_GROUP = 4
N_EXPERTS = N_GROUPS * EXPERTS_PER_GROUP
D_EXPERT = 512
DN_ALPHA = (2 * DEPTH) ** 0.25
LN_EPS = 1e-5
PROMPT_ROWS = BATCH * SEQ
SAMPLE_ROWS = DEC_BATCH * DEC_SEQ
TOTAL_ROWS = PROMPT_ROWS + SAMPLE_ROWS

LANES = 128
SUBLANES = 8
POOL_HIST_PAD = 16
CONV_C_HIST_PAD = 32
CONV_D_HIST_PAD = 8
ROUTER_LANES = 128
ROUTER_MASKED = -1e30
CLASS_LANE = N_GROUPS + N_EXPERTS
PAIRS = ((0, 1), (0, 2), (1, 2), (1, 3), (0, 3), (2, 3))
N_CLASSES = N_GROUPS * len(PAIRS)
D_WIDE = D_MODEL + ROUTER_LANES
WIDE_TILES = D_WIDE // LANES
KV_ROWS = N_MEM * MEM_HEADS * MEM_HEAD_DIM // LANES
ATT_ROWS = KV_ROWS // N_MEM * DEC_SEQ
VMEM_LIMIT_BYTES = 52 * 1024 * 1024

F32 = jnp.float32
BF16 = jnp.bfloat16


def _layer_norm(x, g, b):
    mu = jnp.mean(x, axis=-1, keepdims=True)
    xc = x - mu
    var = jnp.mean(xc * xc, axis=-1, keepdims=True)
    return xc * lax.rsqrt(var + LN_EPS) * g + b


def _dot(a, b):
    return jnp.dot(a.astype(BF16), b.astype(BF16), preferred_element_type=F32)


def _params(*semantics):
    return pltpu.CompilerParams(dimension_semantics=semantics, vmem_limit_bytes=VMEM_LIMIT_BYTES)


def _const_spec(shape):
    nd = len(shape)
    return pl.BlockSpec(shape, lambda *_: (0,) * nd)


class _Layer(NamedTuple):
    stack: jax.Array
    layer: int


def _param_spec(p):
    if isinstance(p, _Layer):
        nd = p.stack.ndim
        return pl.BlockSpec((1,) + p.stack.shape[1:], lambda *_: (p.layer,) + (0,) * (nd - 1))
    return _const_spec(p.shape)


def _param_arg(p):
    return p.stack if isinstance(p, _Layer) else p


class _Slab(NamedTuple):
    array: jax.Array
    row0: int
    n_rows: int
    out_row0: int
    dst: Optional[jax.Array]


class _SlabIO(NamedTuple):
    in_spec: pl.BlockSpec
    out_spec: pl.BlockSpec
    out_shape: jax.ShapeDtypeStruct
    alias_spec: list
    alias_arg: tuple
    aliases: dict
    adapt: Callable


def _slab_io(src, rows, out_width, block_index, *, n_inputs, row_major_tiles=False):
    assert src.row0 % rows == 0 and src.out_row0 % rows == 0 and src.n_rows % rows == 0
    in0, out0 = src.row0 // rows, src.out_row0 // rows
    in_spec = pl.BlockSpec((rows, src.array.shape[1]), lambda *g: (in0 + block_index(*g), 0))
    if row_major_tiles:
        pieces = out_width // LANES
        out_spec = pl.BlockSpec((rows * pieces, LANES), lambda *g: (out0 + block_index(*g), 0))
        out_shape = jax.ShapeDtypeStruct((TOTAL_ROWS * pieces, LANES), F32)
    else:
        out_spec = pl.BlockSpec((rows, out_width), lambda *g: (out0 + block_index(*g), 0))
        out_shape = jax.ShapeDtypeStruct((TOTAL_ROWS, out_width), F32)
    if src.dst is None:
        return _SlabIO(in_spec, out_spec, out_shape, [], (), {}, lambda kernel: kernel)

    def adapt(kernel):
        return lambda *refs: kernel(*refs[:n_inputs], *refs[n_inputs + 1:])

    return _SlabIO(in_spec, out_spec, out_shape, [pl.BlockSpec(memory_space=pl.ANY)], (src.dst,),
                   {n_inputs: 0}, adapt)


def _mixer_ab_kernel(x_ref, hist_ref, w_in_ref, lnv_g_ref, lnv_b_ref, ws_ref, bs_ref, wpool_ref,
                     pscale_ref, w_out_ref, lng_ref, lnb_ref, *refs, seqs, tt, pos0, with_v):
    if with_v:
        h_out_ref, v_out_ref, hist_out_ref, zbuf = refs
    else:
        h_out_ref, hist_out_ref, zbuf = refs
    t_idx = pl.program_id(1)
    rows = tt * seqs
    hp = POOL_HIST_PAD

    @pl.when(t_idx == 0)
    def _():
        zbuf[pl.ds((hp - POOL_HIST) * seqs, POOL_HIST * seqs), :] = hist_ref[0]

    x = x_ref[...]
    h = _dot(x, w_in_ref[0])
    ua = jax.nn.gelu(h[:, :2 * D_A])
    u = ua[:, :D_A]
    v = _layer_norm(ua[:, D_A:], lnv_g_ref[...], lnv_b_ref[...])
    z = h[:, 2 * D_A:]
    if with_v:
        v_out_ref[0] = v
    zbuf[pl.ds(hp * seqs, rows), :] = z

    if seqs == 1:
        tri = (lax.broadcasted_iota(jnp.int32, (CHUNK, CHUNK), 0)
               >= lax.broadcasted_iota(jnp.int32, (CHUNK, CHUNK), 1))
        w_heads = [jnp.where(tri, ws_ref[hh], 0.0).astype(BF16) for hh in range(A_HEADS)]
        vb = v.astype(BF16)
        chunks = []
        for c in range(tt // CHUNK):
            heads = []
            for hh in range(A_HEADS):
                vc = vb[c * CHUNK:(c + 1) * CHUNK, hh * A_HEAD_DIM:(hh + 1) * A_HEAD_DIM]
                heads.append(jnp.dot(w_heads[hh], vc, preferred_element_type=F32))
            chunks.append(jnp.concatenate(heads, axis=-1) + bs_ref[...])
        mixed = jnp.concatenate(chunks, axis=0)
    else:
        parts = []
        for t in range(tt):
            acc = bs_ref[t:t + 1, :]
            for s in range(t + 1):
                acc = acc + ws_ref[t * tt + s:t * tt + s + 1, :] * v[s * seqs:(s + 1) * seqs, :]
            parts.append(acc)
        mixed = jnp.concatenate(parts, axis=0)
    a_out = u * mixed

    outs = []
    for gi, w in enumerate(POOL_WINDOWS):
        lo, hi = gi * B_GROUP_DIM, (gi + 1) * B_GROUP_DIM
        acc = zbuf[pl.ds(hp * seqs, rows), lo:hi]
        for j in range(1, w):
            acc = acc + zbuf[pl.ds((hp - j) * seqs, rows), lo:hi]
        if pos0 + 1 >= w:
            cnt = float(w)
        else:
            assert seqs == 1
            pos = pos0 + t_idx * tt + lax.broadcasted_iota(jnp.int32, (rows, 1), 0)
            cnt = jnp.minimum(pos + 1, w).astype(F32)
        pooled = acc / cnt - z[:, lo:hi]
        outs.append(_dot(pooled, wpool_ref[0, gi]))
    b_out = jnp.concatenate(outs, axis=-1) * pscale_ref[...]

    y = _dot(jnp.concatenate([a_out, b_out], axis=-1), w_out_ref[0])
    h_out_ref[...] = _layer_norm(DN_ALPHA * x + y, lng_ref[...], lnb_ref[...])

    hist_out_ref[0] = zbuf[pl.ds((tt + hp - POOL_HIST) * seqs, POOL_HIST * seqs), :]
    zbuf[pl.ds(0, hp * seqs), :] = zbuf[pl.ds(tt * seqs, hp * seqs), :]


def _mixer_ab(src, hist, w_in, lnv_g, lnv_b, ws, bs, wpool, pscale, w_out, lng, lnb, *, seqs, tt, pos0, with_v):
    nb = hist.shape[0]
    rows = tt * seqs
    n_t = src.n_rows // (nb * rows)
    kern = functools.partial(_mixer_ab_kernel, seqs=seqs, tt=tt, pos0=pos0, with_v=with_v)
    weights = (w_in, lnv_g, lnv_b, ws, bs, wpool, pscale, w_out, lng, lnb)
    io = _slab_io(src, rows, D_MODEL, lambda n, t: n * n_t + t, n_inputs=2 + len(weights))
    v_spec = pl.BlockSpec((1, rows, D_A), lambda n, t: (n, t, 0))
    hist_spec = pl.BlockSpec((1, POOL_HIST * seqs, D_B), lambda n, t: (n, 0, 0))
    v_shape = jax.ShapeDtypeStruct((nb, n_t * rows, D_A), F32)
    hist_shape = jax.ShapeDtypeStruct((nb, POOL_HIST * seqs, D_B), F32)
    return pl.pallas_call(
        io.adapt(kern),
        grid=(nb, n_t),
        in_specs=[io.in_spec, hist_spec] + [_param_spec(w) for w in weights] + io.alias_spec,
        out_specs=[io.out_spec, v_spec, hist_spec] if with_v else [io.out_spec, hist_spec],
        out_shape=[io.out_shape, v_shape, hist_shape] if with_v else [io.out_shape, hist_shape],
        input_output_aliases=io.aliases,
        scratch_shapes=[pltpu.VMEM(((POOL_HIST_PAD + tt) * seqs, D_B), F32)],
        compiler_params=_params("parallel", "arbitrary"),
        name="mixer_ab",
    )(src.array, hist, *map(_param_arg, weights), *io.alias_arg)


def _shift_rows(taps, hist_pad):
    base = hist_pad - (taps - 1)
    return max((base + k) // SUBLANES * SUBLANES for k in range(taps) if (base + k) % SUBLANES)


def _dwconv(buf, w_ref, out, shifted, *, taps, hist_pad, seqs, rows, chunk):
    base = (hist_pad - (taps - 1)) * seqs
    if shifted is not None:
        span = rows + _shift_rows(taps, hist_pad)
        for b in range(1, SUBLANES):
            shifted[b - 1] = buf[pl.ds(b, span), :]

    def tap(k, r0, n, lanes):
        off = base + k * seqs
        phase = off % SUBLANES
        if shifted is None or phase == 0:
            return buf[pl.ds(off + r0, n), lanes]
        return shifted[phase - 1, pl.ds(off - phase + r0, n), lanes]

    if shifted is None:
        def full_width(i, carry):
            r0 = i * chunk if isinstance(i, int) else pl.multiple_of(i * chunk, chunk)
            acc = w_ref[0:1, :] * tap(0, r0, chunk, slice(None))
            for k in range(1, taps):
                acc = acc + w_ref[k:k + 1, :] * tap(k, r0, chunk, slice(None))
            out[pl.ds(r0, chunk), :] = acc
            return carry

        if seqs % SUBLANES == 0:
            lax.fori_loop(0, rows // chunk, full_width, 0)
        else:
            for i in range(rows // chunk):
                full_width(i, 0)
        return

    for lt in range(buf.shape[1] // LANES):
        lanes = slice(lt * LANES, (lt + 1) * LANES)
        wk = [jnp.broadcast_to(w_ref[k:k + 1, lanes], (SUBLANES, LANES)) for k in range(taps)]
        for rg in range(0, rows, SUBLANES):
            acc = wk[0] * tap(0, rg, SUBLANES, lanes)
            for k in range(1, taps):
                acc = acc + wk[k] * tap(k, rg, SUBLANES, lanes)
            out[pl.ds(rg, SUBLANES), lanes] = acc


def _mixer_cd_kernel(x_ref, hist_c_ref, hist_d_ref, w_in_ref, ccw_ref, ccb_ref, lncg_ref, lncb_ref,
                     cdw_ref, w_out_ref, lng_ref, lnb_ref,
                     h_out_ref, hist_c_out_ref, hist_d_out_ref, cbuf, dbuf, cc_scr, cd_scr, *maybe_shifted,
                     seqs, tt):
    shifted = maybe_shifted[0] if maybe_shifted else None
    t_idx = pl.program_id(1)
    rows = tt * seqs
    hc, hd = CONV_C_HIST_PAD, CONV_D_HIST_PAD
    nc, nd = CONV_C - 1, CONV_D - 1

    @pl.when(t_idx == 0)
    def _():
        cbuf[pl.ds((hc - nc) * seqs, nc * seqs), :] = hist_c_ref[0]
        dbuf[pl.ds((hd - nd) * seqs, nd * seqs), :] = hist_d_ref[0]

    x = x_ref[...]
    h = _dot(x, w_in_ref[0])
    glu = h[:, :D_C] * jax.nn.sigmoid(h[:, D_C:2 * D_C])
    o = 2 * D_C
    gate_b = h[:, o:o + D_D]
    gx = h[:, o + D_D:o + 2 * D_D] * h[:, o + 2 * D_D:]
    cbuf[pl.ds(hc * seqs, rows), :] = glu
    dbuf[pl.ds(hd * seqs, rows), :] = gx

    chunk = min(rows, 32)
    _dwconv(cbuf, ccw_ref, cc_scr, shifted, taps=CONV_C, hist_pad=hc, seqs=seqs, rows=rows, chunk=chunk)
    _dwconv(dbuf, cdw_ref, cd_scr, None, taps=CONV_D, hist_pad=hd, seqs=seqs, rows=rows, chunk=chunk)

    c_out = jax.nn.silu(_layer_norm(cc_scr[...] + ccb_ref[...], lncg_ref[...], lncb_ref[...]))
    d_out = gate_b * cd_scr[...]
    y = _dot(jnp.concatenate([c_out, d_out], axis=-1), w_out_ref[0])
    h_out_ref[...] = _layer_norm(DN_ALPHA * x + y, lng_ref[...], lnb_ref[...])

    hist_c_out_ref[0] = cbuf[pl.ds((tt + hc - nc) * seqs, nc * seqs), :]
    hist_d_out_ref[0] = dbuf[pl.ds((tt + hd - nd) * seqs, nd * seqs), :]
    cbuf[pl.ds(0, hc * seqs), :] = cbuf[pl.ds(tt * seqs, hc * seqs), :]
    dbuf[pl.ds(0, hd * seqs), :] = dbuf[pl.ds(tt * seqs, hd * seqs), :]


def _mixer_cd(src, hist_c, hist_d, w_in, ccw, ccb, lncg, lncb, cdw, w_out, lng, lnb, *, seqs, tt):
    nb = hist_c.shape[0]
    rows = tt * seqs
    n_t = src.n_rows // (nb * rows)
    nc, nd = CONV_C - 1, CONV_D - 1
    kern = functools.partial(_mixer_cd_kernel, seqs=seqs, tt=tt)
    weights = (w_in, ccw, ccb, lncg, lncb, cdw, w_out, lng, lnb)
    io = _slab_io(src, rows, D_MODEL, lambda n, t: n * n_t + t, n_inputs=3 + len(weights))
    hist_c_spec = pl.BlockSpec((1, nc * seqs, D_C), lambda n, t: (n, 0, 0))
    hist_d_spec = pl.BlockSpec((1, nd * seqs, D_D), lambda n, t: (n, 0, 0))
    return pl.pallas_call(
        io.adapt(kern),
        grid=(nb, n_t),
        in_specs=[io.in_spec, hist_c_spec, hist_d_spec] + [_param_spec(w) for w in weights] + io.alias_spec,
        out_specs=[io.out_spec, hist_c_spec, hist_d_spec],
        out_shape=[
            io.out_shape,
            jax.ShapeDtypeStruct((nb, nc * seqs, D_C), F32),
            jax.ShapeDtypeStruct((nb, nd * seqs, D_D), F32),
        ],
        input_output_aliases=io.aliases,
        scratch_shapes=[
            pltpu.VMEM(((CONV_C_HIST_PAD + tt) * seqs, D_C), F32),
            pltpu.VMEM(((CONV_D_HIST_PAD + tt) * seqs, D_D), F32),
            pltpu.VMEM((rows, D_C), F32),
            pltpu.VMEM((rows, D_D), F32),
        ] + ([pltpu.VMEM((SUBLANES - 1, rows + _shift_rows(CONV_C, CONV_C_HIST_PAD), D_C), F32)]
             if seqs % SUBLANES else []),
        compiler_params=_params("parallel", "arbitrary"),
        name="mixer_cd",
    )(src.array, hist_c, hist_d, *map(_param_arg, weights), *io.alias_arg)


def _kv_proj_kernel(mem_ref, wk_ref, wv_ref, k_ref, v_ref, kview_ref, vview_ref):
    m = mem_ref[...].astype(BF16)
    halves = MEM_HEAD_DIM // LANES
    for w_ref, o_ref, view_ref in ((wk_ref, k_ref, kview_ref), (wv_ref, v_ref, vview_ref)):
        y = jnp.dot(m, w_ref[0], preferred_element_type=F32)
        o_ref[0] = y
        for hh in range(MEM_HEADS):
            for lt in range(halves):
                col = hh * MEM_HEAD_DIM + lt * LANES
                view_ref[0, 0, pl.ds(lt * MEM_HEADS + hh, N_MEM, stride=halves * MEM_HEADS), :] = (
                    y[:, col:col + LANES])


def _kv_proj(mem, wk, wv):
    rows = jax.ShapeDtypeStruct((DEPTH, BATCH * N_MEM, D_MODEL), F32)
    view = jax.ShapeDtypeStruct((DEPTH, BATCH, KV_ROWS, LANES), F32)
    w_spec = pl.BlockSpec((1, D_MODEL, D_MODEL), lambda l, n: (l, 0, 0))
    return pl.pallas_call(
        _kv_proj_kernel,
        grid=(DEPTH, BATCH),
        in_specs=[pl.BlockSpec((N_MEM, D_MODEL), lambda l, n: (n, 0)), w_spec, w_spec],
        out_specs=[pl.BlockSpec((1, N_MEM, D_MODEL), lambda l, n: (l, n, 0))] * 2
        + [pl.BlockSpec((1, 1, KV_ROWS, LANES), lambda l, n: (l, n, 0, 0))] * 2,
        out_shape=[rows, rows, view, view],
        compiler_params=_params("parallel", "arbitrary"),
        name="kv_proj",
    )(mem, wk, wv)


def _attend(q, k, v):
    scale = MEM_HEAD_DIM ** -0.5
    qb = q.astype(BF16)
    outs = []
    for hh in range(MEM_HEADS):
        lo, hi = hh * MEM_HEAD_DIM, (hh + 1) * MEM_HEAD_DIM
        s = lax.dot_general(qb[:, lo:hi], k[:, lo:hi], (((1,), (1,)), ((), ())),
                            preferred_element_type=F32) * scale
        s = s - jnp.max(s, axis=-1, keepdims=True)
        p = jnp.exp(s)
        p = p / jnp.sum(p, axis=-1, keepdims=True)
        outs.append(jnp.dot(p.astype(BF16), v[:, lo:hi], preferred_element_type=F32))
    return jnp.concatenate(outs, axis=-1)


def _attn_prompt_kernel(x_ref, k_ref, v_ref, wq_ref, wo_ref, wr_ref, br_ref, lng_ref, lnb_ref, o_ref):
    x = x_ref[...]
    q = _dot(x, wq_ref[0])
    o = _attend(q, k_ref[0].astype(BF16), v_ref[0].astype(BF16))
    y = _dot(o, wo_ref[0])
    h2 = _layer_norm(DN_ALPHA * x + y, lng_ref[...], lnb_ref[...])
    _store_wide_rows(o_ref, h2, _route(_router_logits(h2, wr_ref[0]) + br_ref[0]))


def _store_wide_rows(o_ref, h2, gates):
    rows = h2.shape[0]
    for c in range(WIDE_TILES):
        piece = gates if c == WIDE_TILES - 1 else h2[:, c * LANES:(c + 1) * LANES]
        o_ref[pl.ds(c, rows, stride=WIDE_TILES), :] = piece


def _attn_prompt(src, k, v, layer, wq, wo, wr, br, lng, lnb, *, tq):
    n_t = src.n_rows // (BATCH * tq)
    weights = (wq, wo, wr, br, lng, lnb)
    io = _slab_io(src, tq, D_WIDE, lambda n, i: n * n_t + i, n_inputs=3 + len(weights), row_major_tiles=True)
    kv_spec = pl.BlockSpec((1, N_MEM, D_MODEL), lambda n, i: (layer, n, 0))
    return pl.pallas_call(
        io.adapt(_attn_prompt_kernel),
        grid=(BATCH, n_t),
        in_specs=[io.in_spec, kv_spec, kv_spec] + [_param_spec(w) for w in weights] + io.alias_spec,
        out_specs=io.out_spec,
        out_shape=io.out_shape,
        input_output_aliases=io.aliases,
        compiler_params=_params("parallel", "arbitrary"),
        name="attn_prompt",
    )(src.array, k, v, *map(_param_arg, weights), *io.alias_arg)


def _att_row(half, head, t):
    return (half * MEM_HEADS + head) * DEC_SEQ + t


def _attn_sample_kernel(x_ref, k_ref, v_ref, wq_ref, wo_ref, wr_ref, br_ref, lng_ref, lnb_ref, o_ref,
                        q_scr, qp_scr, op_scr, *, seq_blk, n_blk):
    i = pl.program_id(0)
    halves = MEM_HEAD_DIM // LANES
    blocks = [(t, hh, lt) for t in range(DEC_SEQ) for hh in range(MEM_HEADS) for lt in range(halves)]

    @pl.when(i == 0)
    def _():
        q_scr[...] = _dot(x_ref[...], wq_ref[0])
        for t, hh, lt in blocks:
            col = hh * MEM_HEAD_DIM + lt * LANES
            qp_scr[pl.ds(_att_row(lt, hh, t), DEC_BATCH, stride=ATT_ROWS), :] = (
                q_scr[t * DEC_BATCH:(t + 1) * DEC_BATCH, col:col + LANES])

    half_rows = ATT_ROWS // halves
    col = lax.broadcasted_iota(jnp.int32, (half_rows, KV_ROWS), 1)
    row_head = lax.shift_right_logical(lax.broadcasted_iota(jnp.int32, (half_rows, KV_ROWS), 0),
                                       DEC_SEQ.bit_length() - 1)
    col_cls = col & (halves * MEM_HEADS - 1)
    match0 = col_cls == row_head
    match1 = col_cls == row_head + MEM_HEADS
    scale = MEM_HEAD_DIM ** -0.5

    def body(j, carry):
        n = i * seq_blk + j
        r0 = pl.multiple_of(n * ATT_ROWS, ATT_ROWS)
        qp = qp_scr[pl.ds(r0, ATT_ROWS), :].astype(BF16)
        s = lax.dot_general(qp, k_ref[0, j].astype(BF16), (((1,), (1,)), ((), ())),
                            preferred_element_type=F32)
        part = jnp.where(match0, s[:half_rows], 0.0) + pltpu.roll(
            jnp.where(match1, s[half_rows:], 0.0), KV_ROWS - MEM_HEADS, axis=1)
        sv = jnp.where(match0, part * scale, ROUTER_MASKED)
        e = jnp.exp(sv - jnp.max(sv, axis=-1, keepdims=True))
        p = e / jnp.sum(e, axis=-1, keepdims=True)
        pp = jnp.concatenate([p, pltpu.roll(p, MEM_HEADS, axis=1)], axis=0).astype(BF16)
        op_scr[pl.ds(r0, ATT_ROWS), :] = jnp.dot(pp, v_ref[0, j].astype(BF16), preferred_element_type=F32)
        return carry

    lax.fori_loop(0, seq_blk, body, 0, unroll=True)

    @pl.when(i == n_blk - 1)
    def _():
        for t, hh, lt in blocks:
            col = hh * MEM_HEAD_DIM + lt * LANES
            q_scr[t * DEC_BATCH:(t + 1) * DEC_BATCH, col:col + LANES] = (
                op_scr[pl.ds(_att_row(lt, hh, t), DEC_BATCH, stride=ATT_ROWS), :])
        x = x_ref[...]
        y = _dot(q_scr[...], wo_ref[0])
        h2 = _layer_norm(DN_ALPHA * x + y, lng_ref[...], lnb_ref[...])
        _store_wide_rows(o_ref, h2, _route(_router_logits(h2, wr_ref[0]) + br_ref[0]))


def _attn_sample(src, k, v, layer, wq, wo, wr, br, lng, lnb, *, seq_blk):
    m = src.n_rows
    n_blk = DEC_BATCH // seq_blk
    kern = functools.partial(_attn_sample_kernel, seq_blk=seq_blk, n_blk=n_blk)
    weights = (wq, wo, wr, br, lng, lnb)
    io = _slab_io(src, m, D_WIDE, lambda i: 0, n_inputs=3 + len(weights), row_major_tiles=True)
    kv_spec = pl.BlockSpec((1, seq_blk, KV_ROWS, LANES), lambda i: (layer, i, 0, 0))
    return pl.pallas_call(
        io.adapt(kern),
        grid=(n_blk,),
        in_specs=[io.in_spec, kv_spec, kv_spec] + [_param_spec(w) for w in weights] + io.alias_spec,
        out_specs=io.out_spec,
        out_shape=io.out_shape,
        input_output_aliases=io.aliases,
        scratch_shapes=[pltpu.VMEM((m, D_MODEL), F32),
                        pltpu.VMEM((DEC_BATCH * ATT_ROWS, LANES), F32),
                        pltpu.VMEM((DEC_BATCH * ATT_ROWS, LANES), F32)],
        compiler_params=_params("arbitrary"),
        name="attn_sample",
    )(src.array, k, v, *map(_param_arg, weights), *io.alias_arg)


def _split_bf16(x):
    hi = x.astype(BF16)
    return hi, (x - hi.astype(F32)).astype(BF16)


def _router_logits(h, w):
    h_hi, h_lo = _split_bf16(h)
    w_hi, w_lo = _split_bf16(w)
    a = jnp.dot(h_hi, jnp.concatenate([w_hi, w_lo], axis=-1), preferred_element_type=F32)
    b = jnp.dot(h_lo, w_hi, preferred_element_type=F32)
    return a[:, :ROUTER_LANES] + a[:, ROUTER_LANES:] + b


def _route(logits):
    lane = lax.broadcasted_iota(jnp.int32, logits.shape, 1)
    lane_f = lane.astype(F32)
    neg = ROUTER_MASKED
    is_grp = lane < N_GROUPS
    gl = jnp.where(is_grp, logits, neg)
    gmax = jnp.max(gl, axis=-1, keepdims=True)
    gsel = jnp.min(jnp.where(gl == gmax, lane_f, float(ROUTER_LANES)), axis=-1, keepdims=True)
    g_w = 1.0 / jnp.sum(jnp.exp(gl - gmax), axis=-1, keepdims=True)
    e_idx = lane - N_GROUPS
    e_grp = lax.shift_right_arithmetic(e_idx, 2).astype(F32)
    in_grp = (e_idx >= 0) & (e_idx < N_EXPERTS) & (e_grp == gsel)
    el = jnp.where(in_grp, logits, neg)
    m1 = jnp.max(el, axis=-1, keepdims=True)
    i1 = jnp.min(jnp.where(el == m1, lane_f, float(ROUTER_LANES)), axis=-1, keepdims=True)
    el2 = jnp.where(lane_f == i1, neg, el)
    m2 = jnp.max(el2, axis=-1, keepdims=True)
    i2 = jnp.min(jnp.where(el2 == m2, lane_f, float(ROUTER_LANES)), axis=-1, keepdims=True)
    e2 = jnp.exp(m2 - m1)
    den = 1.0 + e2
    w1 = (1.0 / den) * g_w
    w2 = (e2 / den) * g_w
    first_lane = N_GROUPS + EXPERTS_PER_GROUP * gsel
    a = jnp.minimum(i1, i2) - first_lane
    b = jnp.maximum(i1, i2) - first_lane
    cls = gsel * len(PAIRS)
    for j, (pa, pb) in enumerate(PAIRS):
        cls = cls + jnp.where((a == pa) & (b == pb), float(j), 0.0)
    return (jnp.where(lane_f == i1, w1, 0.0) + jnp.where(lane_f == i2, w2, 0.0)
            + jnp.where(lane == CLASS_LANE, cls, 0.0))


def _gate_column(gates, expert):
    lane = lax.broadcasted_iota(jnp.int32, gates.shape, 1)
    return jnp.sum(jnp.where(lane == expert + N_GROUPS, gates, 0.0), axis=-1, keepdims=True)


def _expert_hidden(xb, wg, wu, gate):
    hg = jnp.dot(xb, wg, preferred_element_type=F32)
    hu = jnp.dot(xb, wu, preferred_element_type=F32)
    return (jax.nn.silu(hg) * hu * gate).astype(BF16)


FLAG_ACTIVE, FLAG_FIRST, FLAG_LAST, FLAG_NEW_A, FLAG_NEW_B = 1, 2, 4, 8, 16


def _moe_sparse_kernel(src_ref, tile_ref, ea_ref, eb_ref, lo_ref, hi_ref, flag_ref,
                       x_hbm, wga_ref, wua_ref, wda_ref, wgb_ref, wub_ref, wdb_ref, lng_ref, lnb_ref,
                       out_hbm, xbuf, obuf, acc, wg_s, wu_s, wd_s, gsem, ssem, *, tm, n_tiles, n_items):
    i = pl.program_id(0)
    t = tile_ref[i]
    slot = t % 2
    flags = flag_ref[i]

    def start_gather(tile, s):
        for r in range(tm):
            tok = src_ref[tile * tm + r]
            pltpu.make_async_copy(x_hbm.at[pl.ds(tok * WIDE_TILES, WIDE_TILES)],
                                  xbuf.at[s, pl.ds(r * WIDE_TILES, WIDE_TILES)], gsem.at[s]).start()

    def piece(c):
        return xbuf[slot, pl.ds(c, tm, stride=WIDE_TILES), :]

    def activations():
        return jnp.concatenate([piece(c) for c in range(D_MODEL // LANES)], axis=-1)

    def start_scatter(tile, s):
        for r in range(tm):
            tok = src_ref[tile * tm + r]
            pltpu.make_async_copy(obuf.at[s, pl.ds(r, 1)], out_hbm.at[pl.ds(tok, 1)],
                                  ssem.at[s]).start(priority=r % 2)

    def for_slot(value, fn):
        for s in (0, 1):
            pl.when(value == s)(functools.partial(fn, s))

    def wait_gather(s):
        pltpu.make_async_copy(x_hbm.at[pl.ds(0, tm * WIDE_TILES)], xbuf.at[s], gsem.at[s]).wait()

    def wait_scatter(s):
        pltpu.make_async_copy(obuf.at[s], out_hbm.at[pl.ds(0, tm)], ssem.at[s]).wait()

    @pl.when(i == 0)
    def _():
        start_gather(0, 0)

    @pl.when((flags & FLAG_FIRST) != 0)
    def _():
        @pl.when(t + 1 < n_tiles)
        def _():
            for_slot(1 - slot, lambda s: start_gather(t + 1, s))
        wait_gather(slot)
        acc[...] = jnp.zeros_like(acc)

    @pl.when((flags & FLAG_NEW_A) != 0)
    def _():
        wg_s[0] = wga_ref[0, 0].astype(BF16)
        wu_s[0] = wua_ref[0, 0].astype(BF16)
        wd_s[0] = wda_ref[0, 0].astype(BF16)

    @pl.when((flags & FLAG_NEW_B) != 0)
    def _():
        wg_s[1] = wgb_ref[0, 0].astype(BF16)
        wu_s[1] = wub_ref[0, 0].astype(BF16)
        wd_s[1] = wdb_ref[0, 0].astype(BF16)

    @pl.when((flags & FLAG_ACTIVE) != 0)
    def _():
        xb = activations().astype(BF16)
        gates = piece(WIDE_TILES - 1)
        row = t * tm + lax.broadcasted_iota(jnp.int32, (tm, 1), 0)
        in_class = (row >= lo_ref[i]) & (row < hi_ref[i])
        gate_a = jnp.where(in_class, _gate_column(gates, ea_ref[i]), 0.0)
        gate_b = jnp.where(in_class, _gate_column(gates, eb_ref[i]), 0.0)
        hid_a = _expert_hidden(xb, wg_s[0], wu_s[0], gate_a)
        hid_b = _expert_hidden(xb, wg_s[1], wu_s[1], gate_b)
        acc[...] += (jnp.dot(hid_a, wd_s[0], preferred_element_type=F32)
                     + jnp.dot(hid_b, wd_s[1], preferred_element_type=F32))

    @pl.when((flags & FLAG_LAST) != 0)
    def _():
        @pl.when(t >= 2)
        def _():
            wait_scatter(slot)
        obuf[slot] = _layer_norm(DN_ALPHA * activations() + acc[...], lng_ref[...], lnb_ref[...])
        for_slot(slot, lambda s: start_scatter(t, s))

    @pl.when(i == n_items - 1)
    def _():
        wait_scatter(0)
        wait_scatter(1)


def _moe_schedule(cls, *, tm):
    m = cls.shape[0]
    n_tiles = m // tm
    n_items = n_tiles + N_CLASSES - 1
    i32 = jnp.int32
    src = jnp.argsort(cls, stable=True).astype(i32)
    counts = jnp.sum((cls[:, None] == jnp.arange(N_CLASSES, dtype=i32)[None, :]).astype(i32), axis=0)
    ends = jnp.cumsum(counts)
    starts = ends - counts
    def count(mask):
        return jnp.sum(mask.astype(i32), axis=-1)

    def pick(onehot, values):
        return jnp.sum(jnp.where(onehot, values[None, :], 0), axis=-1)

    tile_lo = jnp.arange(n_tiles, dtype=i32) * tm
    first_c = count(ends[None, :] <= tile_lo[:, None])
    last_c = count(starts[None, :] < (tile_lo + tm)[:, None]) - 1
    per_tile = last_c - first_c + 1
    item_end = jnp.cumsum(per_tile)
    idx = jnp.arange(n_items, dtype=i32)
    tile = jnp.minimum(count(item_end[None, :] <= idx[:, None]), n_tiles - 1)
    in_tile = tile[:, None] == jnp.arange(n_tiles, dtype=i32)[None, :]
    end = pick(in_tile, item_end)
    begin = end - pick(in_tile, per_tile)
    active = idx < item_end[-1]
    c = jnp.where(active, pick(in_tile, first_c) + idx - begin, last_c[n_tiles - 1])
    in_class = c[:, None] == jnp.arange(N_CLASSES, dtype=i32)[None, :]
    class_a = [g * EXPERTS_PER_GROUP + pa for g in range(N_GROUPS) for pa, _ in PAIRS]
    class_b = [g * EXPERTS_PER_GROUP + pb for g in range(N_GROUPS) for _, pb in PAIRS]
    ea = pick(in_class, jnp.asarray(class_a, dtype=i32))
    eb = pick(in_class, jnp.asarray(class_b, dtype=i32))
    lo = jnp.where(active, pick(in_class, starts), 0)
    hi = jnp.where(active, pick(in_class, ends), 0)
    new_a = jnp.concatenate([jnp.ones((1,), bool), ea[1:] != ea[:-1]])
    new_b = jnp.concatenate([jnp.ones((1,), bool), eb[1:] != eb[:-1]])
    flags = (active * FLAG_ACTIVE + (active & (idx == begin)) * FLAG_FIRST
             + (active & (idx == end - 1)) * FLAG_LAST + new_a * FLAG_NEW_A + new_b * FLAG_NEW_B)
    return src, tile, ea, eb, lo.astype(i32), hi.astype(i32), flags.astype(i32)


def _moe_sparse(x, layer, wg, wu, wd, lng, lnb, *, tm):
    m = x.shape[0] // WIDE_TILES
    n_tiles = m // tm
    n_items = n_tiles + N_CLASSES - 1
    cls = x[WIDE_TILES - 1::WIDE_TILES, CLASS_LANE].astype(jnp.int32)
    tables = _moe_schedule(cls, tm=tm)
    kern = functools.partial(_moe_sparse_kernel, tm=tm, n_tiles=n_tiles, n_items=n_items)

    def w_spec(shape, which):
        return pl.BlockSpec((1, 1) + shape, lambda i, src, tile, ea, eb, lo, hi, fl: (layer, (ea, eb)[which][i], 0, 0))

    up, down = (D_MODEL, D_EXPERT), (D_EXPERT, D_MODEL)
    grid_spec = pltpu.PrefetchScalarGridSpec(
        num_scalar_prefetch=len(tables),
        grid=(n_items,),
        in_specs=[
            pl.BlockSpec(memory_space=pl.ANY),
            w_spec(up, 0), w_spec(up, 0), w_spec(down, 0), w_spec(up, 1), w_spec(up, 1), w_spec(down, 1),
            _const_spec(lng.shape), _const_spec(lnb.shape),
        ],
        out_specs=pl.BlockSpec(memory_space=pl.ANY),
        scratch_shapes=[
            pltpu.VMEM((2, tm * WIDE_TILES, LANES), F32), pltpu.VMEM((2, tm, D_MODEL), F32),
            pltpu.VMEM((tm, D_MODEL), F32),
            pltpu.VMEM((2,) + up, BF16), pltpu.VMEM((2,) + up, BF16), pltpu.VMEM((2,) + down, BF16),
            pltpu.SemaphoreType.DMA((2,)), pltpu.SemaphoreType.DMA((2,)),
        ],
    )
    return pl.pallas_call(
        kern,
        grid_spec=grid_spec,
        out_shape=jax.ShapeDtypeStruct((m, D_MODEL), F32),
        compiler_params=_params("arbitrary"),
        name="moe_sparse",
    )(*tables, x, wg, wu, wd, wg, wu, wd, lng, lnb)


def _row(v):
    return v.reshape(1, -1)


def _to_time_major(s):
    return jnp.transpose(s, (1, 0, 2)).reshape(1, -1, s.shape[-1])


def _from_time_major(s, steps):
    return jnp.transpose(s.reshape(steps, DEC_BATCH, s.shape[-1]), (1, 0, 2))


def _kv_lane_view(cache):
    d, n = cache.shape[:2]
    halves = MEM_HEAD_DIM // LANES
    v = cache.reshape(d, n, N_MEM, MEM_HEADS, halves, LANES)
    return jnp.transpose(v, (0, 1, 2, 4, 3, 5)).reshape(d, n, KV_ROWS, LANES)


def _kv_from_lane_view(view):
    d, n = view.shape[:2]
    halves = MEM_HEAD_DIM // LANES
    v = view.reshape(d, n, N_MEM, halves, MEM_HEADS, LANES)
    return jnp.transpose(v, (0, 1, 2, 4, 3, 5)).reshape(d, n, N_MEM, MEM_HEADS, MEM_HEAD_DIM)


def kernel(x_prompt, x_sample, cache_mem_k, cache_mem_v, state_pool, state_conv_c, state_conv_d, mem_prompt,
           w_in_ab, ln_v_g, ln_v_b, w_spatial, b_spatial, w_pool, pool_scale, w_out_ab,
           w_in_cd, conv_c_w, conv_c_b, ln_c_g, ln_c_b, conv_d_w, w_out_cd,
           w_q, w_k, w_v, w_o, w_group, b_group, w_router, b_router, w_gate, w_up, w_down, ln_g, ln_b):
    mem_flat = mem_prompt.reshape(BATCH * N_MEM, D_MODEL)
    cache_k = _kv_lane_view(cache_mem_k)
    cache_v = _kv_lane_view(cache_mem_v)
    kp, vp, kp_view, vp_view = _kv_proj(mem_flat, w_k.astype(BF16), w_v.astype(BF16))

    w_in_ab, w_out_ab, w_pool = w_in_ab.astype(BF16), w_out_ab.astype(BF16), w_pool.astype(BF16)
    w_in_cd, w_out_cd = w_in_cd.astype(BF16), w_out_cd.astype(BF16)
    w_q, w_o = w_q.astype(BF16), w_o.astype(BF16)
    pad = ROUTER_LANES - N_GROUPS - N_EXPERTS
    w_route = jnp.pad(jnp.concatenate([w_group, w_router], axis=-1), ((0, 0), (0, 0), (0, pad)))
    b_route = jnp.pad(jnp.concatenate([b_group, b_router], axis=-1), ((0, 0), (0, pad)))[:, None, :]

    prompt = _Slab(x_prompt.reshape(PROMPT_ROWS, D_MODEL), 0, PROMPT_ROWS, 0, None)
    sample = _Slab(_to_time_major(x_sample)[0], 0, SAMPLE_ROWS, PROMPT_ROWS, None)

    def both(h):
        return (_Slab(h, 0, PROMPT_ROWS, 0, None), _Slab(h, PROMPT_ROWS, SAMPLE_ROWS, PROMPT_ROWS, None))

    pool_p, pool_s, chunk_v_s = [], [], []
    conv_c_p, conv_c_s, conv_d_p, conv_d_s = [], [], [], []

    for l in range(DEPTH):
        i = l // 2
        lng0, lnb0 = _row(ln_g[l, 0]), _row(ln_b[l, 0])
        if l % 2 == 0:
            bias = jnp.repeat(b_spatial[i].T, A_HEAD_DIM, axis=1)
            small = jnp.repeat(
                jnp.transpose(w_spatial[i][:, :DEC_SEQ, :DEC_SEQ], (1, 2, 0)).reshape(DEC_SEQ * DEC_SEQ, A_HEADS),
                A_HEAD_DIM, axis=1)
            common = (_Layer(w_in_ab, i), _row(ln_v_g[i]), _row(ln_v_b[i]))
            tail = (_Layer(w_pool, i), _row(pool_scale[i]), _Layer(w_out_ab, i), lng0, lnb0)
            h, hist_p = _mixer_ab(prompt, jnp.zeros((BATCH, POOL_HIST, D_B), F32), *common, w_spatial[i], bias,
                                  *tail, seqs=1, tt=512, pos0=0, with_v=False)
            h, v_rows, hist_s = _mixer_ab(sample._replace(dst=h), _to_time_major(state_pool[i]), *common, small,
                                          bias[:DEC_SEQ], *tail, seqs=DEC_BATCH, tt=DEC_SEQ, pos0=PAST_LEN,
                                          with_v=True)
            pool_p.append(hist_p)
            pool_s.append(_from_time_major(hist_s, POOL_HIST))
            chunk_v_s.append(_from_time_major(v_rows, DEC_SEQ))
        else:
            wts = (_Layer(w_in_cd, i), conv_c_w[i], _row(conv_c_b[i]), _row(ln_c_g[i]), _row(ln_c_b[i]),
                   conv_d_w[i], _Layer(w_out_cd, i), lng0, lnb0)
            h, hc_p, hd_p = _mixer_cd(prompt, jnp.zeros((BATCH, CONV_C - 1, D_C), F32),
                                      jnp.zeros((BATCH, CONV_D - 1, D_D), F32), *wts, seqs=1, tt=512)
            h, hc_s, hd_s = _mixer_cd(sample._replace(dst=h), _to_time_major(state_conv_c[i]),
                                      _to_time_major(state_conv_d[i]), *wts, seqs=DEC_BATCH, tt=DEC_SEQ)
            conv_c_p.append(hc_p)
            conv_d_p.append(hd_p)
            conv_c_s.append(_from_time_major(hc_s, CONV_C - 1))
            conv_d_s.append(_from_time_major(hd_s, CONV_D - 1))
        prompt, sample = both(h)

        att_w = (_Layer(w_q, l), _Layer(w_o, l), _Layer(w_route, l), _Layer(b_route, l),
                 _row(ln_g[l, 1]), _row(ln_b[l, 1]))
        h_wide = _attn_prompt(prompt, kp, vp, l, *att_w, tq=512)
        h_wide = _attn_sample(sample._replace(dst=h_wide), cache_k, cache_v, l, *att_w, seq_blk=4)

        h = _moe_sparse(h_wide, l, w_gate, w_up, w_down, _row(ln_g[l, 2]), _row(ln_b[l, 2]), tm=256)
        prompt, sample = both(h)

    y_prompt = h[:PROMPT_ROWS].reshape(BATCH, SEQ, D_MODEL)
    y_sample = _from_time_major(h[PROMPT_ROWS:], DEC_SEQ)
    return (y_prompt, y_sample, _kv_from_lane_view(kp_view), _kv_from_lane_view(vp_view), jnp.stack(pool_p),
            jnp.stack(conv_c_p), jnp.stack(conv_d_p), jnp.stack(chunk_v_s), jnp.stack(pool_s),
            jnp.stack(conv_c_s), jnp.stack(conv_d_s))
```

```python
import functools
from typing import Callable, NamedTuple, Optional

import jax
import jax.numpy as jnp
from jax import lax
from jax.experimental import pallas as pl
from jax.experimental.pallas import tpu as pltpu

D_MODEL = 1024
BATCH = 8
SEQ = 2048
DEPTH = 4
DEC_BATCH = 128
DEC_SEQ = 4
PAST_LEN = 16384

CHUNK = 128
A_HEADS = 4
D_A = D_MODEL // 2
A_HEAD_DIM = D_A // A_HEADS
POOL_WINDOWS = (2, 4, 8, 16)
B_GROUPS = len(POOL_WINDOWS)
D_B = D_MODEL // 2
B_GROUP_DIM = D_B // B_GROUPS
POOL_HIST = max(POOL_WINDOWS) - 1
D_C = D_MODEL // 2
CONV_C = 31
D_D = D_MODEL // 2
CONV_D = 3
N_MEM = 256
MEM_HEADS = 4
MEM_HEAD_DIM = D_MODEL // MEM_HEADS
N_GROUPS = 4
EXPERTS_PER_GROUP = 4
N_EXPERTS = N_GROUPS * EXPERTS_PER_GROUP
D_EXPERT = 512
DN_ALPHA = (2 * DEPTH) ** 0.25
LN_EPS = 1e-5
PROMPT_ROWS = BATCH * SEQ
SAMPLE_ROWS = DEC_BATCH * DEC_SEQ
TOTAL_ROWS = PROMPT_ROWS + SAMPLE_ROWS

LANES = 128
SUBLANES = 8
POOL_HIST_PAD = 16
CONV_C_HIST_PAD = 32
CONV_D_HIST_PAD = 8
ROUTER_LANES = 128
ROUTER_MASKED = -1e30
CLASS_LANE = N_GROUPS + N_EXPERTS
PAIRS = ((0, 1), (0, 2), (1, 2), (1, 3), (0, 3), (2, 3))
N_CLASSES = N_GROUPS * len(PAIRS)
D_WIDE = D_MODEL + ROUTER_LANES
WIDE_TILES = D_WIDE // LANES
KV_ROWS = N_MEM * MEM_HEADS * MEM_HEAD_DIM // LANES
ATT_ROWS = KV_ROWS // N_MEM * DEC_SEQ
VMEM_LIMIT_BYTES = 52 * 1024 * 1024

F32 = jnp.float32
BF16 = jnp.bfloat16


def _layer_norm(x, g, b):
    mu = jnp.mean(x, axis=-1, keepdims=True)
    xc = x - mu
    var = jnp.mean(xc * xc, axis=-1, keepdims=True)
    return xc * lax.rsqrt(var + LN_EPS) * g + b


def _dot(a, b):
    return jnp.dot(a.astype(BF16), b.astype(BF16), preferred_element_type=F32)


def _params(*semantics):
    return pltpu.CompilerParams(dimension_semantics=semantics, vmem_limit_bytes=VMEM_LIMIT_BYTES)


def _const_spec(shape):
    nd = len(shape)
    return pl.BlockSpec(shape, lambda *_: (0,) * nd)


class _Layer(NamedTuple):
    stack: jax.Array
    layer: int


def _param_spec(p):
    if isinstance(p, _Layer):
        nd = p.stack.ndim
        return pl.BlockSpec((1,) + p.stack.shape[1:], lambda *_: (p.layer,) + (0,) * (nd - 1))
    return _const_spec(p.shape)


def _param_arg(p):
    return p.stack if isinstance(p, _Layer) else p


class _Slab(NamedTuple):
    array: jax.Array
    row0: int
    n_rows: int
    out_row0: int
    dst: Optional[tuple]


class _SlabIO(NamedTuple):
    in_spec: pl.BlockSpec
    out_specs: list
    out_shapes: list
    alias_spec: list
    alias_arg: tuple
    aliases: dict
    adapt: Callable


def _slab_io(src, rows, outs, block_index, *, n_inputs):
    assert src.row0 % rows == 0 and src.out_row0 % rows == 0 and src.n_rows % rows == 0
    in0, out0 = src.row0 // rows, src.out_row0 // rows
    in_spec = pl.BlockSpec((rows, src.array.shape[1]), lambda *g: (in0 + block_index(*g), 0))
    out_specs, out_shapes = [], []
    for width, row_major_tiles in outs:
        pieces, lanes = (width // LANES, LANES) if row_major_tiles else (1, width)
        out_specs.append(pl.BlockSpec((rows * pieces, lanes), lambda *g: (out0 + block_index(*g), 0)))
        out_shapes.append(jax.ShapeDtypeStruct((TOTAL_ROWS * pieces, lanes), F32))
    if src.dst is None:
        return _SlabIO(in_spec, out_specs, out_shapes, [], (), {}, lambda kernel: kernel)
    n_dst = len(outs)
    assert len(src.dst) == n_dst

    def adapt(kernel):
        return lambda *refs: kernel(*refs[:n_inputs], *refs[n_inputs + n_dst:])

    return _SlabIO(in_spec, out_specs, out_shapes, [pl.BlockSpec(memory_space=pl.ANY)] * n_dst, tuple(src.dst),
                   {n_inputs + j: j for j in range(n_dst)}, adapt)


def _mixer_ab_kernel(x_ref, hist_ref, w_in_ref, lnv_g_ref, lnv_b_ref, ws_ref, bs_ref, wpool_ref,
                     pscale_ref, w_out_ref, lng_ref, lnb_ref, *refs, seqs, tt, pos0, with_v):
    if with_v:
        h_out_ref, v_out_ref, hist_out_ref, zbuf = refs
    else:
        h_out_ref, hist_out_ref, zbuf = refs
    t_idx = pl.program_id(1)
    rows = tt * seqs
    hp = POOL_HIST_PAD

    @pl.when(t_idx == 0)
    def _():
        zbuf[pl.ds((hp - POOL_HIST) * seqs, POOL_HIST * seqs), :] = hist_ref[0]

    x = x_ref[...]
    h = _dot(x, w_in_ref[0])
    ua = jax.nn.gelu(h[:, :2 * D_A])
    u = ua[:, :D_A]
    v = _layer_norm(ua[:, D_A:], lnv_g_ref[...], lnv_b_ref[...])
    z = h[:, 2 * D_A:]
    if with_v:
        v_out_ref[0] = v
    zbuf[pl.ds(hp * seqs, rows), :] = z

    if seqs == 1:
        tri = (lax.broadcasted_iota(jnp.int32, (CHUNK, CHUNK), 0)
               >= lax.broadcasted_iota(jnp.int32, (CHUNK, CHUNK), 1))
        w_heads = [jnp.where(tri, ws_ref[hh], 0.0).astype(BF16) for hh in range(A_HEADS)]
        vb = v.astype(BF16)
        chunks = []
        for c in range(tt // CHUNK):
            heads = []
            for hh in range(A_HEADS):
                vc = vb[c * CHUNK:(c + 1) * CHUNK, hh * A_HEAD_DIM:(hh + 1) * A_HEAD_DIM]
                heads.append(jnp.dot(w_heads[hh], vc, preferred_element_type=F32))
            chunks.append(jnp.concatenate(heads, axis=-1) + bs_ref[...])
        mixed = jnp.concatenate(chunks, axis=0)
    else:
        parts = []
        for t in range(tt):
            acc = bs_ref[t:t + 1, :]
            for s in range(t + 1):
                acc = acc + ws_ref[t * tt + s:t * tt + s + 1, :] * v[s * seqs:(s + 1) * seqs, :]
            parts.append(acc)
        mixed = jnp.concatenate(parts, axis=0)
    a_out = u * mixed

    outs = []
    for gi, w in enumerate(POOL_WINDOWS):
        lo, hi = gi * B_GROUP_DIM, (gi + 1) * B_GROUP_DIM
        acc = zbuf[pl.ds(hp * seqs, rows), lo:hi]
        for j in range(1, w):
            acc = acc + zbuf[pl.ds((hp - j) * seqs, rows), lo:hi]
        if pos0 + 1 >= w:
            cnt = float(w)
        else:
            assert seqs == 1
            pos = pos0 + t_idx * tt + lax.broadcasted_iota(jnp.int32, (rows, 1), 0)
            cnt = jnp.minimum(pos + 1, w).astype(F32)
        pooled = acc / cnt - z[:, lo:hi]
        outs.append(_dot(pooled, wpool_ref[0, gi]))
    b_out = jnp.concatenate(outs, axis=-1) * pscale_ref[...]

    y = _dot(jnp.concatenate([a_out, b_out], axis=-1), w_out_ref[0])
    h_out_ref[...] = _layer_norm(DN_ALPHA * x + y, lng_ref[...], lnb_ref[...])

    hist_out_ref[0] = zbuf[pl.ds((tt + hp - POOL_HIST) * seqs, POOL_HIST * seqs), :]
    zbuf[pl.ds(0, hp * seqs), :] = zbuf[pl.ds(tt * seqs, hp * seqs), :]


def _mixer_ab(src, hist, w_in, lnv_g, lnv_b, ws, bs, wpool, pscale, w_out, lng, lnb, *, seqs, tt, pos0, with_v):
    nb = hist.shape[0]
    rows = tt * seqs
    n_t = src.n_rows // (nb * rows)
    kern = functools.partial(_mixer_ab_kernel, seqs=seqs, tt=tt, pos0=pos0, with_v=with_v)
    weights = (w_in, lnv_g, lnv_b, ws, bs, wpool, pscale, w_out, lng, lnb)
    io = _slab_io(src, rows, [(D_MODEL, False)], lambda n, t: n * n_t + t, n_inputs=2 + len(weights))
    v_spec = pl.BlockSpec((1, rows, D_A), lambda n, t: (n, t, 0))
    hist_spec = pl.BlockSpec((1, POOL_HIST * seqs, D_B), lambda n, t: (n, 0, 0))
    v_shape = jax.ShapeDtypeStruct((nb, n_t * rows, D_A), F32)
    hist_shape = jax.ShapeDtypeStruct((nb, POOL_HIST * seqs, D_B), F32)
    return pl.pallas_call(
        io.adapt(kern),
        grid=(nb, n_t),
        in_specs=[io.in_spec, hist_spec] + [_param_spec(w) for w in weights] + io.alias_spec,
        out_specs=io.out_specs + ([v_spec, hist_spec] if with_v else [hist_spec]),
        out_shape=io.out_shapes + ([v_shape, hist_shape] if with_v else [hist_shape]),
        input_output_aliases=io.aliases,
        scratch_shapes=[pltpu.VMEM(((POOL_HIST_PAD + tt) * seqs, D_B), F32)],
        compiler_params=_params("parallel", "arbitrary"),
        name="mixer_ab",
    )(src.array, hist, *map(_param_arg, weights), *io.alias_arg)


def _shift_rows(taps, hist_pad):
    base = hist_pad - (taps - 1)
    return max((base + k) // SUBLANES * SUBLANES for k in range(taps) if (base + k) % SUBLANES)


def _dwconv(buf, w_ref, out, shifted, *, taps, hist_pad, seqs, rows, chunk):
    base = (hist_pad - (taps - 1)) * seqs
    if shifted is not None:
        span = rows + _shift_rows(taps, hist_pad)
        for b in range(1, SUBLANES):
            shifted[b - 1] = buf[pl.ds(b, span), :]

    def tap(k, r0, n, lanes):
        off = base + k * seqs
        phase = off % SUBLANES
        if shifted is None or phase == 0:
            return buf[pl.ds(off + r0, n), lanes]
        return shifted[phase - 1, pl.ds(off - phase + r0, n), lanes]

    if shifted is None:
        def full_width(i, carry):
            r0 = i * chunk if isinstance(i, int) else pl.multiple_of(i * chunk, chunk)
            acc = w_ref[0:1, :] * tap(0, r0, chunk, slice(None))
            for k in range(1, taps):
                acc = acc + w_ref[k:k + 1, :] * tap(k, r0, chunk, slice(None))
            out[pl.ds(r0, chunk), :] = acc
            return carry

        if seqs % SUBLANES == 0:
            lax.fori_loop(0, rows // chunk, full_width, 0)
        else:
            for i in range(rows // chunk):
                full_width(i, 0)
        return

    for lt in range(buf.shape[1] // LANES):
        lanes = slice(lt * LANES, (lt + 1) * LANES)
        wk = [jnp.broadcast_to(w_ref[k:k + 1, lanes], (SUBLANES, LANES)) for k in range(taps)]
        for rg in range(0, rows, SUBLANES):
            acc = wk[0] * tap(0, rg, SUBLANES, lanes)
            for k in range(1, taps):
                acc = acc + wk[k] * tap(k, rg, SUBLANES, lanes)
            out[pl.ds(rg, SUBLANES), lanes] = acc


def _mixer_cd_kernel(x_ref, hist_c_ref, hist_d_ref, w_in_ref, ccw_ref, ccb_ref, lncg_ref, lncb_ref,
                     cdw_ref, w_out_ref, lng_ref, lnb_ref,
                     h_out_ref, hist_c_out_ref, hist_d_out_ref, cbuf, dbuf, cc_scr, cd_scr, *maybe_shifted,
                     seqs, tt):
    shifted = maybe_shifted[0] if maybe_shifted else None
    t_idx = pl.program_id(1)
    rows = tt * seqs
    hc, hd = CONV_C_HIST_PAD, CONV_D_HIST_PAD
    nc, nd = CONV_C - 1, CONV_D - 1

    @pl.when(t_idx == 0)
    def _():
        cbuf[pl.ds((hc - nc) * seqs, nc * seqs), :] = hist_c_ref[0]
        dbuf[pl.ds((hd - nd) * seqs, nd * seqs), :] = hist_d_ref[0]

    x = x_ref[...]
    h = _dot(x, w_in_ref[0])
    glu = h[:, :D_C] * jax.nn.sigmoid(h[:, D_C:2 * D_C])
    o = 2 * D_C
    gate_b = h[:, o:o + D_D]
    gx = h[:, o + D_D:o + 2 * D_D] * h[:, o + 2 * D_D:]
    cbuf[pl.ds(hc * seqs, rows), :] = glu
    dbuf[pl.ds(hd * seqs, rows), :] = gx

    chunk = min(rows, 32)
    _dwconv(cbuf, ccw_ref, cc_scr, shifted, taps=CONV_C, hist_pad=hc, seqs=seqs, rows=rows, chunk=chunk)
    _dwconv(dbuf, cdw_ref, cd_scr, None, taps=CONV_D, hist_pad=hd, seqs=seqs, rows=rows, chunk=chunk)

    c_out = jax.nn.silu(_layer_norm(cc_scr[...] + ccb_ref[...], lncg_ref[...], lncb_ref[...]))
    d_out = gate_b * cd_scr[...]
    y = _dot(jnp.concatenate([c_out, d_out], axis=-1), w_out_ref[0])
    h_out_ref[...] = _layer_norm(DN_ALPHA * x + y, lng_ref[...], lnb_ref[...])

    hist_c_out_ref[0] = cbuf[pl.ds((tt + hc - nc) * seqs, nc * seqs), :]
    hist_d_out_ref[0] = dbuf[pl.ds((tt + hd - nd) * seqs, nd * seqs), :]
    cbuf[pl.ds(0, hc * seqs), :] = cbuf[pl.ds(tt * seqs, hc * seqs), :]
    dbuf[pl.ds(0, hd * seqs), :] = dbuf[pl.ds(tt * seqs, hd * seqs), :]


def _mixer_cd(src, hist_c, hist_d, w_in, ccw, ccb, lncg, lncb, cdw, w_out, lng, lnb, *, seqs, tt):
    nb = hist_c.shape[0]
    rows = tt * seqs
    n_t = src.n_rows // (nb * rows)
    nc, nd = CONV_C - 1, CONV_D - 1
    kern = functools.partial(_mixer_cd_kernel, seqs=seqs, tt=tt)
    weights = (w_in, ccw, ccb, lncg, lncb, cdw, w_out, lng, lnb)
    io = _slab_io(src, rows, [(D_MODEL, False)], lambda n, t: n * n_t + t, n_inputs=3 + len(weights))
    hist_c_spec = pl.BlockSpec((1, nc * seqs, D_C), lambda n, t: (n, 0, 0))
    hist_d_spec = pl.BlockSpec((1, nd * seqs, D_D), lambda n, t: (n, 0, 0))
    return pl.pallas_call(
        io.adapt(kern),
        grid=(nb, n_t),
        in_specs=[io.in_spec, hist_c_spec, hist_d_spec] + [_param_spec(w) for w in weights] + io.alias_spec,
        out_specs=io.out_specs + [hist_c_spec, hist_d_spec],
        out_shape=io.out_shapes + [
            jax.ShapeDtypeStruct((nb, nc * seqs, D_C), F32),
            jax.ShapeDtypeStruct((nb, nd * seqs, D_D), F32),
        ],
        input_output_aliases=io.aliases,
        scratch_shapes=[
            pltpu.VMEM(((CONV_C_HIST_PAD + tt) * seqs, D_C), F32),
            pltpu.VMEM(((CONV_D_HIST_PAD + tt) * seqs, D_D), F32),
            pltpu.VMEM((rows, D_C), F32),
            pltpu.VMEM((rows, D_D), F32),
        ] + ([pltpu.VMEM((SUBLANES - 1, rows + _shift_rows(CONV_C, CONV_C_HIST_PAD), D_C), F32)]
             if seqs % SUBLANES else []),
        compiler_params=_params("parallel", "arbitrary"),
        name="mixer_cd",
    )(src.array, hist_c, hist_d, *map(_param_arg, weights), *io.alias_arg)


def _kv_proj_kernel(mem_ref, wk_ref, wv_ref, k_ref, v_ref, kview_ref, vview_ref):
    m = mem_ref[...].astype(BF16)
    halves = MEM_HEAD_DIM // LANES
    for w_ref, o_ref, view_ref in ((wk_ref, k_ref, kview_ref), (wv_ref, v_ref, vview_ref)):
        y = jnp.dot(m, w_ref[0], preferred_element_type=F32)
        o_ref[0] = y.astype(BF16)
        for hh in range(MEM_HEADS):
            for lt in range(halves):
                col = hh * MEM_HEAD_DIM + lt * LANES
                view_ref[0, 0, pl.ds(lt * MEM_HEADS + hh, N_MEM, stride=halves * MEM_HEADS), :] = (
                    y[:, col:col + LANES])


def _kv_proj(mem, wk, wv):
    rows = jax.ShapeDtypeStruct((DEPTH, BATCH * N_MEM, D_MODEL), BF16)
    view = jax.ShapeDtypeStruct((DEPTH, BATCH, KV_ROWS, LANES), F32)
    w_spec = pl.BlockSpec((1, D_MODEL, D_MODEL), lambda l, n: (l, 0, 0))
    return pl.pallas_call(
        _kv_proj_kernel,
        grid=(DEPTH, BATCH),
        in_specs=[pl.BlockSpec((N_MEM, D_MODEL), lambda l, n: (n, 0)), w_spec, w_spec],
        out_specs=[pl.BlockSpec((1, N_MEM, D_MODEL), lambda l, n: (l, n, 0))] * 2
        + [pl.BlockSpec((1, 1, KV_ROWS, LANES), lambda l, n: (l, n, 0, 0))] * 2,
        out_shape=[rows, rows, view, view],
        compiler_params=_params("parallel", "arbitrary"),
        name="kv_proj",
    )(mem, wk, wv)


def _attend(q, k, v):
    scale = MEM_HEAD_DIM ** -0.5
    qb = q.astype(BF16)
    outs = []
    for hh in range(MEM_HEADS):
        lo, hi = hh * MEM_HEAD_DIM, (hh + 1) * MEM_HEAD_DIM
        s = lax.dot_general(qb[:, lo:hi], k[:, lo:hi], (((1,), (1,)), ((), ())),
                            preferred_element_type=F32) * scale
        s = s - jnp.max(s, axis=-1, keepdims=True)
        p = jnp.exp(s)
        p = p / jnp.sum(p, axis=-1, keepdims=True)
        outs.append(jnp.dot(p.astype(BF16), v[:, lo:hi], preferred_element_type=F32))
    return jnp.concatenate(outs, axis=-1)


def _attn_prompt_kernel(x_ref, k_ref, v_ref, wq_ref, wo_ref, wr_ref, br_ref, lng_ref, lnb_ref, o_ref, g_ref):
    x = x_ref[...]
    q = _dot(x, wq_ref[0])
    o = _attend(q, k_ref[0], v_ref[0])
    y = _dot(o, wo_ref[0])
    h2 = _layer_norm(DN_ALPHA * x + y, lng_ref[...], lnb_ref[...])
    _store_wide_rows(o_ref, g_ref, h2, _route(_router_logits(h2, wr_ref[0]) + br_ref[0]))


def _store_wide_rows(o_ref, g_ref, h2, gates):
    rows = h2.shape[0]
    for c in range(WIDE_TILES):
        piece = gates if c == WIDE_TILES - 1 else h2[:, c * LANES:(c + 1) * LANES]
        o_ref[pl.ds(c, rows, stride=WIDE_TILES), :] = piece
    g_ref[...] = gates


ATTN_OUTS = [(D_WIDE, True), (ROUTER_LANES, False)]


def _attn_prompt(src, k, v, layer, wq, wo, wr, br, lng, lnb, *, tq):
    n_t = src.n_rows // (BATCH * tq)
    weights = (wq, wo, wr, br, lng, lnb)
    io = _slab_io(src, tq, ATTN_OUTS, lambda n, i: n * n_t + i, n_inputs=3 + len(weights))
    kv_spec = pl.BlockSpec((1, N_MEM, D_MODEL), lambda n, i: (layer, n, 0))
    return pl.pallas_call(
        io.adapt(_attn_prompt_kernel),
        grid=(BATCH, n_t),
        in_specs=[io.in_spec, kv_spec, kv_spec] + [_param_spec(w) for w in weights] + io.alias_spec,
        out_specs=io.out_specs,
        out_shape=io.out_shapes,
        input_output_aliases=io.aliases,
        compiler_params=_params("parallel", "arbitrary"),
        name="attn_prompt",
    )(src.array, k, v, *map(_param_arg, weights), *io.alias_arg)


def _att_row(half, head, t):
    return (half * MEM_HEADS + head) * DEC_SEQ + t


def _attn_sample_kernel(x_ref, k_ref, v_ref, wq_ref, wo_ref, wr_ref, br_ref, lng_ref, lnb_ref, o_ref, g_ref,
                        q_scr, qp_scr, op_scr, *, seq_blk, n_blk):
    i = pl.program_id(0)
    halves = MEM_HEAD_DIM // LANES
    blocks = [(t, hh, lt) for t in range(DEC_SEQ) for hh in range(MEM_HEADS) for lt in range(halves)]

    @pl.when(i == 0)
    def _():
        q_scr[...] = _dot(x_ref[...], wq_ref[0])
        for t, hh, lt in blocks:
            col = hh * MEM_HEAD_DIM + lt * LANES
            qp_scr[pl.ds(_att_row(lt, hh, t), DEC_BATCH, stride=ATT_ROWS), :] = (
                q_scr[t * DEC_BATCH:(t + 1) * DEC_BATCH, col:col + LANES])

    half_rows = ATT_ROWS // halves
    col = lax.broadcasted_iota(jnp.int32, (half_rows, KV_ROWS), 1)
    row_head = lax.shift_right_logical(lax.broadcasted_iota(jnp.int32, (half_rows, KV_ROWS), 0),
                                       DEC_SEQ.bit_length() - 1)
    col_cls = col & (halves * MEM_HEADS - 1)
    match0 = col_cls == row_head
    match1 = col_cls == row_head + MEM_HEADS
    scale = MEM_HEAD_DIM ** -0.5

    def body(j, carry):
        n = i * seq_blk + j
        r0 = pl.multiple_of(n * ATT_ROWS, ATT_ROWS)
        qp = qp_scr[pl.ds(r0, ATT_ROWS), :].astype(BF16)
        s = lax.dot_general(qp, k_ref[0, j].astype(BF16), (((1,), (1,)), ((), ())),
                            preferred_element_type=F32)
        part = jnp.where(match0, s[:half_rows], 0.0) + pltpu.roll(
            jnp.where(match1, s[half_rows:], 0.0), KV_ROWS - MEM_HEADS, axis=1)
        sv = jnp.where(match0, part * scale, ROUTER_MASKED)
        e = jnp.exp(sv - jnp.max(sv, axis=-1, keepdims=True))
        p = e / jnp.sum(e, axis=-1, keepdims=True)
        pp = jnp.concatenate([p, pltpu.roll(p, MEM_HEADS, axis=1)], axis=0).astype(BF16)
        op_scr[pl.ds(r0, ATT_ROWS), :] = jnp.dot(pp, v_ref[0, j].astype(BF16), preferred_element_type=F32)
        return carry

    lax.fori_loop(0, seq_blk, body, 0, unroll=True)

    @pl.when(i == n_blk - 1)
    def _():
        for t, hh, lt in blocks:
            col = hh * MEM_HEAD_DIM + lt * LANES
            q_scr[t * DEC_BATCH:(t + 1) * DEC_BATCH, col:col + LANES] = (
                op_scr[pl.ds(_att_row(lt, hh, t), DEC_BATCH, stride=ATT_ROWS), :])
        x = x_ref[...]
        y = _dot(q_scr[...], wo_ref[0])
        h2 = _layer_norm(DN_ALPHA * x + y, lng_ref[...], lnb_ref[...])
        _store_wide_rows(o_ref, g_ref, h2, _route(_router_logits(h2, wr_ref[0]) + br_ref[0]))


def _attn_sample(src, k, v, layer, wq, wo, wr, br, lng, lnb, *, seq_blk):
    m = src.n_rows
    n_blk = DEC_BATCH // seq_blk
    kern = functools.partial(_attn_sample_kernel, seq_blk=seq_blk, n_blk=n_blk)
    weights = (wq, wo, wr, br, lng, lnb)
    io = _slab_io(src, m, ATTN_OUTS, lambda i: 0, n_inputs=3 + len(weights))
    kv_spec = pl.BlockSpec((1, seq_blk, KV_ROWS, LANES), lambda i: (layer, i, 0, 0))
    return pl.pallas_call(
        io.adapt(kern),
        grid=(n_blk,),
        in_specs=[io.in_spec, kv_spec, kv_spec] + [_param_spec(w) for w in weights] + io.alias_spec,
        out_specs=io.out_specs,
        out_shape=io.out_shapes,
        input_output_aliases=io.aliases,
        scratch_shapes=[pltpu.VMEM((m, D_MODEL), F32),
                        pltpu.VMEM((DEC_BATCH * ATT_ROWS, LANES), F32),
                        pltpu.VMEM((DEC_BATCH * ATT_ROWS, LANES), F32)],
        compiler_params=_params("arbitrary"),
        name="attn_sample",
    )(src.array, k, v, *map(_param_arg, weights), *io.alias_arg)


def _split_bf16(x):
    hi = x.astype(BF16)
    return hi, (x - hi.astype(F32)).astype(BF16)


def _router_logits(h, w):
    h_hi, h_lo = _split_bf16(h)
    w_hi, w_lo = _split_bf16(w)
    a = jnp.dot(h_hi, jnp.concatenate([w_hi, w_lo], axis=-1), preferred_element_type=F32)
    b = jnp.dot(h_lo, w_hi, preferred_element_type=F32)
    return a[:, :ROUTER_LANES] + a[:, ROUTER_LANES:] + b


def _route(logits):
    lane = lax.broadcasted_iota(jnp.int32, logits.shape, 1)
    lane_f = lane.astype(F32)
    neg = ROUTER_MASKED
    is_grp = lane < N_GROUPS
    gl = jnp.where(is_grp, logits, neg)
    gmax = jnp.max(gl, axis=-1, keepdims=True)
    gsel = jnp.min(jnp.where(gl == gmax, lane_f, float(ROUTER_LANES)), axis=-1, keepdims=True)
    g_w = 1.0 / jnp.sum(jnp.exp(gl - gmax), axis=-1, keepdims=True)
    e_idx = lane - N_GROUPS
    e_grp = lax.shift_right_arithmetic(e_idx, 2).astype(F32)
    in_grp = (e_idx >= 0) & (e_idx < N_EXPERTS) & (e_grp == gsel)
    el = jnp.where(in_grp, logits, neg)
    m1 = jnp.max(el, axis=-1, keepdims=True)
    i1 = jnp.min(jnp.where(el == m1, lane_f, float(ROUTER_LANES)), axis=-1, keepdims=True)
    el2 = jnp.where(lane_f == i1, neg, el)
    m2 = jnp.max(el2, axis=-1, keepdims=True)
    i2 = jnp.min(jnp.where(el2 == m2, lane_f, float(ROUTER_LANES)), axis=-1, keepdims=True)
    e2 = jnp.exp(m2 - m1)
    den = 1.0 + e2
    w1 = (1.0 / den) * g_w
    w2 = (e2 / den) * g_w
    first_lane = N_GROUPS + EXPERTS_PER_GROUP * gsel
    a = jnp.minimum(i1, i2) - first_lane
    b = jnp.maximum(i1, i2) - first_lane
    cls = gsel * len(PAIRS)
    for j, (pa, pb) in enumerate(PAIRS):
        cls = cls + jnp.where((a == pa) & (b == pb), float(j), 0.0)
    return (jnp.where(lane_f == i1, w1, 0.0) + jnp.where(lane_f == i2, w2, 0.0)
            + jnp.where(lane == CLASS_LANE, cls, 0.0))


def _gate_column(gates, expert):
    lane = lax.broadcasted_iota(jnp.int32, gates.shape, 1)
    return jnp.sum(jnp.where(lane == expert + N_GROUPS, gates, 0.0), axis=-1, keepdims=True)


def _expert_hidden(xb, wg, wu, gate):
    hg = jnp.dot(xb, wg, preferred_element_type=F32)
    hu = jnp.dot(xb, wu, preferred_element_type=F32)
    return (jax.nn.silu(hg) * hu * gate).astype(BF16)


FLAG_ACTIVE, FLAG_FIRST, FLAG_LAST, FLAG_NEW_A, FLAG_NEW_B = 1, 2, 4, 8, 16


def _moe_sparse_kernel(src_ref, tile_ref, ea_ref, eb_ref, lo_ref, hi_ref, flag_ref,
                       x_hbm, wga_ref, wua_ref, wda_ref, wgb_ref, wub_ref, wdb_ref, lng_ref, lnb_ref,
                       out_hbm, xbuf, obuf, acc, wg_s, wu_s, wd_s, gsem, ssem, *, tm, n_tiles, n_items):
    i = pl.program_id(0)
    t = tile_ref[i]
    slot = t % 2
    flags = flag_ref[i]

    def start_gather(tile, s):
        for r in range(tm):
            tok = src_ref[tile * tm + r]
            pltpu.make_async_copy(x_hbm.at[pl.ds(tok * WIDE_TILES, WIDE_TILES)],
                                  xbuf.at[s, pl.ds(r * WIDE_TILES, WIDE_TILES)], gsem.at[s]).start()

    def piece(c):
        return xbuf[slot, pl.ds(c, tm, stride=WIDE_TILES), :]

    def activations():
        return jnp.concatenate([piece(c) for c in range(D_MODEL // LANES)], axis=-1)

    def start_scatter(tile, s):
        for r in range(tm):
            tok = src_ref[tile * tm + r]
            pltpu.make_async_copy(obuf.at[s, pl.ds(r, 1)], out_hbm.at[pl.ds(tok, 1)],
                                  ssem.at[s]).start(priority=r % 2)

    def for_slot(value, fn):
        for s in (0, 1):
            pl.when(value == s)(functools.partial(fn, s))

    def wait_gather(s):
        pltpu.make_async_copy(x_hbm.at[pl.ds(0, tm * WIDE_TILES)], xbuf.at[s], gsem.at[s]).wait()

    def wait_scatter(s):
        pltpu.make_async_copy(obuf.at[s], out_hbm.at[pl.ds(0, tm)], ssem.at[s]).wait()

    @pl.when(i == 0)
    def _():
        start_gather(0, 0)

    @pl.when((flags & FLAG_FIRST) != 0)
    def _():
        @pl.when(t + 1 < n_tiles)
        def _():
            for_slot(1 - slot, lambda s: start_gather(t + 1, s))
        wait_gather(slot)
        acc[...] = jnp.zeros_like(acc)

    @pl.when((flags & FLAG_NEW_A) != 0)
    def _():
        wg_s[0] = wga_ref[0, 0].astype(BF16)
        wu_s[0] = wua_ref[0, 0].astype(BF16)
        wd_s[0] = wda_ref[0, 0].astype(BF16)

    @pl.when((flags & FLAG_NEW_B) != 0)
    def _():
        wg_s[1] = wgb_ref[0, 0].astype(BF16)
        wu_s[1] = wub_ref[0, 0].astype(BF16)
        wd_s[1] = wdb_ref[0, 0].astype(BF16)

    @pl.when((flags & FLAG_ACTIVE) != 0)
    def _():
        xb = activations().astype(BF16)
        gates = piece(WIDE_TILES - 1)
        row = t * tm + lax.broadcasted_iota(jnp.int32, (tm, 1), 0)
        in_class = (row >= lo_ref[i]) & (row < hi_ref[i])
        gate_a = jnp.where(in_class, _gate_column(gates, ea_ref[i]), 0.0)
        gate_b = jnp.where(in_class, _gate_column(gates, eb_ref[i]), 0.0)
        hid_a = _expert_hidden(xb, wg_s[0], wu_s[0], gate_a)
        hid_b = _expert_hidden(xb, wg_s[1], wu_s[1], gate_b)
        acc[...] += (jnp.dot(hid_a, wd_s[0], preferred_element_type=F32)
                     + jnp.dot(hid_b, wd_s[1], preferred_element_type=F32))

    @pl.when((flags & FLAG_LAST) != 0)
    def _():
        @pl.when(t >= 2)
        def _():
            wait_scatter(slot)
        obuf[slot] = _layer_norm(DN_ALPHA * activations() + acc[...], lng_ref[...], lnb_ref[...])
        for_slot(slot, lambda s: start_scatter(t, s))

    @pl.when(i == n_items - 1)
    def _():
        wait_scatter(0)
        wait_scatter(1)


def _moe_schedule(cls, *, tm):
    m = cls.shape[0]
    n_tiles = m // tm
    n_items = n_tiles + N_CLASSES - 1
    i32 = jnp.int32
    src = jnp.argsort(cls, stable=True).astype(i32)
    counts = jnp.sum((cls[:, None] == jnp.arange(N_CLASSES, dtype=i32)[None, :]).astype(i32), axis=0)
    ends = jnp.cumsum(counts)
    starts = ends - counts
    def count(mask):
        return jnp.sum(mask.astype(i32), axis=-1)

    def pick(onehot, values):
        return jnp.sum(jnp.where(onehot, values[None, :], 0), axis=-1)

    tile_lo = jnp.arange(n_tiles, dtype=i32) * tm
    first_c = count(ends[None, :] <= tile_lo[:, None])
    last_c = count(starts[None, :] < (tile_lo + tm)[:, None]) - 1
    per_tile = last_c - first_c + 1
    item_end = jnp.cumsum(per_tile)
    idx = jnp.arange(n_items, dtype=i32)
    tile = jnp.minimum(count(item_end[None, :] <= idx[:, None]), n_tiles - 1)
    in_tile = tile[:, None] == jnp.arange(n_tiles, dtype=i32)[None, :]
    end = pick(in_tile, item_end)
    begin = end - pick(in_tile, per_tile)
    active = idx < item_end[-1]
    c = jnp.where(active, pick(in_tile, first_c) + idx - begin, last_c[n_tiles - 1])
    in_class = c[:, None] == jnp.arange(N_CLASSES, dtype=i32)[None, :]
    class_a = [g * EXPERTS_PER_GROUP + pa for g in range(N_GROUPS) for pa, _ in PAIRS]
    class_b = [g * EXPERTS_PER_GROUP + pb for g in range(N_GROUPS) for _, pb in PAIRS]
    ea = pick(in_class, jnp.asarray(class_a, dtype=i32))
    eb = pick(in_class, jnp.asarray(class_b, dtype=i32))
    lo = jnp.where(active, pick(in_class, starts), 0)
    hi = jnp.where(active, pick(in_class, ends), 0)
    new_a = jnp.concatenate([jnp.ones((1,), bool), ea[1:] != ea[:-1]])
    new_b = jnp.concatenate([jnp.ones((1,), bool), eb[1:] != eb[:-1]])
    flags = (active * FLAG_ACTIVE + (active & (idx == begin)) * FLAG_FIRST
             + (active & (idx == end - 1)) * FLAG_LAST + new_a * FLAG_NEW_A + new_b * FLAG_NEW_B)
    return src, tile, ea, eb, lo.astype(i32), hi.astype(i32), flags.astype(i32)


def _moe_sparse(x, gates, layer, wg, wu, wd, lng, lnb, *, tm):
    m = gates.shape[0]
    n_tiles = m // tm
    n_items = n_tiles + N_CLASSES - 1
    cls = gates[:, CLASS_LANE].astype(jnp.int32)
    tables = _moe_schedule(cls, tm=tm)
    kern = functools.partial(_moe_sparse_kernel, tm=tm, n_tiles=n_tiles, n_items=n_items)

    def w_spec(shape, which):
        return pl.BlockSpec((1, 1) + shape, lambda i, src, tile, ea, eb, lo, hi, fl: (layer, (ea, eb)[which][i], 0, 0))

    up, down = (D_MODEL, D_EXPERT), (D_EXPERT, D_MODEL)
    grid_spec = pltpu.PrefetchScalarGridSpec(
        num_scalar_prefetch=len(tables),
        grid=(n_items,),
        in_specs=[
            pl.BlockSpec(memory_space=pl.ANY),
            w_spec(up, 0), w_spec(up, 0), w_spec(down, 0), w_spec(up, 1), w_spec(up, 1), w_spec(down, 1),
            _const_spec(lng.shape), _const_spec(lnb.shape),
        ],
        out_specs=pl.BlockSpec(memory_space=pl.ANY),
        scratch_shapes=[
            pltpu.VMEM((2, tm * WIDE_TILES, LANES), F32), pltpu.VMEM((2, tm, D_MODEL), F32),
            pltpu.VMEM((tm, D_MODEL), F32),
            pltpu.VMEM((2,) + up, BF16), pltpu.VMEM((2,) + up, BF16), pltpu.VMEM((2,) + down, BF16),
            pltpu.SemaphoreType.DMA((2,)), pltpu.SemaphoreType.DMA((2,)),
        ],
    )
    return pl.pallas_call(
        kern,
        grid_spec=grid_spec,
        out_shape=jax.ShapeDtypeStruct((m, D_MODEL), F32),
        compiler_params=_params("arbitrary"),
        name="moe_sparse",
    )(*tables, x, wg, wu, wd, wg, wu, wd, lng, lnb)


def _row(v):
    return v.reshape(1, -1)


def _to_time_major(s):
    return jnp.transpose(s, (1, 0, 2)).reshape(1, -1, s.shape[-1])


def _from_time_major(s, steps):
    return jnp.transpose(s.reshape(steps, DEC_BATCH, s.shape[-1]), (1, 0, 2))


def _kv_lane_view(cache):
    d, n = cache.shape[:2]
    halves = MEM_HEAD_DIM // LANES
    v = cache.reshape(d, n, N_MEM, MEM_HEADS, halves, LANES)
    return jnp.transpose(v, (0, 1, 2, 4, 3, 5)).reshape(d, n, KV_ROWS, LANES)


def _kv_from_lane_view(view):
    d, n = view.shape[:2]
    halves = MEM_HEAD_DIM // LANES
    v = view.reshape(d, n, N_MEM, halves, MEM_HEADS, LANES)
    return jnp.transpose(v, (0, 1, 2, 4, 3, 5)).reshape(d, n, N_MEM, MEM_HEADS, MEM_HEAD_DIM)


def kernel(x_prompt, x_sample, cache_mem_k, cache_mem_v, state_pool, state_conv_c, state_conv_d, mem_prompt,
           w_in_ab, ln_v_g, ln_v_b, w_spatial, b_spatial, w_pool, pool_scale, w_out_ab,
           w_in_cd, conv_c_w, conv_c_b, ln_c_g, ln_c_b, conv_d_w, w_out_cd,
           w_q, w_k, w_v, w_o, w_group, b_group, w_router, b_router, w_gate, w_up, w_down, ln_g, ln_b):
    mem_flat = mem_prompt.reshape(BATCH * N_MEM, D_MODEL)
    cache_k = _kv_lane_view(cache_mem_k)
    cache_v = _kv_lane_view(cache_mem_v)
    kp, vp, kp_view, vp_view = _kv_proj(mem_flat, w_k.astype(BF16), w_v.astype(BF16))

    w_in_ab, w_out_ab, w_pool = w_in_ab.astype(BF16), w_out_ab.astype(BF16), w_pool.astype(BF16)
    w_in_cd, w_out_cd = w_in_cd.astype(BF16), w_out_cd.astype(BF16)
    w_q, w_o = w_q.astype(BF16), w_o.astype(BF16)
    pad = ROUTER_LANES - N_GROUPS - N_EXPERTS
    w_route = jnp.pad(jnp.concatenate([w_group, w_router], axis=-1), ((0, 0), (0, 0), (0, pad)))
    b_route = jnp.pad(jnp.concatenate([b_group, b_router], axis=-1), ((0, 0), (0, pad)))[:, None, :]

    prompt = _Slab(x_prompt.reshape(PROMPT_ROWS, D_MODEL), 0, PROMPT_ROWS, 0, None)
    sample = _Slab(_to_time_major(x_sample)[0], 0, SAMPLE_ROWS, PROMPT_ROWS, None)

    def both(h):
        return (_Slab(h, 0, PROMPT_ROWS, 0, None), _Slab(h, PROMPT_ROWS, SAMPLE_ROWS, PROMPT_ROWS, None))

    pool_p, pool_s, chunk_v_s = [], [], []
    conv_c_p, conv_c_s, conv_d_p, conv_d_s = [], [], [], []

    for l in range(DEPTH):
        i = l // 2
        lng0, lnb0 = _row(ln_g[l, 0]), _row(ln_b[l, 0])
        if l % 2 == 0:
            bias = jnp.repeat(b_spatial[i].T, A_HEAD_DIM, axis=1)
            small = jnp.repeat(
                jnp.transpose(w_spatial[i][:, :DEC_SEQ, :DEC_SEQ], (1, 2, 0)).reshape(DEC_SEQ * DEC_SEQ, A_HEADS),
                A_HEAD_DIM, axis=1)
            common = (_Layer(w_in_ab, i), _row(ln_v_g[i]), _row(ln_v_b[i]))
            tail = (_Layer(w_pool, i), _row(pool_scale[i]), _Layer(w_out_ab, i), lng0, lnb0)
            h, hist_p = _mixer_ab(prompt, jnp.zeros((BATCH, POOL_HIST, D_B), F32), *common, w_spatial[i], bias,
                                  *tail, seqs=1, tt=512, pos0=0, with_v=False)
            h, v_rows, hist_s = _mixer_ab(sample._replace(dst=(h,)), _to_time_major(state_pool[i]), *common, small,
                                          bias[:DEC_SEQ], *tail, seqs=DEC_BATCH, tt=DEC_SEQ, pos0=PAST_LEN,
                                          with_v=True)
            pool_p.append(hist_p)
            pool_s.append(_from_time_major(hist_s, POOL_HIST))
            chunk_v_s.append(_from_time_major(v_rows, DEC_SEQ))
        else:
            wts = (_Layer(w_in_cd, i), conv_c_w[i], _row(conv_c_b[i]), _row(ln_c_g[i]), _row(ln_c_b[i]),
                   conv_d_w[i], _Layer(w_out_cd, i), lng0, lnb0)
            h, hc_p, hd_p = _mixer_cd(prompt, jnp.zeros((BATCH, CONV_C - 1, D_C), F32),
                                      jnp.zeros((BATCH, CONV_D - 1, D_D), F32), *wts, seqs=1, tt=512)
            h, hc_s, hd_s = _mixer_cd(sample._replace(dst=(h,)), _to_time_major(state_conv_c[i]),
                                      _to_time_major(state_conv_d[i]), *wts, seqs=DEC_BATCH, tt=DEC_SEQ)
            conv_c_p.append(hc_p)
            conv_d_p.append(hd_p)
            conv_c_s.append(_from_time_major(hc_s, CONV_C - 1))
            conv_d_s.append(_from_time_major(hd_s, CONV_D - 1))
        prompt, sample = both(h)

        att_w = (_Layer(w_q, l), _Layer(w_o, l), _Layer(w_route, l), _Layer(b_route, l),
                 _row(ln_g[l, 1]), _row(ln_b[l, 1]))
        wide_and_gates = _attn_prompt(prompt, kp, vp, l, *att_w, tq=512)
        h_wide, gates = _attn_sample(sample._replace(dst=tuple(wide_and_gates)), cache_k, cache_v, l, *att_w,
                                     seq_blk=4)

        h = _moe_sparse(h_wide, gates, l, w_gate, w_up, w_down, _row(ln_g[l, 2]), _row(ln_b[l, 2]), tm=256)
        prompt, sample = both(h)

    y_prompt = h[:PROMPT_ROWS].reshape(BATCH, SEQ, D_MODEL)
    y_sample = _from_time_major(h[PROMPT_ROWS:], DEC_SEQ)
    return (y_prompt, y_sample, _kv_from_lane_view(kp_view), _kv_from_lane_view(vp_view), jnp.stack(pool_p),
            jnp.stack(conv_c_p), jnp.stack(conv_d_p), jnp.stack(chunk_v_s), jnp.stack(pool_s),
            jnp.stack(conv_c_s), jnp.stack(conv_d_s))
```

```python
import functools
from typing import Callable, NamedTuple, Optional

import jax
import jax.numpy as jnp
from jax import lax
from jax.experimental import pallas as pl
from jax.experimental.pallas import tpu as pltpu

D_MODEL = 1024
BATCH = 8
SEQ = 2048
DEPTH = 4
DEC_BATCH = 128
DEC_SEQ = 4
PAST_LEN = 16384

CHUNK = 128
A_HEADS = 4
D_A = D_MODEL // 2
A_HEAD_DIM = D_A // A_HEADS
POOL_WINDOWS = (2, 4, 8, 16)
B_GROUPS = len(POOL_WINDOWS)
D_B = D_MODEL // 2
B_GROUP_DIM = D_B // B_GROUPS
POOL_HIST = max(POOL_WINDOWS) - 1
D_C = D_MODEL // 2
CONV_C = 31
D_D = D_MODEL // 2
CONV_D = 3
N_MEM = 256
MEM_HEADS = 4
MEM_HEAD_DIM = D_MODEL // MEM_HEADS
N_GROUPS = 4
EXPERTS_PER_GROUP = 4
N_EXPERTS = N_GROUPS * EXPERTS_PER_GROUP
D_EXPERT = 512
DN_ALPHA = (2 * DEPTH) ** 0.25
LN_EPS = 1e-5
PROMPT_ROWS = BATCH * SEQ
SAMPLE_ROWS = DEC_BATCH * DEC_SEQ
TOTAL_ROWS = PROMPT_ROWS + SAMPLE_ROWS

LANES = 128
SUBLANES = 8
POOL_HIST_PAD = 16
CONV_C_HIST_PAD = 32
CONV_D_HIST_PAD = 8
ROUTER_LANES = 128
ROUTER_MASKED = -1e30
CLASS_LANE = N_GROUPS + N_EXPERTS
PAIRS = ((0, 1), (0, 2), (1, 2), (1, 3), (0, 3), (2, 3))
N_CLASSES = N_GROUPS * len(PAIRS)
D_WIDE = D_MODEL + ROUTER_LANES
WIDE_TILES = D_WIDE // LANES
KV_ROWS = N_MEM * MEM_HEADS * MEM_HEAD_DIM // LANES
ATT_ROWS = KV_ROWS // N_MEM * DEC_SEQ
VMEM_LIMIT_BYTES = 52 * 1024 * 1024

F32 = jnp.float32
BF16 = jnp.bfloat16


def _layer_norm(x, g, b):
    mu = jnp.mean(x, axis=-1, keepdims=True)
    xc = x - mu
    var = jnp.mean(xc * xc, axis=-1, keepdims=True)
    return xc * lax.rsqrt(var + LN_EPS) * g + b


def _dot(a, b):
    return jnp.dot(a.astype(BF16), b.astype(BF16), preferred_element_type=F32)


def _params(*semantics):
    return pltpu.CompilerParams(dimension_semantics=semantics, vmem_limit_bytes=VMEM_LIMIT_BYTES)


def _const_spec(shape):
    nd = len(shape)
    return pl.BlockSpec(shape, lambda *_: (0,) * nd)


class _Layer(NamedTuple):
    stack: jax.Array
    layer: int


def _param_spec(p):
    if isinstance(p, _Layer):
        nd = p.stack.ndim
        return pl.BlockSpec((1,) + p.stack.shape[1:], lambda *_: (p.layer,) + (0,) * (nd - 1))
    return _const_spec(p.shape)


def _param_arg(p):
    return p.stack if isinstance(p, _Layer) else p


class _Slab(NamedTuple):
    array: jax.Array
    row0: int
    n_rows: int
    out_row0: int
    dst: Optional[tuple]


class _SlabIO(NamedTuple):
    in_spec: pl.BlockSpec
    out_specs: list
    out_shapes: list
    alias_spec: list
    alias_arg: tuple
    aliases: dict
    adapt: Callable


def _slab_io(src, rows, outs, block_index, *, n_inputs):
    assert src.row0 % rows == 0 and src.out_row0 % rows == 0 and src.n_rows % rows == 0
    in0, out0 = src.row0 // rows, src.out_row0 // rows
    in_spec = pl.BlockSpec((rows, src.array.shape[1]), lambda *g: (in0 + block_index(*g), 0))
    out_specs, out_shapes = [], []
    for width, row_major_tiles in outs:
        pieces, lanes = (width // LANES, LANES) if row_major_tiles else (1, width)
        out_specs.append(pl.BlockSpec((rows * pieces, lanes), lambda *g: (out0 + block_index(*g), 0)))
        out_shapes.append(jax.ShapeDtypeStruct((TOTAL_ROWS * pieces, lanes), F32))
    if src.dst is None:
        return _SlabIO(in_spec, out_specs, out_shapes, [], (), {}, lambda kernel: kernel)
    n_dst = len(outs)
    assert len(src.dst) == n_dst

    def adapt(kernel):
        return lambda *refs: kernel(*refs[:n_inputs], *refs[n_inputs + n_dst:])

    return _SlabIO(in_spec, out_specs, out_shapes, [pl.BlockSpec(memory_space=pl.ANY)] * n_dst, tuple(src.dst),
                   {n_inputs + j: j for j in range(n_dst)}, adapt)


def _mixer_ab_kernel(x_ref, hist_ref, w_in_ref, lnv_g_ref, lnv_b_ref, ws_ref, bs_ref, wpool_ref,
                     pscale_ref, w_out_ref, lng_ref, lnb_ref, *refs, seqs, tt, pos0, with_v):
    if with_v:
        h_out_ref, v_out_ref, hist_out_ref, zbuf = refs
    else:
        h_out_ref, hist_out_ref, zbuf = refs
    t_idx = pl.program_id(1)
    rows = tt * seqs
    hp = POOL_HIST_PAD

    @pl.when(t_idx == 0)
    def _():
        zbuf[pl.ds(0, (hp - POOL_HIST) * seqs), :] = jnp.zeros(((hp - POOL_HIST) * seqs, D_B), F32)
        zbuf[pl.ds((hp - POOL_HIST) * seqs, POOL_HIST * seqs), :] = hist_ref[0]

    x = x_ref[...]
    h = _dot(x, w_in_ref[0])
    ua = jax.nn.gelu(h[:, :2 * D_A])
    u = ua[:, :D_A]
    v = _layer_norm(ua[:, D_A:], lnv_g_ref[...], lnv_b_ref[...])
    z = h[:, 2 * D_A:]
    if with_v:
        v_out_ref[0] = v
    zbuf[pl.ds(hp * seqs, rows), :] = z

    if seqs == 1:
        tri = (lax.broadcasted_iota(jnp.int32, (CHUNK, CHUNK), 0)
               >= lax.broadcasted_iota(jnp.int32, (CHUNK, CHUNK), 1))
        w_heads = [jnp.where(tri, ws_ref[hh], 0.0).astype(BF16) for hh in range(A_HEADS)]
        vb = v.astype(BF16)
        chunks = []
        for c in range(tt // CHUNK):
            heads = []
            for hh in range(A_HEADS):
                vc = vb[c * CHUNK:(c + 1) * CHUNK, hh * A_HEAD_DIM:(hh + 1) * A_HEAD_DIM]
                heads.append(jnp.dot(w_heads[hh], vc, preferred_element_type=F32))
            chunks.append(jnp.concatenate(heads, axis=-1) + bs_ref[...])
        mixed = jnp.concatenate(chunks, axis=0)
    else:
        parts = []
        for t in range(tt):
            acc = bs_ref[t:t + 1, :]
            for s in range(t + 1):
                acc = acc + ws_ref[t * tt + s:t * tt + s + 1, :] * v[s * seqs:(s + 1) * seqs, :]
            parts.append(acc)
        mixed = jnp.concatenate(parts, axis=0)
    a_out = u * mixed

    run = zbuf[...]
    width = 1
    outs = []
    for gi, w in enumerate(POOL_WINDOWS):
        lo, hi = gi * B_GROUP_DIM, (gi + 1) * B_GROUP_DIM
        while width < w:
            run = run + pltpu.roll(run, width * seqs, axis=0)
            width *= 2
        assert width == w
        acc = run[hp * seqs:, :B_GROUP_DIM]
        run = run[:, B_GROUP_DIM:]
        if pos0 + 1 >= w:
            cnt = float(w)
        else:
            assert seqs == 1
            pos = pos0 + t_idx * tt + lax.broadcasted_iota(jnp.int32, (rows, 1), 0)
            cnt = jnp.minimum(pos + 1, w).astype(F32)
        pooled = acc / cnt - z[:, lo:hi]
        outs.append(_dot(pooled, wpool_ref[0, gi]))
    b_out = jnp.concatenate(outs, axis=-1) * pscale_ref[...]

    y = _dot(jnp.concatenate([a_out, b_out], axis=-1), w_out_ref[0])
    h_out_ref[...] = _layer_norm(DN_ALPHA * x + y, lng_ref[...], lnb_ref[...])

    hist_out_ref[0] = zbuf[pl.ds((tt + hp - POOL_HIST) * seqs, POOL_HIST * seqs), :]
    zbuf[pl.ds(0, hp * seqs), :] = zbuf[pl.ds(tt * seqs, hp * seqs), :]


def _mixer_ab(src, hist, w_in, lnv_g, lnv_b, ws, bs, wpool, pscale, w_out, lng, lnb, *, seqs, tt, pos0, with_v):
    nb = hist.shape[0]
    rows = tt * seqs
    n_t = src.n_rows // (nb * rows)
    kern = functools.partial(_mixer_ab_kernel, seqs=seqs, tt=tt, pos0=pos0, with_v=with_v)
    weights = (w_in, lnv_g, lnv_b, ws, bs, wpool, pscale, w_out, lng, lnb)
    io = _slab_io(src, rows, [(D_MODEL, False)], lambda n, t: n * n_t + t, n_inputs=2 + len(weights))
    v_spec = pl.BlockSpec((1, rows, D_A), lambda n, t: (n, t, 0))
    hist_spec = pl.BlockSpec((1, POOL_HIST * seqs, D_B), lambda n, t: (n, 0, 0))
    v_shape = jax.ShapeDtypeStruct((nb, n_t * rows, D_A), F32)
    hist_shape = jax.ShapeDtypeStruct((nb, POOL_HIST * seqs, D_B), F32)
    return pl.pallas_call(
        io.adapt(kern),
        grid=(nb, n_t),
        in_specs=[io.in_spec, hist_spec] + [_param_spec(w) for w in weights] + io.alias_spec,
        out_specs=io.out_specs + ([v_spec, hist_spec] if with_v else [hist_spec]),
        out_shape=io.out_shapes + ([v_shape, hist_shape] if with_v else [hist_shape]),
        input_output_aliases=io.aliases,
        scratch_shapes=[pltpu.VMEM(((POOL_HIST_PAD + tt) * seqs, D_B), F32)],
        compiler_params=_params("parallel", "arbitrary"),
        name="mixer_ab",
    )(src.array, hist, *map(_param_arg, weights), *io.alias_arg)


def _shift_rows(taps, hist_pad):
    base = hist_pad - (taps - 1)
    return max((base + k) // SUBLANES * SUBLANES for k in range(taps) if (base + k) % SUBLANES)


def _dwconv(buf, w_ref, out, shifted, *, taps, hist_pad, seqs, rows, chunk):
    base = (hist_pad - (taps - 1)) * seqs
    if shifted is not None:
        span = rows + _shift_rows(taps, hist_pad)
        for b in range(1, SUBLANES):
            shifted[b - 1] = buf[pl.ds(b, span), :]

    def tap(k, r0, n, lanes):
        off = base + k * seqs
        phase = off % SUBLANES
        if shifted is None or phase == 0:
            return buf[pl.ds(off + r0, n), lanes]
        return shifted[phase - 1, pl.ds(off - phase + r0, n), lanes]

    if shifted is None:
        def full_width(i, carry):
            r0 = i * chunk if isinstance(i, int) else pl.multiple_of(i * chunk, chunk)
            acc = w_ref[0:1, :] * tap(0, r0, chunk, slice(None))
            for k in range(1, taps):
                acc = acc + w_ref[k:k + 1, :] * tap(k, r0, chunk, slice(None))
            out[pl.ds(r0, chunk), :] = acc
            return carry

        if seqs % SUBLANES == 0:
            lax.fori_loop(0, rows // chunk, full_width, 0)
        else:
            for i in range(rows // chunk):
                full_width(i, 0)
        return

    for lt in range(buf.shape[1] // LANES):
        lanes = slice(lt * LANES, (lt + 1) * LANES)
        wk = [jnp.broadcast_to(w_ref[k:k + 1, lanes], (SUBLANES, LANES)) for k in range(taps)]
        for rg in range(0, rows, SUBLANES):
            acc = wk[0] * tap(0, rg, SUBLANES, lanes)
            for k in range(1, taps):
                acc = acc + wk[k] * tap(k, rg, SUBLANES, lanes)
            out[pl.ds(rg, SUBLANES), lanes] = acc


def _mixer_cd_kernel(x_ref, hist_c_ref, hist_d_ref, w_in_ref, ccw_ref, ccb_ref, lncg_ref, lncb_ref,
                     cdw_ref, w_out_ref, lng_ref, lnb_ref,
                     h_out_ref, hist_c_out_ref, hist_d_out_ref, cbuf, dbuf, cc_scr, cd_scr, *maybe_shifted,
                     seqs, tt):
    shifted = maybe_shifted[0] if maybe_shifted else None
    t_idx = pl.program_id(1)
    rows = tt * seqs
    hc, hd = CONV_C_HIST_PAD, CONV_D_HIST_PAD
    nc, nd = CONV_C - 1, CONV_D - 1

    @pl.when(t_idx == 0)
    def _():
        cbuf[pl.ds((hc - nc) * seqs, nc * seqs), :] = hist_c_ref[0]
        dbuf[pl.ds((hd - nd) * seqs, nd * seqs), :] = hist_d_ref[0]

    x = x_ref[...]
    h = _dot(x, w_in_ref[0])
    glu = h[:, :D_C] * jax.nn.sigmoid(h[:, D_C:2 * D_C])
    o = 2 * D_C
    gate_b = h[:, o:o + D_D]
    gx = h[:, o + D_D:o + 2 * D_D] * h[:, o + 2 * D_D:]
    cbuf[pl.ds(hc * seqs, rows), :] = glu
    dbuf[pl.ds(hd * seqs, rows), :] = gx

    chunk = min(rows, 32)
    _dwconv(cbuf, ccw_ref, cc_scr, shifted, taps=CONV_C, hist_pad=hc, seqs=seqs, rows=rows, chunk=chunk)
    _dwconv(dbuf, cdw_ref, cd_scr, None, taps=CONV_D, hist_pad=hd, seqs=seqs, rows=rows, chunk=chunk)

    c_out = jax.nn.silu(_layer_norm(cc_scr[...] + ccb_ref[...], lncg_ref[...], lncb_ref[...]))
    d_out = gate_b * cd_scr[...]
    y = _dot(jnp.concatenate([c_out, d_out], axis=-1), w_out_ref[0])
    h_out_ref[...] = _layer_norm(DN_ALPHA * x + y, lng_ref[...], lnb_ref[...])

    hist_c_out_ref[0] = cbuf[pl.ds((tt + hc - nc) * seqs, nc * seqs), :]
    hist_d_out_ref[0] = dbuf[pl.ds((tt + hd - nd) * seqs, nd * seqs), :]
    cbuf[pl.ds(0, hc * seqs), :] = cbuf[pl.ds(tt * seqs, hc * seqs), :]
    dbuf[pl.ds(0, hd * seqs), :] = dbuf[pl.ds(tt * seqs, hd * seqs), :]


def _mixer_cd(src, hist_c, hist_d, w_in, ccw, ccb, lncg, lncb, cdw, w_out, lng, lnb, *, seqs, tt):
    nb = hist_c.shape[0]
    rows = tt * seqs
    n_t = src.n_rows // (nb * rows)
    nc, nd = CONV_C - 1, CONV_D - 1
    kern = functools.partial(_mixer_cd_kernel, seqs=seqs, tt=tt)
    weights = (w_in, ccw, ccb, lncg, lncb, cdw, w_out, lng, lnb)
    io = _slab_io(src, rows, [(D_MODEL, False)], lambda n, t: n * n_t + t, n_inputs=3 + len(weights))
    hist_c_spec = pl.BlockSpec((1, nc * seqs, D_C), lambda n, t: (n, 0, 0))
    hist_d_spec = pl.BlockSpec((1, nd * seqs, D_D), lambda n, t: (n, 0, 0))
    return pl.pallas_call(
        io.adapt(kern),
        grid=(nb, n_t),
        in_specs=[io.in_spec, hist_c_spec, hist_d_spec] + [_param_spec(w) for w in weights] + io.alias_spec,
        out_specs=io.out_specs + [hist_c_spec, hist_d_spec],
        out_shape=io.out_shapes + [
            jax.ShapeDtypeStruct((nb, nc * seqs, D_C), F32),
            jax.ShapeDtypeStruct((nb, nd * seqs, D_D), F32),
        ],
        input_output_aliases=io.aliases,
        scratch_shapes=[
            pltpu.VMEM(((CONV_C_HIST_PAD + tt) * seqs, D_C), F32),
            pltpu.VMEM(((CONV_D_HIST_PAD + tt) * seqs, D_D), F32),
            pltpu.VMEM((rows, D_C), F32),
            pltpu.VMEM((rows, D_D), F32),
        ] + ([pltpu.VMEM((SUBLANES - 1, rows + _shift_rows(CONV_C, CONV_C_HIST_PAD), D_C), F32)]
             if seqs % SUBLANES else []),
        compiler_params=_params("parallel", "arbitrary"),
        name="mixer_cd",
    )(src.array, hist_c, hist_d, *map(_param_arg, weights), *io.alias_arg)


def _kv_proj_kernel(mem_ref, wk_ref, wv_ref, k_ref, v_ref, kview_ref, vview_ref):
    m = mem_ref[...].astype(BF16)
    halves = MEM_HEAD_DIM // LANES
    for w_ref, o_ref, view_ref in ((wk_ref, k_ref, kview_ref), (wv_ref, v_ref, vview_ref)):
        y = jnp.dot(m, w_ref[0], preferred_element_type=F32)
        o_ref[0] = y.astype(BF16)
        for hh in range(MEM_HEADS):
            for lt in range(halves):
                col = hh * MEM_HEAD_DIM + lt * LANES
                view_ref[0, 0, pl.ds(lt * MEM_HEADS + hh, N_MEM, stride=halves * MEM_HEADS), :] = (
                    y[:, col:col + LANES])


def _kv_proj(mem, wk, wv):
    rows = jax.ShapeDtypeStruct((DEPTH, BATCH * N_MEM, D_MODEL), BF16)
    view = jax.ShapeDtypeStruct((DEPTH, BATCH, KV_ROWS, LANES), F32)
    w_spec = pl.BlockSpec((1, D_MODEL, D_MODEL), lambda l, n: (l, 0, 0))
    return pl.pallas_call(
        _kv_proj_kernel,
        grid=(DEPTH, BATCH),
        in_specs=[pl.BlockSpec((N_MEM, D_MODEL), lambda l, n: (n, 0)), w_spec, w_spec],
        out_specs=[pl.BlockSpec((1, N_MEM, D_MODEL), lambda l, n: (l, n, 0))] * 2
        + [pl.BlockSpec((1, 1, KV_ROWS, LANES), lambda l, n: (l, n, 0, 0))] * 2,
        out_shape=[rows, rows, view, view],
        compiler_params=_params("parallel", "arbitrary"),
        name="kv_proj",
    )(mem, wk, wv)


def _attend(q, k, v):
    scale = MEM_HEAD_DIM ** -0.5
    qb = q.astype(BF16)
    outs = []
    for hh in range(MEM_HEADS):
        lo, hi = hh * MEM_HEAD_DIM, (hh + 1) * MEM_HEAD_DIM
        s = lax.dot_general(qb[:, lo:hi], k[:, lo:hi], (((1,), (1,)), ((), ())),
                            preferred_element_type=F32) * scale
        s = s - jnp.max(s, axis=-1, keepdims=True)
        p = jnp.exp(s)
        p = p / jnp.sum(p, axis=-1, keepdims=True)
        outs.append(jnp.dot(p.astype(BF16), v[:, lo:hi], preferred_element_type=F32))
    return jnp.concatenate(outs, axis=-1)


def _attn_prompt_kernel(x_ref, k_ref, v_ref, wq_ref, wo_ref, wr_ref, br_ref, lng_ref, lnb_ref, o_ref, g_ref):
    x = x_ref[...]
    q = _dot(x, wq_ref[0])
    o = _attend(q, k_ref[0], v_ref[0])
    y = _dot(o, wo_ref[0])
    h2 = _layer_norm(DN_ALPHA * x + y, lng_ref[...], lnb_ref[...])
    _store_wide_rows(o_ref, g_ref, h2, _route(_router_logits(h2, wr_ref[0]) + br_ref[0]))


def _store_wide_rows(o_ref, g_ref, h2, gates):
    rows = h2.shape[0]
    for c in range(WIDE_TILES):
        piece = gates if c == WIDE_TILES - 1 else h2[:, c * LANES:(c + 1) * LANES]
        o_ref[pl.ds(c, rows, stride=WIDE_TILES), :] = piece
    g_ref[...] = gates


ATTN_OUTS = [(D_WIDE, True), (ROUTER_LANES, False)]


def _attn_prompt(src, k, v, layer, wq, wo, wr, br, lng, lnb, *, tq):
    n_t = src.n_rows // (BATCH * tq)
    weights = (wq, wo, wr, br, lng, lnb)
    io = _slab_io(src, tq, ATTN_OUTS, lambda n, i: n * n_t + i, n_inputs=3 + len(weights))
    kv_spec = pl.BlockSpec((1, N_MEM, D_MODEL), lambda n, i: (layer, n, 0))
    return pl.pallas_call(
        io.adapt(_attn_prompt_kernel),
        grid=(BATCH, n_t),
        in_specs=[io.in_spec, kv_spec, kv_spec] + [_param_spec(w) for w in weights] + io.alias_spec,
        out_specs=io.out_specs,
        out_shape=io.out_shapes,
        input_output_aliases=io.aliases,
        compiler_params=_params("parallel", "arbitrary"),
        name="attn_prompt",
    )(src.array, k, v, *map(_param_arg, weights), *io.alias_arg)


def _att_row(half, head, t):
    return (half * MEM_HEADS + head) * DEC_SEQ + t


def _attn_sample_kernel(x_ref, k_ref, v_ref, wq_ref, wo_ref, wr_ref, br_ref, lng_ref, lnb_ref, o_ref, g_ref,
                        q_scr, qp_scr, op_scr, *, seq_blk, n_blk):
    i = pl.program_id(0)
    halves = MEM_HEAD_DIM // LANES
    blocks = [(t, hh, lt) for t in range(DEC_SEQ) for hh in range(MEM_HEADS) for lt in range(halves)]

    @pl.when(i == 0)
    def _():
        q_scr[...] = _dot(x_ref[...], wq_ref[0])
        for t, hh, lt in blocks:
            col = hh * MEM_HEAD_DIM + lt * LANES
            qp_scr[pl.ds(_att_row(lt, hh, t), DEC_BATCH, stride=ATT_ROWS), :] = (
                q_scr[t * DEC_BATCH:(t + 1) * DEC_BATCH, col:col + LANES])

    half_rows = ATT_ROWS // halves
    col = lax.broadcasted_iota(jnp.int32, (half_rows, KV_ROWS), 1)
    row_head = lax.shift_right_logical(lax.broadcasted_iota(jnp.int32, (half_rows, KV_ROWS), 0),
                                       DEC_SEQ.bit_length() - 1)
    col_cls = col & (halves * MEM_HEADS - 1)
    match0 = col_cls == row_head
    match1 = col_cls == row_head + MEM_HEADS
    scale = MEM_HEAD_DIM ** -0.5

    def body(j, carry):
        n = i * seq_blk + j
        r0 = pl.multiple_of(n * ATT_ROWS, ATT_ROWS)
        qp = qp_scr[pl.ds(r0, ATT_ROWS), :].astype(BF16)
        s = lax.dot_general(qp, k_ref[0, j].astype(BF16), (((1,), (1,)), ((), ())),
                            preferred_element_type=F32)
        part = jnp.where(match0, s[:half_rows], 0.0) + pltpu.roll(
            jnp.where(match1, s[half_rows:], 0.0), KV_ROWS - MEM_HEADS, axis=1)
        sv = jnp.where(match0, part * scale, ROUTER_MASKED)
        e = jnp.exp(sv - jnp.max(sv, axis=-1, keepdims=True))
        p = e / jnp.sum(e, axis=-1, keepdims=True)
        pp = jnp.concatenate([p, pltpu.roll(p, MEM_HEADS, axis=1)], axis=0).astype(BF16)
        op_scr[pl.ds(r0, ATT_ROWS), :] = jnp.dot(pp, v_ref[0, j].astype(BF16), preferred_element_type=F32)
        return carry

    lax.fori_loop(0, seq_blk, body, 0, unroll=True)

    @pl.when(i == n_blk - 1)
    def _():
        for t, hh, lt in blocks:
            col = hh * MEM_HEAD_DIM + lt * LANES
            q_scr[t * DEC_BATCH:(t + 1) * DEC_BATCH, col:col + LANES] = (
                op_scr[pl.ds(_att_row(lt, hh, t), DEC_BATCH, stride=ATT_ROWS), :])
        x = x_ref[...]
        y = _dot(q_scr[...], wo_ref[0])
        h2 = _layer_norm(DN_ALPHA * x + y, lng_ref[...], lnb_ref[...])
        _store_wide_rows(o_ref, g_ref, h2, _route(_router_logits(h2, wr_ref[0]) + br_ref[0]))


def _attn_sample(src, k, v, layer, wq, wo, wr, br, lng, lnb, *, seq_blk):
    m = src.n_rows
    n_blk = DEC_BATCH // seq_blk
    kern = functools.partial(_attn_sample_kernel, seq_blk=seq_blk, n_blk=n_blk)
    weights = (wq, wo, wr, br, lng, lnb)
    io = _slab_io(src, m, ATTN_OUTS, lambda i: 0, n_inputs=3 + len(weights))
    kv_spec = pl.BlockSpec((1, seq_blk, KV_ROWS, LANES), lambda i: (layer, i, 0, 0))
    return pl.pallas_call(
        io.adapt(kern),
        grid=(n_blk,),
        in_specs=[io.in_spec, kv_spec, kv_spec] + [_param_spec(w) for w in weights] + io.alias_spec,
        out_specs=io.out_specs,
        out_shape=io.out_shapes,
        input_output_aliases=io.aliases,
        scratch_shapes=[pltpu.VMEM((m, D_MODEL), F32),
                        pltpu.VMEM((DEC_BATCH * ATT_ROWS, LANES), F32),
                        pltpu.VMEM((DEC_BATCH * ATT_ROWS, LANES), F32)],
        compiler_params=_params("arbitrary"),
        name="attn_sample",
    )(src.array, k, v, *map(_param_arg, weights), *io.alias_arg)


def _split_bf16(x):
    hi = x.astype(BF16)
    return hi, (x - hi.astype(F32)).astype(BF16)


def _router_logits(h, w):
    h_hi, h_lo = _split_bf16(h)
    w_hi, w_lo = _split_bf16(w)
    a = jnp.dot(h_hi, jnp.concatenate([w_hi, w_lo], axis=-1), preferred_element_type=F32)
    b = jnp.dot(h_lo, w_hi, preferred_element_type=F32)
    return a[:, :ROUTER_LANES] + a[:, ROUTER_LANES:] + b


def _route(logits):
    lane = lax.broadcasted_iota(jnp.int32, logits.shape, 1)
    lane_f = lane.astype(F32)
    neg = ROUTER_MASKED
    is_grp = lane < N_GROUPS
    gl = jnp.where(is_grp, logits, neg)
    gmax = jnp.max(gl, axis=-1, keepdims=True)
    gsel = jnp.min(jnp.where(gl == gmax, lane_f, float(ROUTER_LANES)), axis=-1, keepdims=True)
    g_w = 1.0 / jnp.sum(jnp.exp(gl - gmax), axis=-1, keepdims=True)
    e_idx = lane - N_GROUPS
    e_grp = lax.shift_right_arithmetic(e_idx, 2).astype(F32)
    in_grp = (e_idx >= 0) & (e_idx < N_EXPERTS) & (e_grp == gsel)
    el = jnp.where(in_grp, logits, neg)
    m1 = jnp.max(el, axis=-1, keepdims=True)
    i1 = jnp.min(jnp.where(el == m1, lane_f, float(ROUTER_LANES)), axis=-1, keepdims=True)
    el2 = jnp.where(lane_f == i1, neg, el)
    m2 = jnp.max(el2, axis=-1, keepdims=True)
    i2 = jnp.min(jnp.where(el2 == m2, lane_f, float(ROUTER_LANES)), axis=-1, keepdims=True)
    e2 = jnp.exp(m2 - m1)
    den = 1.0 + e2
    w1 = (1.0 / den) * g_w
    w2 = (e2 / den) * g_w
    first_lane = N_GROUPS + EXPERTS_PER_GROUP * gsel
    a = jnp.minimum(i1, i2) - first_lane
    b = jnp.maximum(i1, i2) - first_lane
    cls = gsel * len(PAIRS)
    for j, (pa, pb) in enumerate(PAIRS):
        cls = cls + jnp.where((a == pa) & (b == pb), float(j), 0.0)
    return (jnp.where(lane_f == i1, w1, 0.0) + jnp.where(lane_f == i2, w2, 0.0)
            + jnp.where(lane == CLASS_LANE, cls, 0.0))


def _gate_column(gates, expert):
    lane = lax.broadcasted_iota(jnp.int32, gates.shape, 1)
    return jnp.sum(jnp.where(lane == expert + N_GROUPS, gates, 0.0), axis=-1, keepdims=True)


def _expert_hidden(xb, wg, wu, gate):
    hg = jnp.dot(xb, wg, preferred_element_type=F32)
    hu = jnp.dot(xb, wu, preferred_element_type=F32)
    return (jax.nn.silu(hg) * hu * gate).astype(BF16)


FLAG_ACTIVE, FLAG_FIRST, FLAG_LAST, FLAG_NEW_A, FLAG_NEW_B = 1, 2, 4, 8, 16


def _moe_sparse_kernel(src_ref, tile_ref, ea_ref, eb_ref, lo_ref, hi_ref, flag_ref,
                       x_hbm, wga_ref, wua_ref, wda_ref, wgb_ref, wub_ref, wdb_ref, lng_ref, lnb_ref,
                       out_hbm, xbuf, obuf, acc, wg_s, wu_s, wd_s, gsem, ssem, *, tm, n_tiles, n_items):
    i = pl.program_id(0)
    t = tile_ref[i]
    slot = t % 2
    flags = flag_ref[i]

    def start_gather(tile, s):
        for r in range(tm):
            tok = src_ref[tile * tm + r]
            pltpu.make_async_copy(x_hbm.at[pl.ds(tok * WIDE_TILES, WIDE_TILES)],
                                  xbuf.at[s, pl.ds(r * WIDE_TILES, WIDE_TILES)], gsem.at[s]).start()

    def piece(c):
        return xbuf[slot, pl.ds(c, tm, stride=WIDE_TILES), :]

    def activations():
        return jnp.concatenate([piece(c) for c in range(D_MODEL // LANES)], axis=-1)

    def start_scatter(tile, s):
        for r in range(tm):
            tok = src_ref[tile * tm + r]
            pltpu.make_async_copy(obuf.at[s, pl.ds(r, 1)], out_hbm.at[pl.ds(tok, 1)],
                                  ssem.at[s]).start(priority=r % 2)

    def for_slot(value, fn):
        for s in (0, 1):
            pl.when(value == s)(functools.partial(fn, s))

    def wait_gather(s):
        pltpu.make_async_copy(x_hbm.at[pl.ds(0, tm * WIDE_TILES)], xbuf.at[s], gsem.at[s]).wait()

    def wait_scatter(s):
        pltpu.make_async_copy(obuf.at[s], out_hbm.at[pl.ds(0, tm)], ssem.at[s]).wait()

    @pl.when(i == 0)
    def _():
        start_gather(0, 0)

    @pl.when((flags & FLAG_FIRST) != 0)
    def _():
        @pl.when(t + 1 < n_tiles)
        def _():
            for_slot(1 - slot, lambda s: start_gather(t + 1, s))
        wait_gather(slot)
        acc[...] = jnp.zeros_like(acc)

    @pl.when((flags & FLAG_NEW_A) != 0)
    def _():
        wg_s[0] = wga_ref[0, 0].astype(BF16)
        wu_s[0] = wua_ref[0, 0].astype(BF16)
        wd_s[0] = wda_ref[0, 0].astype(BF16)

    @pl.when((flags & FLAG_NEW_B) != 0)
    def _():
        wg_s[1] = wgb_ref[0, 0].astype(BF16)
        wu_s[1] = wub_ref[0, 0].astype(BF16)
        wd_s[1] = wdb_ref[0, 0].astype(BF16)

    @pl.when((flags & FLAG_ACTIVE) != 0)
    def _():
        xb = activations().astype(BF16)
        gates = piece(WIDE_TILES - 1)
        row = t * tm + lax.broadcasted_iota(jnp.int32, (tm, 1), 0)
        in_class = (row >= lo_ref[i]) & (row < hi_ref[i])
        gate_a = jnp.where(in_class, _gate_column(gates, ea_ref[i]), 0.0)
        gate_b = jnp.where(in_class, _gate_column(gates, eb_ref[i]), 0.0)
        hid_a = _expert_hidden(xb, wg_s[0], wu_s[0], gate_a)
        hid_b = _expert_hidden(xb, wg_s[1], wu_s[1], gate_b)
        acc[...] += (jnp.dot(hid_a, wd_s[0], preferred_element_type=F32)
                     + jnp.dot(hid_b, wd_s[1], preferred_element_type=F32))

    @pl.when((flags & FLAG_LAST) != 0)
    def _():
        @pl.when(t >= 2)
        def _():
            wait_scatter(slot)
        obuf[slot] = _layer_norm(DN_ALPHA * activations() + acc[...], lng_ref[...], lnb_ref[...])
        for_slot(slot, lambda s: start_scatter(t, s))

    @pl.when(i == n_items - 1)
    def _():
        wait_scatter(0)
        wait_scatter(1)


def _moe_schedule(cls, *, tm):
    m = cls.shape[0]
    n_tiles = m // tm
    n_items = n_tiles + N_CLASSES - 1
    i32 = jnp.int32
    src = jnp.argsort(cls, stable=True).astype(i32)
    counts = jnp.sum((cls[:, None] == jnp.arange(N_CLASSES, dtype=i32)[None, :]).astype(i32), axis=0)
    ends = jnp.cumsum(counts)
    starts = ends - counts
    def count(mask):
        return jnp.sum(mask.astype(i32), axis=-1)

    def pick(onehot, values):
        return jnp.sum(jnp.where(onehot, values[None, :], 0), axis=-1)

    tile_lo = jnp.arange(n_tiles, dtype=i32) * tm
    first_c = count(ends[None, :] <= tile_lo[:, None])
    last_c = count(starts[None, :] < (tile_lo + tm)[:, None]) - 1
    per_tile = last_c - first_c + 1
    item_end = jnp.cumsum(per_tile)
    idx = jnp.arange(n_items, dtype=i32)
    tile = jnp.minimum(count(item_end[None, :] <= idx[:, None]), n_tiles - 1)
    in_tile = tile[:, None] == jnp.arange(n_tiles, dtype=i32)[None, :]
    end = pick(in_tile, item_end)
    begin = end - pick(in_tile, per_tile)
    active = idx < item_end[-1]
    c = jnp.where(active, pick(in_tile, first_c) + idx - begin, last_c[n_tiles - 1])
    in_class = c[:, None] == jnp.arange(N_CLASSES, dtype=i32)[None, :]
    class_a = [g * EXPERTS_PER_GROUP + pa for g in range(N_GROUPS) for pa, _ in PAIRS]
    class_b = [g * EXPERTS_PER_GROUP + pb for g in range(N_GROUPS) for _, pb in PAIRS]
    ea = pick(in_class, jnp.asarray(class_a, dtype=i32))
    eb = pick(in_class, jnp.asarray(class_b, dtype=i32))
    lo = jnp.where(active, pick(in_class, starts), 0)
    hi = jnp.where(active, pick(in_class, ends), 0)
    new_a = jnp.concatenate([jnp.ones((1,), bool), ea[1:] != ea[:-1]])
    new_b = jnp.concatenate([jnp.ones((1,), bool), eb[1:] != eb[:-1]])
    flags = (active * FLAG_ACTIVE + (active & (idx == begin)) * FLAG_FIRST
             + (active & (idx == end - 1)) * FLAG_LAST + new_a * FLAG_NEW_A + new_b * FLAG_NEW_B)
    return src, tile, ea, eb, lo.astype(i32), hi.astype(i32), flags.astype(i32)


def _moe_sparse(x, gates, layer, wg, wu, wd, lng, lnb, *, tm):
    m = gates.shape[0]
    n_tiles = m // tm
    n_items = n_tiles + N_CLASSES - 1
    cls = gates[:, CLASS_LANE].astype(jnp.int32)
    tables = _moe_schedule(cls, tm=tm)
    kern = functools.partial(_moe_sparse_kernel, tm=tm, n_tiles=n_tiles, n_items=n_items)

    def w_spec(shape, which):
        return pl.BlockSpec((1, 1) + shape, lambda i, src, tile, ea, eb, lo, hi, fl: (layer, (ea, eb)[which][i], 0, 0))

    up, down = (D_MODEL, D_EXPERT), (D_EXPERT, D_MODEL)
    grid_spec = pltpu.PrefetchScalarGridSpec(
        num_scalar_prefetch=len(tables),
        grid=(n_items,),
        in_specs=[
            pl.BlockSpec(memory_space=pl.ANY),
            w_spec(up, 0), w_spec(up, 0), w_spec(down, 0), w_spec(up, 1), w_spec(up, 1), w_spec(down, 1),
            _const_spec(lng.shape), _const_spec(lnb.shape),
        ],
        out_specs=pl.BlockSpec(memory_space=pl.ANY),
        scratch_shapes=[
            pltpu.VMEM((2, tm * WIDE_TILES, LANES), F32), pltpu.VMEM((2, tm, D_MODEL), F32),
            pltpu.VMEM((tm, D_MODEL), F32),
            pltpu.VMEM((2,) + up, BF16), pltpu.VMEM((2,) + up, BF16), pltpu.VMEM((2,) + down, BF16),
            pltpu.SemaphoreType.DMA((2,)), pltpu.SemaphoreType.DMA((2,)),
        ],
    )
    return pl.pallas_call(
        kern,
        grid_spec=grid_spec,
        out_shape=jax.ShapeDtypeStruct((m, D_MODEL), F32),
        compiler_params=_params("arbitrary"),
        name="moe_sparse",
    )(*tables, x, wg, wu, wd, wg, wu, wd, lng, lnb)


def _row(v):
    return v.reshape(1, -1)


def _to_time_major(s):
    return jnp.transpose(s, (1, 0, 2)).reshape(1, -1, s.shape[-1])


def _from_time_major(s, steps):
    return jnp.transpose(s.reshape(steps, DEC_BATCH, s.shape[-1]), (1, 0, 2))


def _kv_lane_view(cache):
    d, n = cache.shape[:2]
    halves = MEM_HEAD_DIM // LANES
    v = cache.reshape(d, n, N_MEM, MEM_HEADS, halves, LANES)
    return jnp.transpose(v, (0, 1, 2, 4, 3, 5)).reshape(d, n, KV_ROWS, LANES)


def _kv_from_lane_view(view):
    d, n = view.shape[:2]
    halves = MEM_HEAD_DIM // LANES
    v = view.reshape(d, n, N_MEM, halves, MEM_HEADS, LANES)
    return jnp.transpose(v, (0, 1, 2, 4, 3, 5)).reshape(d, n, N_MEM, MEM_HEADS, MEM_HEAD_DIM)


def kernel(x_prompt, x_sample, cache_mem_k, cache_mem_v, state_pool, state_conv_c, state_conv_d, mem_prompt,
           w_in_ab, ln_v_g, ln_v_b, w_spatial, b_spatial, w_pool, pool_scale, w_out_ab,
           w_in_cd, conv_c_w, conv_c_b, ln_c_g, ln_c_b, conv_d_w, w_out_cd,
           w_q, w_k, w_v, w_o, w_group, b_group, w_router, b_router, w_gate, w_up, w_down, ln_g, ln_b):
    mem_flat = mem_prompt.reshape(BATCH * N_MEM, D_MODEL)
    cache_k = _kv_lane_view(cache_mem_k)
    cache_v = _kv_lane_view(cache_mem_v)
    kp, vp, kp_view, vp_view = _kv_proj(mem_flat, w_k.astype(BF16), w_v.astype(BF16))

    w_in_ab, w_out_ab, w_pool = w_in_ab.astype(BF16), w_out_ab.astype(BF16), w_pool.astype(BF16)
    w_in_cd, w_out_cd = w_in_cd.astype(BF16), w_out_cd.astype(BF16)
    w_q, w_o = w_q.astype(BF16), w_o.astype(BF16)
    pad = ROUTER_LANES - N_GROUPS - N_EXPERTS
    w_route = jnp.pad(jnp.concatenate([w_group, w_router], axis=-1), ((0, 0), (0, 0), (0, pad)))
    b_route = jnp.pad(jnp.concatenate([b_group, b_router], axis=-1), ((0, 0), (0, pad)))[:, None, :]

    prompt = _Slab(x_prompt.reshape(PROMPT_ROWS, D_MODEL), 0, PROMPT_ROWS, 0, None)
    sample = _Slab(_to_time_major(x_sample)[0], 0, SAMPLE_ROWS, PROMPT_ROWS, None)

    def both(h):
        return (_Slab(h, 0, PROMPT_ROWS, 0, None), _Slab(h, PROMPT_ROWS, SAMPLE_ROWS, PROMPT_ROWS, None))

    pool_p, pool_s, chunk_v_s = [], [], []
    conv_c_p, conv_c_s, conv_d_p, conv_d_s = [], [], [], []

    for l in range(DEPTH):
        i = l // 2
        lng0, lnb0 = _row(ln_g[l, 0]), _row(ln_b[l, 0])
        if l % 2 == 0:
            bias = jnp.repeat(b_spatial[i].T, A_HEAD_DIM, axis=1)
            small = jnp.repeat(
                jnp.transpose(w_spatial[i][:, :DEC_SEQ, :DEC_SEQ], (1, 2, 0)).reshape(DEC_SEQ * DEC_SEQ, A_HEADS),
                A_HEAD_DIM, axis=1)
            common = (_Layer(w_in_ab, i), _row(ln_v_g[i]), _row(ln_v_b[i]))
            tail = (_Layer(w_pool, i), _row(pool_scale[i]), _Layer(w_out_ab, i), lng0, lnb0)
            h, hist_p = _mixer_ab(prompt, jnp.zeros((BATCH, POOL_HIST, D_B), F32), *common, w_spatial[i], bias,
                                  *tail, seqs=1, tt=512, pos0=0, with_v=False)
            h, v_rows, hist_s = _mixer_ab(sample._replace(dst=(h,)), _to_time_major(state_pool[i]), *common, small,
                                          bias[:DEC_SEQ], *tail, seqs=DEC_BATCH, tt=DEC_SEQ, pos0=PAST_LEN,
                                          with_v=True)
            pool_p.append(hist_p)
            pool_s.append(_from_time_major(hist_s, POOL_HIST))
            chunk_v_s.append(_from_time_major(v_rows, DEC_SEQ))
        else:
            wts = (_Layer(w_in_cd, i), conv_c_w[i], _row(conv_c_b[i]), _row(ln_c_g[i]), _row(ln_c_b[i]),
                   conv_d_w[i], _Layer(w_out_cd, i), lng0, lnb0)
            h, hc_p, hd_p = _mixer_cd(prompt, jnp.zeros((BATCH, CONV_C - 1, D_C), F32),
                                      jnp.zeros((BATCH, CONV_D - 1, D_D), F32), *wts, seqs=1, tt=512)
            h, hc_s, hd_s = _mixer_cd(sample._replace(dst=(h,)), _to_time_major(state_conv_c[i]),
                                      _to_time_major(state_conv_d[i]), *wts, seqs=DEC_BATCH, tt=DEC_SEQ)
            conv_c_p.append(hc_p)
            conv_d_p.append(hd_p)
            conv_c_s.append(_from_time_major(hc_s, CONV_C - 1))
            conv_d_s.append(_from_time_major(hd_s, CONV_D - 1))
        prompt, sample = both(h)

        att_w = (_Layer(w_q, l), _Layer(w_o, l), _Layer(w_route, l), _Layer(b_route, l),
                 _row(ln_g[l, 1]), _row(ln_b[l, 1]))
        wide_and_gates = _attn_prompt(prompt, kp, vp, l, *att_w, tq=1024)
        h_wide, gates = _attn_sample(sample._replace(dst=tuple(wide_and_gates)), cache_k, cache_v, l, *att_w,
                                     seq_blk=4)

        h = _moe_sparse(h_wide, gates, l, w_gate, w_up, w_down, _row(ln_g[l, 2]), _row(ln_b[l, 2]), tm=256)
        prompt, sample = both(h)

    y_prompt = h[:PROMPT_ROWS].reshape(BATCH, SEQ, D_MODEL)
    y_sample = _from_time_major(h[PROMPT_ROWS:], DEC_SEQ)
    return (y_prompt, y_sample, _kv_from_lane_view(kp_view), _kv_from_lane_view(vp_view), jnp.stack(pool_p),
            jnp.stack(conv_c_p), jnp.stack(conv_d_p), jnp.stack(chunk_v_s), jnp.stack(pool_s),
            jnp.stack(conv_c_s), jnp.stack(conv_d_s))
```

```python
import functools
from typing import Callable, NamedTuple, Optional

import jax
import jax.numpy as jnp
from jax import lax
from jax.experimental import pallas as pl
from jax.experimental.pallas import tpu as pltpu

D_MODEL = 1024
BATCH = 8
SEQ = 2048
DEPTH = 4
DEC_BATCH = 128
DEC_SEQ = 4
PAST_LEN = 16384

CHUNK = 128
A_HEADS = 4
D_A = D_MODEL // 2
A_HEAD_DIM = D_A // A_HEADS
POOL_WINDOWS = (2, 4, 8, 16)
B_GROUPS = len(POOL_WINDOWS)
D_B = D_MODEL // 2
B_GROUP_DIM = D_B // B_GROUPS
POOL_HIST = max(POOL_WINDOWS) - 1
D_C = D_MODEL // 2
CONV_C = 31
D_D = D_MODEL // 2
CONV_D = 3
N_MEM = 256
MEM_HEADS = 4
MEM_HEAD_DIM = D_MODEL // MEM_HEADS
N_GROUPS = 4
EXPERTS_PER_GROUP = 4
N_EXPERTS = N_GROUPS * EXPERTS_PER_GROUP
D_EXPERT = 512
DN_ALPHA = (2 * DEPTH) ** 0.25
LN_EPS = 1e-5
PROMPT_ROWS = BATCH * SEQ
SAMPLE_ROWS = DEC_BATCH * DEC_SEQ
TOTAL_ROWS = PROMPT_ROWS + SAMPLE_ROWS

LANES = 128
SUBLANES = 8
POOL_HIST_PAD = 16
CONV_C_HIST_PAD = 32
CONV_D_HIST_PAD = 8
ROUTER_LANES = 128
ROUTER_MASKED = -1e30
CLASS_LANE = N_GROUPS + N_EXPERTS
PAIRS = ((0, 1), (0, 2), (1, 2), (1, 3), (0, 3), (2, 3))
N_CLASSES = N_GROUPS * len(PAIRS)
D_WIDE = D_MODEL + ROUTER_LANES
WIDE_TILES = D_WIDE // LANES
KV_ROWS = N_MEM * MEM_HEADS * MEM_HEAD_DIM // LANES
ATT_ROWS = KV_ROWS // N_MEM * DEC_SEQ
VMEM_LIMIT_BYTES = 52 * 1024 * 1024

F32 = jnp.float32
BF16 = jnp.bfloat16


def _layer_norm(x, g, b):
    mu = jnp.mean(x, axis=-1, keepdims=True)
    xc = x - mu
    var = jnp.mean(xc * xc, axis=-1, keepdims=True)
    return xc * lax.rsqrt(var + LN_EPS) * g + b


def _dot(a, b):
    return jnp.dot(a.astype(BF16), b.astype(BF16), preferred_element_type=F32)


def _params(*semantics):
    return pltpu.CompilerParams(dimension_semantics=semantics, vmem_limit_bytes=VMEM_LIMIT_BYTES)


def _const_spec(shape):
    nd = len(shape)
    return pl.BlockSpec(shape, lambda *_: (0,) * nd)


class _Layer(NamedTuple):
    stack: jax.Array
    layer: int


def _param_spec(p):
    if isinstance(p, _Layer):
        nd = p.stack.ndim
        return pl.BlockSpec((1,) + p.stack.shape[1:], lambda *_: (p.layer,) + (0,) * (nd - 1))
    return _const_spec(p.shape)


def _param_arg(p):
    return p.stack if isinstance(p, _Layer) else p


class _Slab(NamedTuple):
    array: jax.Array
    row0: int
    n_rows: int
    out_row0: int
    dst: Optional[tuple]


class _SlabIO(NamedTuple):
    in_spec: pl.BlockSpec
    out_specs: list
    out_shapes: list
    alias_spec: list
    alias_arg: tuple
    aliases: dict
    adapt: Callable


def _slab_io(src, rows, outs, block_index, *, n_inputs):
    assert src.row0 % rows == 0 and src.out_row0 % rows == 0 and src.n_rows % rows == 0
    in0, out0 = src.row0 // rows, src.out_row0 // rows
    in_spec = pl.BlockSpec((rows, src.array.shape[1]), lambda *g: (in0 + block_index(*g), 0))
    out_specs, out_shapes = [], []
    for width, row_major_tiles in outs:
        pieces, lanes = (width // LANES, LANES) if row_major_tiles else (1, width)
        out_specs.append(pl.BlockSpec((rows * pieces, lanes), lambda *g: (out0 + block_index(*g), 0)))
        out_shapes.append(jax.ShapeDtypeStruct((TOTAL_ROWS * pieces, lanes), F32))
    if src.dst is None:
        return _SlabIO(in_spec, out_specs, out_shapes, [], (), {}, lambda kernel: kernel)
    n_dst = len(outs)
    assert len(src.dst) == n_dst

    def adapt(kernel):
        return lambda *refs: kernel(*refs[:n_inputs], *refs[n_inputs + n_dst:])

    return _SlabIO(in_spec, out_specs, out_shapes, [pl.BlockSpec(memory_space=pl.ANY)] * n_dst, tuple(src.dst),
                   {n_inputs + j: j for j in range(n_dst)}, adapt)


def _mixer_ab_kernel(x_ref, hist_ref, w_in_ref, lnv_g_ref, lnv_b_ref, ws_ref, bs_ref, wpool_ref,
                     pscale_ref, w_out_ref, lng_ref, lnb_ref, *refs, seqs, tt, pos0, with_v):
    if with_v:
        h_out_ref, v_out_ref, hist_out_ref, zbuf = refs
    else:
        h_out_ref, hist_out_ref, zbuf = refs
    t_idx = pl.program_id(1)
    rows = tt * seqs
    hp = POOL_HIST_PAD

    @pl.when(t_idx == 0)
    def _():
        zbuf[pl.ds(0, (hp - POOL_HIST) * seqs), :] = jnp.zeros(((hp - POOL_HIST) * seqs, D_B), F32)
        zbuf[pl.ds((hp - POOL_HIST) * seqs, POOL_HIST * seqs), :] = hist_ref[0]

    x = x_ref[...]
    h = _dot(x, w_in_ref[0])
    ua = jax.nn.gelu(h[:, :2 * D_A])
    u = ua[:, :D_A]
    v = _layer_norm(ua[:, D_A:], lnv_g_ref[...], lnv_b_ref[...])
    z = h[:, 2 * D_A:]
    if with_v:
        v_out_ref[0] = v
    zbuf[pl.ds(hp * seqs, rows), :] = z

    if seqs == 1:
        tri = (lax.broadcasted_iota(jnp.int32, (CHUNK, CHUNK), 0)
               >= lax.broadcasted_iota(jnp.int32, (CHUNK, CHUNK), 1))
        w_heads = [jnp.where(tri, ws_ref[hh], 0.0).astype(BF16) for hh in range(A_HEADS)]
        vb = v.astype(BF16)
        chunks = []
        for c in range(tt // CHUNK):
            heads = []
            for hh in range(A_HEADS):
                vc = vb[c * CHUNK:(c + 1) * CHUNK, hh * A_HEAD_DIM:(hh + 1) * A_HEAD_DIM]
                heads.append(jnp.dot(w_heads[hh], vc, preferred_element_type=F32))
            chunks.append(jnp.concatenate(heads, axis=-1) + bs_ref[...])
        mixed = jnp.concatenate(chunks, axis=0)
    else:
        parts = []
        for t in range(tt):
            acc = bs_ref[t:t + 1, :]
            for s in range(t + 1):
                acc = acc + ws_ref[t * tt + s:t * tt + s + 1, :] * v[s * seqs:(s + 1) * seqs, :]
            parts.append(acc)
        mixed = jnp.concatenate(parts, axis=0)
    a_out = u * mixed

    run = zbuf[...]
    width = 1
    outs = []
    for gi, w in enumerate(POOL_WINDOWS):
        lo, hi = gi * B_GROUP_DIM, (gi + 1) * B_GROUP_DIM
        while width < w:
            run = run + pltpu.roll(run, width * seqs, axis=0)
            width *= 2
        assert width == w
        acc = run[hp * seqs:, :B_GROUP_DIM]
        run = run[:, B_GROUP_DIM:]
        if pos0 + 1 >= w:
            cnt = float(w)
        else:
            assert seqs == 1
            pos = pos0 + t_idx * tt + lax.broadcasted_iota(jnp.int32, (rows, 1), 0)
            cnt = jnp.minimum(pos + 1, w).astype(F32)
        pooled = acc / cnt - z[:, lo:hi]
        outs.append(_dot(pooled, wpool_ref[0, gi]))
    b_out = jnp.concatenate(outs, axis=-1) * pscale_ref[...]

    y = _dot(jnp.concatenate([a_out, b_out], axis=-1), w_out_ref[0])
    h_out_ref[...] = _layer_norm(DN_ALPHA * x + y, lng_ref[...], lnb_ref[...])

    hist_out_ref[0] = zbuf[pl.ds((tt + hp - POOL_HIST) * seqs, POOL_HIST * seqs), :]
    zbuf[pl.ds(0, hp * seqs), :] = zbuf[pl.ds(tt * seqs, hp * seqs), :]


def _mixer_ab(src, hist, w_in, lnv_g, lnv_b, ws, bs, wpool, pscale, w_out, lng, lnb, *, seqs, tt, pos0, with_v):
    nb = hist.shape[0]
    rows = tt * seqs
    n_t = src.n_rows // (nb * rows)
    kern = functools.partial(_mixer_ab_kernel, seqs=seqs, tt=tt, pos0=pos0, with_v=with_v)
    weights = (w_in, lnv_g, lnv_b, ws, bs, wpool, pscale, w_out, lng, lnb)
    io = _slab_io(src, rows, [(D_MODEL, False)], lambda n, t: n * n_t + t, n_inputs=2 + len(weights))
    v_spec = pl.BlockSpec((1, rows, D_A), lambda n, t: (n, t, 0))
    hist_spec = pl.BlockSpec((1, POOL_HIST * seqs, D_B), lambda n, t: (n, 0, 0))
    v_shape = jax.ShapeDtypeStruct((nb, n_t * rows, D_A), F32)
    hist_shape = jax.ShapeDtypeStruct((nb, POOL_HIST * seqs, D_B), F32)
    return pl.pallas_call(
        io.adapt(kern),
        grid=(nb, n_t),
        in_specs=[io.in_spec, hist_spec] + [_param_spec(w) for w in weights] + io.alias_spec,
        out_specs=io.out_specs + ([v_spec, hist_spec] if with_v else [hist_spec]),
        out_shape=io.out_shapes + ([v_shape, hist_shape] if with_v else [hist_shape]),
        input_output_aliases=io.aliases,
        scratch_shapes=[pltpu.VMEM(((POOL_HIST_PAD + tt) * seqs, D_B), F32)],
        compiler_params=_params("parallel", "arbitrary"),
        name="mixer_ab",
    )(src.array, hist, *map(_param_arg, weights), *io.alias_arg)


def _shift_rows(taps, hist_pad):
    base = hist_pad - (taps - 1)
    return max((base + k) // SUBLANES * SUBLANES for k in range(taps) if (base + k) % SUBLANES)


def _dwconv(buf, w_ref, out, shifted, *, taps, hist_pad, seqs, rows, chunk):
    base = (hist_pad - (taps - 1)) * seqs
    if shifted is not None:
        span = rows + _shift_rows(taps, hist_pad)
        for b in range(1, SUBLANES):
            shifted[b - 1] = buf[pl.ds(b, span), :]

    def tap(k, r0, n, lanes):
        off = base + k * seqs
        phase = off % SUBLANES
        if shifted is None or phase == 0:
            return buf[pl.ds(off + r0, n), lanes]
        return shifted[phase - 1, pl.ds(off - phase + r0, n), lanes]

    if shifted is None:
        def full_width(i, carry):
            r0 = i * chunk if isinstance(i, int) else pl.multiple_of(i * chunk, chunk)
            acc = w_ref[0:1, :] * tap(0, r0, chunk, slice(None))
            for k in range(1, taps):
                acc = acc + w_ref[k:k + 1, :] * tap(k, r0, chunk, slice(None))
            out[pl.ds(r0, chunk), :] = acc
            return carry

        if seqs % SUBLANES == 0:
            lax.fori_loop(0, rows // chunk, full_width, 0)
        else:
            for i in range(rows // chunk):
                full_width(i, 0)
        return

    for lt in range(buf.shape[1] // LANES):
        lanes = slice(lt * LANES, (lt + 1) * LANES)
        wk = [jnp.broadcast_to(w_ref[k:k + 1, lanes], (SUBLANES, LANES)) for k in range(taps)]
        for rg in range(0, rows, SUBLANES):
            acc = wk[0] * tap(0, rg, SUBLANES, lanes)
            for k in range(1, taps):
                acc = acc + wk[k] * tap(k, rg, SUBLANES, lanes)
            out[pl.ds(rg, SUBLANES), lanes] = acc


def _mixer_cd_kernel(x_ref, hist_c_ref, hist_d_ref, w_in_ref, ccw_ref, ccb_ref, lncg_ref, lncb_ref,
                     cdw_ref, w_out_ref, lng_ref, lnb_ref,
                     h_out_ref, hist_c_out_ref, hist_d_out_ref, cbuf, dbuf, cc_scr, cd_scr, *maybe_shifted,
                     seqs, tt):
    shifted = maybe_shifted[0] if maybe_shifted else None
    t_idx = pl.program_id(1)
    rows = tt * seqs
    hc, hd = CONV_C_HIST_PAD, CONV_D_HIST_PAD
    nc, nd = CONV_C - 1, CONV_D - 1

    @pl.when(t_idx == 0)
    def _():
        cbuf[pl.ds((hc - nc) * seqs, nc * seqs), :] = hist_c_ref[0]
        dbuf[pl.ds((hd - nd) * seqs, nd * seqs), :] = hist_d_ref[0]

    x = x_ref[...]
    h = _dot(x, w_in_ref[0])
    glu = h[:, :D_C] * jax.nn.sigmoid(h[:, D_C:2 * D_C])
    o = 2 * D_C
    gate_b = h[:, o:o + D_D]
    gx = h[:, o + D_D:o + 2 * D_D] * h[:, o + 2 * D_D:]
    cbuf[pl.ds(hc * seqs, rows), :] = glu
    dbuf[pl.ds(hd * seqs, rows), :] = gx

    chunk = min(rows, 32)
    _dwconv(cbuf, ccw_ref, cc_scr, shifted, taps=CONV_C, hist_pad=hc, seqs=seqs, rows=rows, chunk=chunk)
    _dwconv(dbuf, cdw_ref, cd_scr, None, taps=CONV_D, hist_pad=hd, seqs=seqs, rows=rows, chunk=chunk)

    c_out = jax.nn.silu(_layer_norm(cc_scr[...] + ccb_ref[...], lncg_ref[...], lncb_ref[...]))
    d_out = gate_b * cd_scr[...]
    y = _dot(jnp.concatenate([c_out, d_out], axis=-1), w_out_ref[0])
    h_out_ref[...] = _layer_norm(DN_ALPHA * x + y, lng_ref[...], lnb_ref[...])

    hist_c_out_ref[0] = cbuf[pl.ds((tt + hc - nc) * seqs, nc * seqs), :]
    hist_d_out_ref[0] = dbuf[pl.ds((tt + hd - nd) * seqs, nd * seqs), :]
    cbuf[pl.ds(0, hc * seqs), :] = cbuf[pl.ds(tt * seqs, hc * seqs), :]
    dbuf[pl.ds(0, hd * seqs), :] = dbuf[pl.ds(tt * seqs, hd * seqs), :]


def _mixer_cd(src, hist_c, hist_d, w_in, ccw, ccb, lncg, lncb, cdw, w_out, lng, lnb, *, seqs, tt):
    nb = hist_c.shape[0]
    rows = tt * seqs
    n_t = src.n_rows // (nb * rows)
    nc, nd = CONV_C - 1, CONV_D - 1
    kern = functools.partial(_mixer_cd_kernel, seqs=seqs, tt=tt)
    weights = (w_in, ccw, ccb, lncg, lncb, cdw, w_out, lng, lnb)
    io = _slab_io(src, rows, [(D_MODEL, False)], lambda n, t: n * n_t + t, n_inputs=3 + len(weights))
    hist_c_spec = pl.BlockSpec((1, nc * seqs, D_C), lambda n, t: (n, 0, 0))
    hist_d_spec = pl.BlockSpec((1, nd * seqs, D_D), lambda n, t: (n, 0, 0))
    return pl.pallas_call(
        io.adapt(kern),
        grid=(nb, n_t),
        in_specs=[io.in_spec, hist_c_spec, hist_d_spec] + [_param_spec(w) for w in weights] + io.alias_spec,
        out_specs=io.out_specs + [hist_c_spec, hist_d_spec],
        out_shape=io.out_shapes + [
            jax.ShapeDtypeStruct((nb, nc * seqs, D_C), F32),
            jax.ShapeDtypeStruct((nb, nd * seqs, D_D), F32),
        ],
        input_output_aliases=io.aliases,
        scratch_shapes=[
            pltpu.VMEM(((CONV_C_HIST_PAD + tt) * seqs, D_C), F32),
            pltpu.VMEM(((CONV_D_HIST_PAD + tt) * seqs, D_D), F32),
            pltpu.VMEM((rows, D_C), F32),
            pltpu.VMEM((rows, D_D), F32),
        ] + ([pltpu.VMEM((SUBLANES - 1, rows + _shift_rows(CONV_C, CONV_C_HIST_PAD), D_C), F32)]
             if seqs % SUBLANES else []),
        compiler_params=_params("parallel", "arbitrary"),
        name="mixer_cd",
    )(src.array, hist_c, hist_d, *map(_param_arg, weights), *io.alias_arg)


def _kv_proj_kernel(mem_ref, wk_ref, wv_ref, k_ref, v_ref, kview_ref, vview_ref):
    m = mem_ref[...].astype(BF16)
    halves = MEM_HEAD_DIM // LANES
    for w_ref, o_ref, view_ref in ((wk_ref, k_ref, kview_ref), (wv_ref, v_ref, vview_ref)):
        y = jnp.dot(m, w_ref[0], preferred_element_type=F32)
        o_ref[0] = y.astype(BF16)
        for hh in range(MEM_HEADS):
            for lt in range(halves):
                col = hh * MEM_HEAD_DIM + lt * LANES
                view_ref[0, 0, pl.ds(lt * MEM_HEADS + hh, N_MEM, stride=halves * MEM_HEADS), :] = (
                    y[:, col:col + LANES])


def _kv_proj(mem, wk, wv):
    rows = jax.ShapeDtypeStruct((DEPTH, BATCH * N_MEM, D_MODEL), BF16)
    view = jax.ShapeDtypeStruct((DEPTH, BATCH, KV_ROWS, LANES), F32)
    w_spec = pl.BlockSpec((1, D_MODEL, D_MODEL), lambda l, n: (l, 0, 0))
    return pl.pallas_call(
        _kv_proj_kernel,
        grid=(DEPTH, BATCH),
        in_specs=[pl.BlockSpec((N_MEM, D_MODEL), lambda l, n: (n, 0)), w_spec, w_spec],
        out_specs=[pl.BlockSpec((1, N_MEM, D_MODEL), lambda l, n: (l, n, 0))] * 2
        + [pl.BlockSpec((1, 1, KV_ROWS, LANES), lambda l, n: (l, n, 0, 0))] * 2,
        out_shape=[rows, rows, view, view],
        compiler_params=_params("parallel", "arbitrary"),
        name="kv_proj",
    )(mem, wk, wv)


def _attend(q, k, v):
    scale = MEM_HEAD_DIM ** -0.5
    qb = q.astype(BF16)
    outs = []
    for hh in range(MEM_HEADS):
        lo, hi = hh * MEM_HEAD_DIM, (hh + 1) * MEM_HEAD_DIM
        s = lax.dot_general(qb[:, lo:hi], k[:, lo:hi], (((1,), (1,)), ((), ())),
                            preferred_element_type=F32) * scale
        s = s - jnp.max(s, axis=-1, keepdims=True)
        p = jnp.exp(s)
        p = p / jnp.sum(p, axis=-1, keepdims=True)
        outs.append(jnp.dot(p.astype(BF16), v[:, lo:hi], preferred_element_type=F32))
    return jnp.concatenate(outs, axis=-1)


def _attn_prompt_kernel(x_ref, k_ref, v_ref, wq_ref, wo_ref, wr_ref, br_ref, lng_ref, lnb_ref, o_ref, g_ref):
    x = x_ref[...]
    q = _dot(x, wq_ref[0])
    o = _attend(q, k_ref[0], v_ref[0])
    y = _dot(o, wo_ref[0])
    h2 = _layer_norm(DN_ALPHA * x + y, lng_ref[...], lnb_ref[...])
    _store_wide_rows(o_ref, g_ref, h2, _route(_router_logits(h2, wr_ref[0]) + br_ref[0]))


def _store_wide_rows(o_ref, g_ref, h2, gates):
    rows = h2.shape[0]
    for c in range(WIDE_TILES):
        piece = gates if c == WIDE_TILES - 1 else h2[:, c * LANES:(c + 1) * LANES]
        o_ref[pl.ds(c, rows, stride=WIDE_TILES), :] = piece
    g_ref[...] = gates


ATTN_OUTS = [(D_WIDE, True), (ROUTER_LANES, False)]


def _attn_prompt(src, k, v, layer, wq, wo, wr, br, lng, lnb, *, tq):
    n_t = src.n_rows // (BATCH * tq)
    weights = (wq, wo, wr, br, lng, lnb)
    io = _slab_io(src, tq, ATTN_OUTS, lambda n, i: n * n_t + i, n_inputs=3 + len(weights))
    kv_spec = pl.BlockSpec((1, N_MEM, D_MODEL), lambda n, i: (layer, n, 0))
    return pl.pallas_call(
        io.adapt(_attn_prompt_kernel),
        grid=(BATCH, n_t),
        in_specs=[io.in_spec, kv_spec, kv_spec] + [_param_spec(w) for w in weights] + io.alias_spec,
        out_specs=io.out_specs,
        out_shape=io.out_shapes,
        input_output_aliases=io.aliases,
        compiler_params=_params("parallel", "arbitrary"),
        name="attn_prompt",
    )(src.array, k, v, *map(_param_arg, weights), *io.alias_arg)


def _att_row(half, head, t):
    return (half * MEM_HEADS + head) * DEC_SEQ + t


def _attn_sample_kernel(x_ref, k_ref, v_ref, wq_ref, wo_ref, wr_ref, br_ref, lng_ref, lnb_ref, o_ref, g_ref,
                        q_scr, qp_scr, op_scr, *, seq_blk, n_blk):
    i = pl.program_id(0)
    halves = MEM_HEAD_DIM // LANES
    blocks = [(t, hh, lt) for t in range(DEC_SEQ) for hh in range(MEM_HEADS) for lt in range(halves)]

    @pl.when(i == 0)
    def _():
        q_scr[...] = _dot(x_ref[...], wq_ref[0])
        for t, hh, lt in blocks:
            col = hh * MEM_HEAD_DIM + lt * LANES
            qp_scr[pl.ds(_att_row(lt, hh, t), DEC_BATCH, stride=ATT_ROWS), :] = (
                q_scr[t * DEC_BATCH:(t + 1) * DEC_BATCH, col:col + LANES])

    half_rows = ATT_ROWS // halves
    col = lax.broadcasted_iota(jnp.int32, (half_rows, KV_ROWS), 1)
    row_head = lax.shift_right_logical(lax.broadcasted_iota(jnp.int32, (half_rows, KV_ROWS), 0),
                                       DEC_SEQ.bit_length() - 1)
    col_cls = col & (halves * MEM_HEADS - 1)
    match0 = col_cls == row_head
    match1 = col_cls == row_head + MEM_HEADS
    scale = MEM_HEAD_DIM ** -0.5

    def body(j, carry):
        n = i * seq_blk + j
        r0 = pl.multiple_of(n * ATT_ROWS, ATT_ROWS)
        qp = qp_scr[pl.ds(r0, ATT_ROWS), :].astype(BF16)
        s = lax.dot_general(qp, k_ref[0, j].astype(BF16), (((1,), (1,)), ((), ())),
                            preferred_element_type=F32)
        part = jnp.where(match0, s[:half_rows], 0.0) + pltpu.roll(
            jnp.where(match1, s[half_rows:], 0.0), KV_ROWS - MEM_HEADS, axis=1)
        sv = jnp.where(match0, part * scale, ROUTER_MASKED)
        e = jnp.exp(sv - jnp.max(sv, axis=-1, keepdims=True))
        p = e / jnp.sum(e, axis=-1, keepdims=True)
        pp = jnp.concatenate([p, pltpu.roll(p, MEM_HEADS, axis=1)], axis=0).astype(BF16)
        op_scr[pl.ds(r0, ATT_ROWS), :] = jnp.dot(pp, v_ref[0, j].astype(BF16), preferred_element_type=F32)
        return carry

    lax.fori_loop(0, seq_blk, body, 0, unroll=True)

    @pl.when(i == n_blk - 1)
    def _():
        for t, hh, lt in blocks:
            col = hh * MEM_HEAD_DIM + lt * LANES
            q_scr[t * DEC_BATCH:(t + 1) * DEC_BATCH, col:col + LANES] = (
                op_scr[pl.ds(_att_row(lt, hh, t), DEC_BATCH, stride=ATT_ROWS), :])
        x = x_ref[...]
        y = _dot(q_scr[...], wo_ref[0])
        h2 = _layer_norm(DN_ALPHA * x + y, lng_ref[...], lnb_ref[...])
        _store_wide_rows(o_ref, g_ref, h2, _route(_router_logits(h2, wr_ref[0]) + br_ref[0]))


def _attn_sample(src, k, v, layer, wq, wo, wr, br, lng, lnb, *, seq_blk):
    m = src.n_rows
    n_blk = DEC_BATCH // seq_blk
    kern = functools.partial(_attn_sample_kernel, seq_blk=seq_blk, n_blk=n_blk)
    weights = (wq, wo, wr, br, lng, lnb)
    io = _slab_io(src, m, ATTN_OUTS, lambda i: 0, n_inputs=3 + len(weights))
    kv_spec = pl.BlockSpec((1, seq_blk, KV_ROWS, LANES), lambda i: (layer, i, 0, 0))
    return pl.pallas_call(
        io.adapt(kern),
        grid=(n_blk,),
        in_specs=[io.in_spec, kv_spec, kv_spec] + [_param_spec(w) for w in weights] + io.alias_spec,
        out_specs=io.out_specs,
        out_shape=io.out_shapes,
        input_output_aliases=io.aliases,
        scratch_shapes=[pltpu.VMEM((m, D_MODEL), F32),
                        pltpu.VMEM((DEC_BATCH * ATT_ROWS, LANES), F32),
                        pltpu.VMEM((DEC_BATCH * ATT_ROWS, LANES), F32)],
        compiler_params=_params("arbitrary"),
        name="attn_sample",
    )(src.array, k, v, *map(_param_arg, weights), *io.alias_arg)


def _split_bf16(x):
    hi = x.astype(BF16)
    return hi, (x - hi.astype(F32)).astype(BF16)


def _router_logits(h, w):
    h_hi, h_lo = _split_bf16(h)
    w_hi, w_lo = _split_bf16(w)
    a = jnp.dot(h_hi, jnp.concatenate([w_hi, w_lo], axis=-1), preferred_element_type=F32)
    b = jnp.dot(h_lo, w_hi, preferred_element_type=F32)
    return a[:, :ROUTER_LANES] + a[:, ROUTER_LANES:] + b


def _route(logits):
    lane = lax.broadcasted_iota(jnp.int32, logits.shape, 1)
    lane_f = lane.astype(F32)
    neg = ROUTER_MASKED
    is_grp = lane < N_GROUPS
    gl = jnp.where(is_grp, logits, neg)
    gmax = jnp.max(gl, axis=-1, keepdims=True)
    gsel = jnp.min(jnp.where(gl == gmax, lane_f, float(ROUTER_LANES)), axis=-1, keepdims=True)
    g_w = 1.0 / jnp.sum(jnp.exp(gl - gmax), axis=-1, keepdims=True)
    e_idx = lane - N_GROUPS
    e_grp = lax.shift_right_arithmetic(e_idx, 2).astype(F32)
    in_grp = (e_idx >= 0) & (e_idx < N_EXPERTS) & (e_grp == gsel)
    el = jnp.where(in_grp, logits, neg)
    m1 = jnp.max(el, axis=-1, keepdims=True)
    i1 = jnp.min(jnp.where(el == m1, lane_f, float(ROUTER_LANES)), axis=-1, keepdims=True)
    el2 = jnp.where(lane_f == i1, neg, el)
    m2 = jnp.max(el2, axis=-1, keepdims=True)
    i2 = jnp.min(jnp.where(el2 == m2, lane_f, float(ROUTER_LANES)), axis=-1, keepdims=True)
    e2 = jnp.exp(m2 - m1)
    den = 1.0 + e2
    w1 = (1.0 / den) * g_w
    w2 = (e2 / den) * g_w
    first_lane = N_GROUPS + EXPERTS_PER_GROUP * gsel
    a = jnp.minimum(i1, i2) - first_lane
    b = jnp.maximum(i1, i2) - first_lane
    cls = gsel * len(PAIRS)
    for j, (pa, pb) in enumerate(PAIRS):
        cls = cls + jnp.where((a == pa) & (b == pb), float(j), 0.0)
    return (jnp.where(lane_f == i1, w1, 0.0) + jnp.where(lane_f == i2, w2, 0.0)
            + jnp.where(lane == CLASS_LANE, cls, 0.0))


def _gate_column(gates, expert):
    lane = lax.broadcasted_iota(jnp.int32, gates.shape, 1)
    return jnp.sum(jnp.where(lane == expert + N_GROUPS, gates, 0.0), axis=-1, keepdims=True)


def _expert_hidden(xb, wg, wu, gate):
    hg = jnp.dot(xb, wg, preferred_element_type=F32)
    hu = jnp.dot(xb, wu, preferred_element_type=F32)
    return (jax.nn.silu(hg) * hu * gate).astype(BF16)


FLAG_ACTIVE, FLAG_FIRST, FLAG_LAST, FLAG_NEW_A, FLAG_NEW_B = 1, 2, 4, 8, 16


def _moe_sparse_kernel(src_ref, tile_ref, ea_ref, eb_ref, lo_ref, hi_ref, flag_ref,
                       x_hbm, wga_ref, wua_ref, wda_ref, wgb_ref, wub_ref, wdb_ref, lng_ref, lnb_ref,
                       out_hbm, xbuf, obuf, acc, wg_s, wu_s, wd_s, gsem, ssem, *, tm, n_tiles, n_items):
    i = pl.program_id(0)
    t = tile_ref[i]
    slot = t % 2
    flags = flag_ref[i]

    def start_gather(tile, s):
        for r in range(tm):
            tok = src_ref[tile * tm + r]
            pltpu.make_async_copy(x_hbm.at[pl.ds(tok * WIDE_TILES, WIDE_TILES)],
                                  xbuf.at[s, pl.ds(r * WIDE_TILES, WIDE_TILES)], gsem.at[s]).start()

    def piece(c):
        return xbuf[slot, pl.ds(c, tm, stride=WIDE_TILES), :]

    def activations():
        return jnp.concatenate([piece(c) for c in range(D_MODEL // LANES)], axis=-1)

    def start_scatter(tile, s):
        for r in range(tm):
            tok = src_ref[tile * tm + r]
            pltpu.make_async_copy(obuf.at[s, pl.ds(r, 1)], out_hbm.at[pl.ds(tok, 1)],
                                  ssem.at[s]).start(priority=r % 2)

    def for_slot(value, fn):
        for s in (0, 1):
            pl.when(value == s)(functools.partial(fn, s))

    def wait_gather(s):
        pltpu.make_async_copy(x_hbm.at[pl.ds(0, tm * WIDE_TILES)], xbuf.at[s], gsem.at[s]).wait()

    def wait_scatter(s):
        pltpu.make_async_copy(obuf.at[s], out_hbm.at[pl.ds(0, tm)], ssem.at[s]).wait()

    @pl.when(i == 0)
    def _():
        start_gather(0, 0)

    @pl.when((flags & FLAG_FIRST) != 0)
    def _():
        @pl.when(t + 1 < n_tiles)
        def _():
            for_slot(1 - slot, lambda s: start_gather(t + 1, s))
        wait_gather(slot)
        acc[...] = jnp.zeros_like(acc)

    @pl.when((flags & FLAG_NEW_A) != 0)
    def _():
        wg_s[0] = wga_ref[0, 0].astype(BF16)
        wu_s[0] = wua_ref[0, 0].astype(BF16)
        wd_s[0] = wda_ref[0, 0].astype(BF16)

    @pl.when((flags & FLAG_NEW_B) != 0)
    def _():
        wg_s[1] = wgb_ref[0, 0].astype(BF16)
        wu_s[1] = wub_ref[0, 0].astype(BF16)
        wd_s[1] = wdb_ref[0, 0].astype(BF16)

    @pl.when((flags & FLAG_ACTIVE) != 0)
    def _():
        xb = activations().astype(BF16)
        gates = piece(WIDE_TILES - 1)
        row = t * tm + lax.broadcasted_iota(jnp.int32, (tm, 1), 0)
        in_class = (row >= lo_ref[i]) & (row < hi_ref[i])
        gate_a = jnp.where(in_class, _gate_column(gates, ea_ref[i]), 0.0)
        gate_b = jnp.where(in_class, _gate_column(gates, eb_ref[i]), 0.0)
        hid_a = _expert_hidden(xb, wg_s[0], wu_s[0], gate_a)
        hid_b = _expert_hidden(xb, wg_s[1], wu_s[1], gate_b)
        acc[...] += (jnp.dot(hid_a, wd_s[0], preferred_element_type=F32)
                     + jnp.dot(hid_b, wd_s[1], preferred_element_type=F32))

    @pl.when((flags & FLAG_LAST) != 0)
    def _():
        @pl.when(t >= 2)
        def _():
            wait_scatter(slot)
        obuf[slot] = _layer_norm(DN_ALPHA * activations() + acc[...], lng_ref[...], lnb_ref[...])
        for_slot(slot, lambda s: start_scatter(t, s))

    @pl.when(i == n_items - 1)
    def _():
        wait_scatter(0)
        wait_scatter(1)


def _moe_schedule(cls, *, tm):
    m = cls.shape[0]
    n_tiles = m // tm
    n_items = n_tiles + N_CLASSES - 1
    i32 = jnp.int32
    src = jnp.argsort(cls, stable=True).astype(i32)
    counts = jnp.sum((cls[:, None] == jnp.arange(N_CLASSES, dtype=i32)[None, :]).astype(i32), axis=0)
    ends = jnp.cumsum(counts)
    starts = ends - counts
    def count(mask):
        return jnp.sum(mask.astype(i32), axis=-1)

    def pick(onehot, values):
        return jnp.sum(jnp.where(onehot, values[None, :], 0), axis=-1)

    tile_lo = jnp.arange(n_tiles, dtype=i32) * tm
    first_c = count(ends[None, :] <= tile_lo[:, None])
    last_c = count(starts[None, :] < (tile_lo + tm)[:, None]) - 1
    per_tile = last_c - first_c + 1
    item_end = jnp.cumsum(per_tile)
    idx = jnp.arange(n_items, dtype=i32)
    tile = jnp.minimum(count(item_end[None, :] <= idx[:, None]), n_tiles - 1)
    in_tile = tile[:, None] == jnp.arange(n_tiles, dtype=i32)[None, :]
    end = pick(in_tile, item_end)
    begin = end - pick(in_tile, per_tile)
    active = idx < item_end[-1]
    c = jnp.where(active, pick(in_tile, first_c) + idx - begin, last_c[n_tiles - 1])
    in_class = c[:, None] == jnp.arange(N_CLASSES, dtype=i32)[None, :]
    class_a = [g * EXPERTS_PER_GROUP + pa for g in range(N_GROUPS) for pa, _ in PAIRS]
    class_b = [g * EXPERTS_PER_GROUP + pb for g in range(N_GROUPS) for _, pb in PAIRS]
    ea = pick(in_class, jnp.asarray(class_a, dtype=i32))
    eb = pick(in_class, jnp.asarray(class_b, dtype=i32))
    lo = jnp.where(active, pick(in_class, starts), 0)
    hi = jnp.where(active, pick(in_class, ends), 0)
    new_a = jnp.concatenate([jnp.ones((1,), bool), ea[1:] != ea[:-1]])
    new_b = jnp.concatenate([jnp.ones((1,), bool), eb[1:] != eb[:-1]])
    flags = (active * FLAG_ACTIVE + (active & (idx == begin)) * FLAG_FIRST
             + (active & (idx == end - 1)) * FLAG_LAST + new_a * FLAG_NEW_A + new_b * FLAG_NEW_B)
    return src, tile, ea, eb, lo.astype(i32), hi.astype(i32), flags.astype(i32)


def _moe_sparse(x, gates, layer, wg, wu, wd, lng, lnb, *, tm):
    m = gates.shape[0]
    n_tiles = m // tm
    n_items = n_tiles + N_CLASSES - 1
    cls = gates[:, CLASS_LANE].astype(jnp.int32)
    tables = _moe_schedule(cls, tm=tm)
    kern = functools.partial(_moe_sparse_kernel, tm=tm, n_tiles=n_tiles, n_items=n_items)

    def w_spec(shape, which):
        return pl.BlockSpec((1, 1) + shape, lambda i, src, tile, ea, eb, lo, hi, fl: (layer, (ea, eb)[which][i], 0, 0))

    up, down = (D_MODEL, D_EXPERT), (D_EXPERT, D_MODEL)
    grid_spec = pltpu.PrefetchScalarGridSpec(
        num_scalar_prefetch=len(tables),
        grid=(n_items,),
        in_specs=[
            pl.BlockSpec(memory_space=pl.ANY),
            w_spec(up, 0), w_spec(up, 0), w_spec(down, 0), w_spec(up, 1), w_spec(up, 1), w_spec(down, 1),
            _const_spec(lng.shape), _const_spec(lnb.shape),
        ],
        out_specs=pl.BlockSpec(memory_space=pl.ANY),
        scratch_shapes=[
            pltpu.VMEM((2, tm * WIDE_TILES, LANES), F32), pltpu.VMEM((2, tm, D_MODEL), F32),
            pltpu.VMEM((tm, D_MODEL), F32),
            pltpu.VMEM((2,) + up, BF16), pltpu.VMEM((2,) + up, BF16), pltpu.VMEM((2,) + down, BF16),
            pltpu.SemaphoreType.DMA((2,)), pltpu.SemaphoreType.DMA((2,)),
        ],
    )
    return pl.pallas_call(
        kern,
        grid_spec=grid_spec,
        out_shape=jax.ShapeDtypeStruct((m, D_MODEL), F32),
        compiler_params=_params("arbitrary"),
        name="moe_sparse",
    )(*tables, x, wg, wu, wd, wg, wu, wd, lng, lnb)


def _row(v):
    return v.reshape(1, -1)


def _to_time_major(s):
    return jnp.transpose(s, (1, 0, 2)).reshape(1, -1, s.shape[-1])


def _from_time_major(s, steps):
    return jnp.transpose(s.reshape(steps, DEC_BATCH, s.shape[-1]), (1, 0, 2))


def _kv_lane_view(cache):
    d, n = cache.shape[:2]
    halves = MEM_HEAD_DIM // LANES
    v = cache.reshape(d, n, N_MEM, MEM_HEADS, halves, LANES)
    return jnp.transpose(v, (0, 1, 2, 4, 3, 5)).reshape(d, n, KV_ROWS, LANES)


def _kv_from_lane_view(view):
    d, n = view.shape[:2]
    halves = MEM_HEAD_DIM // LANES
    v = view.reshape(d, n, N_MEM, halves, MEM_HEADS, LANES)
    return jnp.transpose(v, (0, 1, 2, 4, 3, 5)).reshape(d, n, N_MEM, MEM_HEADS, MEM_HEAD_DIM)


def kernel(x_prompt, x_sample, cache_mem_k, cache_mem_v, state_pool, state_conv_c, state_conv_d, mem_prompt,
           w_in_ab, ln_v_g, ln_v_b, w_spatial, b_spatial, w_pool, pool_scale, w_out_ab,
           w_in_cd, conv_c_w, conv_c_b, ln_c_g, ln_c_b, conv_d_w, w_out_cd,
           w_q, w_k, w_v, w_o, w_group, b_group, w_router, b_router, w_gate, w_up, w_down, ln_g, ln_b):
    mem_flat = mem_prompt.reshape(BATCH * N_MEM, D_MODEL)
    cache_k = _kv_lane_view(cache_mem_k)
    cache_v = _kv_lane_view(cache_mem_v)
    kp, vp, kp_view, vp_view = _kv_proj(mem_flat, w_k.astype(BF16), w_v.astype(BF16))

    w_in_ab, w_out_ab, w_pool = w_in_ab.astype(BF16), w_out_ab.astype(BF16), w_pool.astype(BF16)
    w_in_cd, w_out_cd = w_in_cd.astype(BF16), w_out_cd.astype(BF16)
    w_q, w_o = w_q.astype(BF16), w_o.astype(BF16)
    pad = ROUTER_LANES - N_GROUPS - N_EXPERTS
    w_route = jnp.pad(jnp.concatenate([w_group, w_router], axis=-1), ((0, 0), (0, 0), (0, pad)))
    b_route = jnp.pad(jnp.concatenate([b_group, b_router], axis=-1), ((0, 0), (0, pad)))[:, None, :]

    prompt = _Slab(x_prompt.reshape(PROMPT_ROWS, D_MODEL), 0, PROMPT_ROWS, 0, None)
    sample = _Slab(_to_time_major(x_sample)[0], 0, SAMPLE_ROWS, PROMPT_ROWS, None)

    def both(h):
        return (_Slab(h, 0, PROMPT_ROWS, 0, None), _Slab(h, PROMPT_ROWS, SAMPLE_ROWS, PROMPT_ROWS, None))

    pool_p, pool_s, chunk_v_s = [], [], []
    conv_c_p, conv_c_s, conv_d_p, conv_d_s = [], [], [], []

    for l in range(DEPTH):
        i = l // 2
        lng0, lnb0 = _row(ln_g[l, 0]), _row(ln_b[l, 0])
        if l % 2 == 0:
            bias = jnp.repeat(b_spatial[i].T, A_HEAD_DIM, axis=1)
            small = jnp.repeat(
                jnp.transpose(w_spatial[i][:, :DEC_SEQ, :DEC_SEQ], (1, 2, 0)).reshape(DEC_SEQ * DEC_SEQ, A_HEADS),
                A_HEAD_DIM, axis=1)
            common = (_Layer(w_in_ab, i), _row(ln_v_g[i]), _row(ln_v_b[i]))
            tail = (_Layer(w_pool, i), _row(pool_scale[i]), _Layer(w_out_ab, i), lng0, lnb0)
            h, hist_p = _mixer_ab(prompt, jnp.zeros((BATCH, POOL_HIST, D_B), F32), *common, w_spatial[i], bias,
                                  *tail, seqs=1, tt=512, pos0=0, with_v=False)
            h, v_rows, hist_s = _mixer_ab(sample._replace(dst=(h,)), _to_time_major(state_pool[i]), *common, small,
                                          bias[:DEC_SEQ], *tail, seqs=DEC_BATCH, tt=DEC_SEQ, pos0=PAST_LEN,
                                          with_v=True)
            pool_p.append(hist_p)
            pool_s.append(_from_time_major(hist_s, POOL_HIST))
            chunk_v_s.append(_from_time_major(v_rows, DEC_SEQ))
        else:
            wts = (_Layer(w_in_cd, i), conv_c_w[i], _row(conv_c_b[i]), _row(ln_c_g[i]), _row(ln_c_b[i]),
                   conv_d_w[i], _Layer(w_out_cd, i), lng0, lnb0)
            h, hc_p, hd_p = _mixer_cd(prompt, jnp.zeros((BATCH, CONV_C - 1, D_C), F32),
                                      jnp.zeros((BATCH, CONV_D - 1, D_D), F32), *wts, seqs=1, tt=512)
            h, hc_s, hd_s = _mixer_cd(sample._replace(dst=(h,)), _to_time_major(state_conv_c[i]),
                                      _to_time_major(state_conv_d[i]), *wts, seqs=DEC_BATCH, tt=DEC_SEQ)
            conv_c_p.append(hc_p)
            conv_d_p.append(hd_p)
            conv_c_s.append(_from_time_major(hc_s, CONV_C - 1))
            conv_d_s.append(_from_time_major(hd_s, CONV_D - 1))
        prompt, sample = both(h)

        att_w = (_Layer(w_q, l), _Layer(w_o, l), _Layer(w_route, l), _Layer(b_route, l),
                 _row(ln_g[l, 1]), _row(ln_b[l, 1]))
        wide_and_gates = _attn_prompt(prompt, kp, vp, l, *att_w, tq=1024)
        h_wide, gates = _attn_sample(sample._replace(dst=tuple(wide_and_gates)), cache_k, cache_v, l, *att_w,
                                     seq_blk=8)

        h = _moe_sparse(h_wide, gates, l, w_gate, w_up, w_down, _row(ln_g[l, 2]), _row(ln_b[l, 2]), tm=256)
        prompt, sample = both(h)

    y_prompt = h[:PROMPT_ROWS].reshape(BATCH, SEQ, D_MODEL)
    y_sample = _from_time_major(h[PROMPT_ROWS:], DEC_SEQ)
    return (y_prompt, y_sample, _kv_from_lane_view(kp_view), _kv_from_lane_view(vp_view), jnp.stack(pool_p),
            jnp.stack(conv_c_p), jnp.stack(conv_d_p), jnp.stack(chunk_v_s), jnp.stack(pool_s),
            jnp.stack(conv_c_s), jnp.stack(conv_d_s))
```

```python
import functools
from typing import Callable, NamedTuple, Optional

import jax
import jax.numpy as jnp
from jax import lax
from jax.experimental import pallas as pl
from jax.experimental.pallas import tpu as pltpu

D_MODEL = 1024
BATCH = 8
SEQ = 2048
DEPTH = 4
DEC_BATCH = 128
DEC_SEQ = 4
PAST_LEN = 16384

CHUNK = 128
A_HEADS = 4
D_A = D_MODEL // 2
A_HEAD_DIM = D_A // A_HEADS
POOL_WINDOWS = (2, 4, 8, 16)
B_GROUPS = len(POOL_WINDOWS)
D_B = D_MODEL // 2
B_GROUP_DIM = D_B // B_GROUPS
POOL_HIST = max(POOL_WINDOWS) - 1
D_C = D_MODEL // 2
CONV_C = 31
D_D = D_MODEL // 2
CONV_D = 3
N_MEM = 256
MEM_HEADS = 4
MEM_HEAD_DIM = D_MODEL // MEM_HEADS
N_GROUPS = 4
EXPERTS_PER_GROUP = 4
N_EXPERTS = N_GROUPS * EXPERTS_PER_GROUP
D_EXPERT = 512
DN_ALPHA = (2 * DEPTH) ** 0.25
LN_EPS = 1e-5
PROMPT_ROWS = BATCH * SEQ
SAMPLE_ROWS = DEC_BATCH * DEC_SEQ
TOTAL_ROWS = PROMPT_ROWS + SAMPLE_ROWS

LANES = 128
SUBLANES = 8
POOL_HIST_PAD = 16
CONV_C_HIST_PAD = 32
CONV_D_HIST_PAD = 8
ROUTER_LANES = 128
ROUTER_MASKED = -1e30
CLASS_LANE = N_GROUPS + N_EXPERTS
PAIRS = ((0, 1), (0, 2), (1, 2), (1, 3), (0, 3), (2, 3))
N_CLASSES = N_GROUPS * len(PAIRS)
D_WIDE = D_MODEL + ROUTER_LANES
WIDE_TILES = D_WIDE // LANES
KV_ROWS = N_MEM * MEM_HEADS * MEM_HEAD_DIM // LANES
ATT_ROWS = KV_ROWS // N_MEM * DEC_SEQ
VMEM_LIMIT_BYTES = 52 * 1024 * 1024

F32 = jnp.float32
BF16 = jnp.bfloat16


def _layer_norm(x, g, b):
    mu = jnp.mean(x, axis=-1, keepdims=True)
    xc = x - mu
    var = jnp.mean(xc * xc, axis=-1, keepdims=True)
    return xc * lax.rsqrt(var + LN_EPS) * g + b


def _dot(a, b):
    return jnp.dot(a.astype(BF16), b.astype(BF16), preferred_element_type=F32)


def _params(*semantics):
    return pltpu.CompilerParams(dimension_semantics=semantics, vmem_limit_bytes=VMEM_LIMIT_BYTES)


def _const_spec(shape):
    nd = len(shape)
    return pl.BlockSpec(shape, lambda *_: (0,) * nd)


class _Layer(NamedTuple):
    stack: jax.Array
    layer: int


def _param_spec(p):
    if isinstance(p, _Layer):
        nd = p.stack.ndim
        return pl.BlockSpec((1,) + p.stack.shape[1:], lambda *_: (p.layer,) + (0,) * (nd - 1))
    return _const_spec(p.shape)


def _param_arg(p):
    return p.stack if isinstance(p, _Layer) else p


class _Slab(NamedTuple):
    array: jax.Array
    row0: int
    n_rows: int
    out_row0: int
    dst: Optional[tuple]


class _SlabIO(NamedTuple):
    in_spec: pl.BlockSpec
    out_specs: list
    out_shapes: list
    alias_spec: list
    alias_arg: tuple
    aliases: dict
    adapt: Callable


def _slab_io(src, rows, outs, block_index, *, n_inputs):
    assert src.row0 % rows == 0 and src.out_row0 % rows == 0 and src.n_rows % rows == 0
    in0, out0 = src.row0 // rows, src.out_row0 // rows
    in_spec = pl.BlockSpec((rows, src.array.shape[1]), lambda *g: (in0 + block_index(*g), 0))
    out_specs, out_shapes = [], []
    for width, row_major_tiles in outs:
        pieces, lanes = (width // LANES, LANES) if row_major_tiles else (1, width)
        out_specs.append(pl.BlockSpec((rows * pieces, lanes), lambda *g: (out0 + block_index(*g), 0)))
        out_shapes.append(jax.ShapeDtypeStruct((TOTAL_ROWS * pieces, lanes), F32))
    if src.dst is None:
        return _SlabIO(in_spec, out_specs, out_shapes, [], (), {}, lambda kernel: kernel)
    n_dst = len(outs)
    assert len(src.dst) == n_dst

    def adapt(kernel):
        return lambda *refs: kernel(*refs[:n_inputs], *refs[n_inputs + n_dst:])

    return _SlabIO(in_spec, out_specs, out_shapes, [pl.BlockSpec(memory_space=pl.ANY)] * n_dst, tuple(src.dst),
                   {n_inputs + j: j for j in range(n_dst)}, adapt)


def _mixer_ab_kernel(x_ref, hist_ref, w_in_ref, lnv_g_ref, lnv_b_ref, ws_ref, bs_ref, wpool_ref,
                     pscale_ref, w_out_ref, lng_ref, lnb_ref, *refs, seqs, tt, pos0, with_v):
    if with_v:
        h_out_ref, v_out_ref, hist_out_ref, zbuf = refs
    else:
        h_out_ref, hist_out_ref, zbuf = refs
    t_idx = pl.program_id(1)
    rows = tt * seqs
    hp = POOL_HIST_PAD

    @pl.when(t_idx == 0)
    def _():
        zbuf[pl.ds(0, (hp - POOL_HIST) * seqs), :] = jnp.zeros(((hp - POOL_HIST) * seqs, D_B), F32)
        zbuf[pl.ds((hp - POOL_HIST) * seqs, POOL_HIST * seqs), :] = hist_ref[0]

    x = x_ref[...]
    h = _dot(x, w_in_ref[0])
    ua = jax.nn.gelu(h[:, :2 * D_A])
    u = ua[:, :D_A]
    v = _layer_norm(ua[:, D_A:], lnv_g_ref[...], lnv_b_ref[...])
    z = h[:, 2 * D_A:]
    if with_v:
        v_out_ref[0] = v
    zbuf[pl.ds(hp * seqs, rows), :] = z

    if seqs == 1:
        tri = (lax.broadcasted_iota(jnp.int32, (CHUNK, CHUNK), 0)
               >= lax.broadcasted_iota(jnp.int32, (CHUNK, CHUNK), 1))
        w_heads = [jnp.where(tri, ws_ref[hh], 0.0).astype(BF16) for hh in range(A_HEADS)]
        vb = v.astype(BF16)
        chunks = []
        for c in range(tt // CHUNK):
            heads = []
            for hh in range(A_HEADS):
                vc = vb[c * CHUNK:(c + 1) * CHUNK, hh * A_HEAD_DIM:(hh + 1) * A_HEAD_DIM]
                heads.append(jnp.dot(w_heads[hh], vc, preferred_element_type=F32))
            chunks.append(jnp.concatenate(heads, axis=-1) + bs_ref[...])
        mixed = jnp.concatenate(chunks, axis=0)
    else:
        parts = []
        for t in range(tt):
            acc = bs_ref[t:t + 1, :]
            for s in range(t + 1):
                acc = acc + ws_ref[t * tt + s:t * tt + s + 1, :] * v[s * seqs:(s + 1) * seqs, :]
            parts.append(acc)
        mixed = jnp.concatenate(parts, axis=0)
    a_out = u * mixed

    run = zbuf[...]
    width = 1
    outs = []
    for gi, w in enumerate(POOL_WINDOWS):
        lo, hi = gi * B_GROUP_DIM, (gi + 1) * B_GROUP_DIM
        while width < w:
            run = run + pltpu.roll(run, width * seqs, axis=0)
            width *= 2
        assert width == w
        acc = run[hp * seqs:, :B_GROUP_DIM]
        run = run[:, B_GROUP_DIM:]
        if pos0 + 1 >= w:
            cnt = float(w)
        else:
            assert seqs == 1
            pos = pos0 + t_idx * tt + lax.broadcasted_iota(jnp.int32, (rows, 1), 0)
            cnt = jnp.minimum(pos + 1, w).astype(F32)
        pooled = acc / cnt - z[:, lo:hi]
        outs.append(_dot(pooled, wpool_ref[0, gi]))
    b_out = jnp.concatenate(outs, axis=-1) * pscale_ref[...]

    y = _dot(jnp.concatenate([a_out, b_out], axis=-1), w_out_ref[0])
    h_out_ref[...] = _layer_norm(DN_ALPHA * x + y, lng_ref[...], lnb_ref[...])

    hist_out_ref[0] = zbuf[pl.ds((tt + hp - POOL_HIST) * seqs, POOL_HIST * seqs), :]
    zbuf[pl.ds(0, hp * seqs), :] = zbuf[pl.ds(tt * seqs, hp * seqs), :]


def _mixer_ab(src, hist, w_in, lnv_g, lnv_b, ws, bs, wpool, pscale, w_out, lng, lnb, *, seqs, tt, pos0, with_v):
    nb = hist.shape[0]
    rows = tt * seqs
    n_t = src.n_rows // (nb * rows)
    kern = functools.partial(_mixer_ab_kernel, seqs=seqs, tt=tt, pos0=pos0, with_v=with_v)
    weights = (w_in, lnv_g, lnv_b, ws, bs, wpool, pscale, w_out, lng, lnb)
    io = _slab_io(src, rows, [(D_MODEL, False)], lambda n, t: n * n_t + t, n_inputs=2 + len(weights))
    v_spec = pl.BlockSpec((1, rows, D_A), lambda n, t: (n, t, 0))
    hist_spec = pl.BlockSpec((1, POOL_HIST * seqs, D_B), lambda n, t: (n, 0, 0))
    v_shape = jax.ShapeDtypeStruct((nb, n_t * rows, D_A), F32)
    hist_shape = jax.ShapeDtypeStruct((nb, POOL_HIST * seqs, D_B), F32)
    return pl.pallas_call(
        io.adapt(kern),
        grid=(nb, n_t),
        in_specs=[io.in_spec, hist_spec] + [_param_spec(w) for w in weights] + io.alias_spec,
        out_specs=io.out_specs + ([v_spec, hist_spec] if with_v else [hist_spec]),
        out_shape=io.out_shapes + ([v_shape, hist_shape] if with_v else [hist_shape]),
        input_output_aliases=io.aliases,
        scratch_shapes=[pltpu.VMEM(((POOL_HIST_PAD + tt) * seqs, D_B), F32)],
        compiler_params=_params("parallel", "arbitrary"),
        name="mixer_ab",
    )(src.array, hist, *map(_param_arg, weights), *io.alias_arg)


def _shift_rows(taps, hist_pad):
    base = hist_pad - (taps - 1)
    return max((base + k) // SUBLANES * SUBLANES for k in range(taps) if (base + k) % SUBLANES)


def _dwconv(buf, w_ref, out, shifted, *, taps, hist_pad, seqs, rows, chunk):
    base = (hist_pad - (taps - 1)) * seqs
    if shifted is not None:
        span = rows + _shift_rows(taps, hist_pad)
        for b in range(1, SUBLANES):
            shifted[b - 1] = buf[pl.ds(b, span), :]

    def tap(k, r0, n, lanes):
        off = base + k * seqs
        phase = off % SUBLANES
        if shifted is None or phase == 0:
            return buf[pl.ds(off + r0, n), lanes]
        return shifted[phase - 1, pl.ds(off - phase + r0, n), lanes]

    if shifted is None:
        def full_width(i, carry):
            r0 = i * chunk if isinstance(i, int) else pl.multiple_of(i * chunk, chunk)
            acc = w_ref[0:1, :] * tap(0, r0, chunk, slice(None))
            for k in range(1, taps):
                acc = acc + w_ref[k:k + 1, :] * tap(k, r0, chunk, slice(None))
            out[pl.ds(r0, chunk), :] = acc
            return carry

        if seqs % SUBLANES == 0:
            lax.fori_loop(0, rows // chunk, full_width, 0)
        else:
            for i in range(rows // chunk):
                full_width(i, 0)
        return

    for lt in range(buf.shape[1] // LANES):
        lanes = slice(lt * LANES, (lt + 1) * LANES)
        wk = [jnp.broadcast_to(w_ref[k:k + 1, lanes], (SUBLANES, LANES)) for k in range(taps)]
        for rg in range(0, rows, SUBLANES):
            acc = wk[0] * tap(0, rg, SUBLANES, lanes)
            for k in range(1, taps):
                acc = acc + wk[k] * tap(k, rg, SUBLANES, lanes)
            out[pl.ds(rg, SUBLANES), lanes] = acc


def _mixer_cd_kernel(x_ref, hist_c_ref, hist_d_ref, w_in_ref, ccw_ref, ccb_ref, lncg_ref, lncb_ref,
                     cdw_ref, w_out_ref, lng_ref, lnb_ref,
                     h_out_ref, hist_c_out_ref, hist_d_out_ref, cbuf, dbuf, cc_scr, cd_scr, *maybe_shifted,
                     seqs, tt):
    shifted = maybe_shifted[0] if maybe_shifted else None
    t_idx = pl.program_id(1)
    rows = tt * seqs
    hc, hd = CONV_C_HIST_PAD, CONV_D_HIST_PAD
    nc, nd = CONV_C - 1, CONV_D - 1

    @pl.when(t_idx == 0)
    def _():
        cbuf[pl.ds((hc - nc) * seqs, nc * seqs), :] = hist_c_ref[0]
        dbuf[pl.ds((hd - nd) * seqs, nd * seqs), :] = hist_d_ref[0]

    x = x_ref[...]
    h = _dot(x, w_in_ref[0])
    glu = h[:, :D_C] * jax.nn.sigmoid(h[:, D_C:2 * D_C])
    o = 2 * D_C
    gate_b = h[:, o:o + D_D]
    gx = h[:, o + D_D:o + 2 * D_D] * h[:, o + 2 * D_D:]
    cbuf[pl.ds(hc * seqs, rows), :] = glu
    dbuf[pl.ds(hd * seqs, rows), :] = gx

    chunk = min(rows, 32)
    _dwconv(cbuf, ccw_ref, cc_scr, shifted, taps=CONV_C, hist_pad=hc, seqs=seqs, rows=rows, chunk=chunk)
    _dwconv(dbuf, cdw_ref, cd_scr, None, taps=CONV_D, hist_pad=hd, seqs=seqs, rows=rows, chunk=chunk)

    c_out = jax.nn.silu(_layer_norm(cc_scr[...] + ccb_ref[...], lncg_ref[...], lncb_ref[...]))
    d_out = gate_b * cd_scr[...]
    y = _dot(jnp.concatenate([c_out, d_out], axis=-1), w_out_ref[0])
    h_out_ref[...] = _layer_norm(DN_ALPHA * x + y, lng_ref[...], lnb_ref[...])

    hist_c_out_ref[0] = cbuf[pl.ds((tt + hc - nc) * seqs, nc * seqs), :]
    hist_d_out_ref[0] = dbuf[pl.ds((tt + hd - nd) * seqs, nd * seqs), :]
    cbuf[pl.ds(0, hc * seqs), :] = cbuf[pl.ds(tt * seqs, hc * seqs), :]
    dbuf[pl.ds(0, hd * seqs), :] = dbuf[pl.ds(tt * seqs, hd * seqs), :]


def _mixer_cd(src, hist_c, hist_d, w_in, ccw, ccb, lncg, lncb, cdw, w_out, lng, lnb, *, seqs, tt):
    nb = hist_c.shape[0]
    rows = tt * seqs
    n_t = src.n_rows // (nb * rows)
    nc, nd = CONV_C - 1, CONV_D - 1
    kern = functools.partial(_mixer_cd_kernel, seqs=seqs, tt=tt)
    weights = (w_in, ccw, ccb, lncg, lncb, cdw, w_out, lng, lnb)
    io = _slab_io(src, rows, [(D_MODEL, False)], lambda n, t: n * n_t + t, n_inputs=3 + len(weights))
    hist_c_spec = pl.BlockSpec((1, nc * seqs, D_C), lambda n, t: (n, 0, 0))
    hist_d_spec = pl.BlockSpec((1, nd * seqs, D_D), lambda n, t: (n, 0, 0))
    return pl.pallas_call(
        io.adapt(kern),
        grid=(nb, n_t),
        in_specs=[io.in_spec, hist_c_spec, hist_d_spec] + [_param_spec(w) for w in weights] + io.alias_spec,
        out_specs=io.out_specs + [hist_c_spec, hist_d_spec],
        out_shape=io.out_shapes + [
            jax.ShapeDtypeStruct((nb, nc * seqs, D_C), F32),
            jax.ShapeDtypeStruct((nb, nd * seqs, D_D), F32),
        ],
        input_output_aliases=io.aliases,
        scratch_shapes=[
            pltpu.VMEM(((CONV_C_HIST_PAD + tt) * seqs, D_C), F32),
            pltpu.VMEM(((CONV_D_HIST_PAD + tt) * seqs, D_D), F32),
            pltpu.VMEM((rows, D_C), F32),
            pltpu.VMEM((rows, D_D), F32),
        ] + ([pltpu.VMEM((SUBLANES - 1, rows + _shift_rows(CONV_C, CONV_C_HIST_PAD), D_C), F32)]
             if seqs % SUBLANES else []),
        compiler_params=_params("parallel", "arbitrary"),
        name="mixer_cd",
    )(src.array, hist_c, hist_d, *map(_param_arg, weights), *io.alias_arg)


KV_SEQS_PER_STEP = 2


def _kv_proj_kernel(mem_ref, wk_ref, wv_ref, k_ref, v_ref, kview_ref, vview_ref):
    m = mem_ref[...].astype(BF16)
    halves = MEM_HEAD_DIM // LANES
    for w_ref, o_ref, view_ref in ((wk_ref, k_ref, kview_ref), (wv_ref, v_ref, vview_ref)):
        y = jnp.dot(m, w_ref[0], preferred_element_type=F32)
        o_ref[0] = y.astype(BF16)
        for s in range(KV_SEQS_PER_STEP):
            for hh in range(MEM_HEADS):
                for lt in range(halves):
                    col = hh * MEM_HEAD_DIM + lt * LANES
                    view_ref[0, s, pl.ds(lt * MEM_HEADS + hh, N_MEM, stride=halves * MEM_HEADS), :] = (
                        y[s * N_MEM:(s + 1) * N_MEM, col:col + LANES])


def _kv_proj(mem, wk, wv):
    rows = jax.ShapeDtypeStruct((DEPTH, BATCH * N_MEM, D_MODEL), BF16)
    view = jax.ShapeDtypeStruct((DEPTH, BATCH, KV_ROWS, LANES), F32)
    w_spec = pl.BlockSpec((1, D_MODEL, D_MODEL), lambda l, n: (l, 0, 0))
    per = KV_SEQS_PER_STEP
    return pl.pallas_call(
        _kv_proj_kernel,
        grid=(DEPTH, BATCH // per),
        in_specs=[pl.BlockSpec((per * N_MEM, D_MODEL), lambda l, n: (n, 0)), w_spec, w_spec],
        out_specs=[pl.BlockSpec((1, per * N_MEM, D_MODEL), lambda l, n: (l, n, 0))] * 2
        + [pl.BlockSpec((1, per, KV_ROWS, LANES), lambda l, n: (l, n, 0, 0))] * 2,
        out_shape=[rows, rows, view, view],
        compiler_params=_params("parallel", "arbitrary"),
        name="kv_proj",
    )(mem, wk, wv)


def _attend(q, k, v):
    scale = MEM_HEAD_DIM ** -0.5
    qb = q.astype(BF16)
    outs = []
    for hh in range(MEM_HEADS):
        lo, hi = hh * MEM_HEAD_DIM, (hh + 1) * MEM_HEAD_DIM
        s = lax.dot_general(qb[:, lo:hi], k[:, lo:hi], (((1,), (1,)), ((), ())),
                            preferred_element_type=F32) * scale
        s = s - jnp.max(s, axis=-1, keepdims=True)
        p = jnp.exp(s)
        p = p / jnp.sum(p, axis=-1, keepdims=True)
        outs.append(jnp.dot(p.astype(BF16), v[:, lo:hi], preferred_element_type=F32))
    return jnp.concatenate(outs, axis=-1)


def _attn_prompt_kernel(x_ref, k_ref, v_ref, wq_ref, wo_ref, wr_ref, br_ref, lng_ref, lnb_ref, o_ref, g_ref):
    x = x_ref[...]
    q = _dot(x, wq_ref[0])
    o = _attend(q, k_ref[0], v_ref[0])
    y = _dot(o, wo_ref[0])
    h2 = _layer_norm(DN_ALPHA * x + y, lng_ref[...], lnb_ref[...])
    _store_wide_rows(o_ref, g_ref, h2, _route(_router_logits(h2, wr_ref[0]) + br_ref[0]))


def _store_wide_rows(o_ref, g_ref, h2, gates):
    rows = h2.shape[0]
    for c in range(WIDE_TILES):
        piece = gates if c == WIDE_TILES - 1 else h2[:, c * LANES:(c + 1) * LANES]
        o_ref[pl.ds(c, rows, stride=WIDE_TILES), :] = piece
    g_ref[...] = gates


ATTN_OUTS = [(D_WIDE, True), (ROUTER_LANES, False)]


def _attn_prompt(src, k, v, layer, wq, wo, wr, br, lng, lnb, *, tq):
    n_t = src.n_rows // (BATCH * tq)
    weights = (wq, wo, wr, br, lng, lnb)
    io = _slab_io(src, tq, ATTN_OUTS, lambda n, i: n * n_t + i, n_inputs=3 + len(weights))
    kv_spec = pl.BlockSpec((1, N_MEM, D_MODEL), lambda n, i: (layer, n, 0))
    return pl.pallas_call(
        io.adapt(_attn_prompt_kernel),
        grid=(BATCH, n_t),
        in_specs=[io.in_spec, kv_spec, kv_spec] + [_param_spec(w) for w in weights] + io.alias_spec,
        out_specs=io.out_specs,
        out_shape=io.out_shapes,
        input_output_aliases=io.aliases,
        compiler_params=_params("parallel", "arbitrary"),
        name="attn_prompt",
    )(src.array, k, v, *map(_param_arg, weights), *io.alias_arg)


def _att_row(half, head, t):
    return (half * MEM_HEADS + head) * DEC_SEQ + t


def _attn_sample_kernel(x_ref, k_ref, v_ref, wq_ref, wo_ref, wr_ref, br_ref, lng_ref, lnb_ref, o_ref, g_ref,
                        q_scr, qp_scr, op_scr, *, seq_blk, n_blk):
    i = pl.program_id(0)
    halves = MEM_HEAD_DIM // LANES
    blocks = [(t, hh, lt) for t in range(DEC_SEQ) for hh in range(MEM_HEADS) for lt in range(halves)]

    @pl.when(i == 0)
    def _():
        q_scr[...] = _dot(x_ref[...], wq_ref[0])
        for t, hh, lt in blocks:
            col = hh * MEM_HEAD_DIM + lt * LANES
            qp_scr[pl.ds(_att_row(lt, hh, t), DEC_BATCH, stride=ATT_ROWS), :] = (
                q_scr[t * DEC_BATCH:(t + 1) * DEC_BATCH, col:col + LANES])

    half_rows = ATT_ROWS // halves
    col = lax.broadcasted_iota(jnp.int32, (half_rows, KV_ROWS), 1)
    row_head = lax.shift_right_logical(lax.broadcasted_iota(jnp.int32, (half_rows, KV_ROWS), 0),
                                       DEC_SEQ.bit_length() - 1)
    col_cls = col & (halves * MEM_HEADS - 1)
    match0 = col_cls == row_head
    match1 = col_cls == row_head + MEM_HEADS
    scale = MEM_HEAD_DIM ** -0.5

    def body(j, carry):
        n = i * seq_blk + j
        r0 = pl.multiple_of(n * ATT_ROWS, ATT_ROWS)
        qp = qp_scr[pl.ds(r0, ATT_ROWS), :].astype(BF16)
        s = lax.dot_general(qp, k_ref[0, j].astype(BF16), (((1,), (1,)), ((), ())),
                            preferred_element_type=F32)
        part = jnp.where(match0, s[:half_rows], 0.0) + pltpu.roll(
            jnp.where(match1, s[half_rows:], 0.0), KV_ROWS - MEM_HEADS, axis=1)
        sv = jnp.where(match0, part * scale, ROUTER_MASKED)
        e = jnp.exp(sv - jnp.max(sv, axis=-1, keepdims=True))
        p = e / jnp.sum(e, axis=-1, keepdims=True)
        pp = jnp.concatenate([p, pltpu.roll(p, MEM_HEADS, axis=1)], axis=0).astype(BF16)
        op_scr[pl.ds(r0, ATT_ROWS), :] = jnp.dot(pp, v_ref[0, j].astype(BF16), preferred_element_type=F32)
        return carry

    lax.fori_loop(0, seq_blk, body, 0, unroll=True)

    @pl.when(i == n_blk - 1)
    def _():
        for t, hh, lt in blocks:
            col = hh * MEM_HEAD_DIM + lt * LANES
            q_scr[t * DEC_BATCH:(t + 1) * DEC_BATCH, col:col + LANES] = (
                op_scr[pl.ds(_att_row(lt, hh, t), DEC_BATCH, stride=ATT_ROWS), :])
        x = x_ref[...]
        y = _dot(q_scr[...], wo_ref[0])
        h2 = _layer_norm(DN_ALPHA * x + y, lng_ref[...], lnb_ref[...])
        _store_wide_rows(o_ref, g_ref, h2, _route(_router_logits(h2, wr_ref[0]) + br_ref[0]))


def _attn_sample(src, k, v, layer, wq, wo, wr, br, lng, lnb, *, seq_blk):
    m = src.n_rows
    n_blk = DEC_BATCH // seq_blk
    kern = functools.partial(_attn_sample_kernel, seq_blk=seq_blk, n_blk=n_blk)
    weights = (wq, wo, wr, br, lng, lnb)
    io = _slab_io(src, m, ATTN_OUTS, lambda i: 0, n_inputs=3 + len(weights))
    kv_spec = pl.BlockSpec((1, seq_blk, KV_ROWS, LANES), lambda i: (layer, i, 0, 0))
    return pl.pallas_call(
        io.adapt(kern),
        grid=(n_blk,),
        in_specs=[io.in_spec, kv_spec, kv_spec] + [_param_spec(w) for w in weights] + io.alias_spec,
        out_specs=io.out_specs,
        out_shape=io.out_shapes,
        input_output_aliases=io.aliases,
        scratch_shapes=[pltpu.VMEM((m, D_MODEL), F32),
                        pltpu.VMEM((DEC_BATCH * ATT_ROWS, LANES), F32),
                        pltpu.VMEM((DEC_BATCH * ATT_ROWS, LANES), F32)],
        compiler_params=_params("arbitrary"),
        name="attn_sample",
    )(src.array, k, v, *map(_param_arg, weights), *io.alias_arg)


def _split_bf16(x):
    hi = x.astype(BF16)
    return hi, (x - hi.astype(F32)).astype(BF16)


def _router_logits(h, w):
    h_hi, h_lo = _split_bf16(h)
    w_hi, w_lo = _split_bf16(w)
    a = jnp.dot(h_hi, jnp.concatenate([w_hi, w_lo], axis=-1), preferred_element_type=F32)
    b = jnp.dot(h_lo, w_hi, preferred_element_type=F32)
    return a[:, :ROUTER_LANES] + a[:, ROUTER_LANES:] + b


def _route(logits):
    lane = lax.broadcasted_iota(jnp.int32, logits.shape, 1)
    lane_f = lane.astype(F32)
    neg = ROUTER_MASKED
    is_grp = lane < N_GROUPS
    gl = jnp.where(is_grp, logits, neg)
    gmax = jnp.max(gl, axis=-1, keepdims=True)
    gsel = jnp.min(jnp.where(gl == gmax, lane_f, float(ROUTER_LANES)), axis=-1, keepdims=True)
    g_w = 1.0 / jnp.sum(jnp.exp(gl - gmax), axis=-1, keepdims=True)
    e_idx = lane - N_GROUPS
    e_grp = lax.shift_right_arithmetic(e_idx, 2).astype(F32)
    in_grp = (e_idx >= 0) & (e_idx < N_EXPERTS) & (e_grp == gsel)
    el = jnp.where(in_grp, logits, neg)
    m1 = jnp.max(el, axis=-1, keepdims=True)
    i1 = jnp.min(jnp.where(el == m1, lane_f, float(ROUTER_LANES)), axis=-1, keepdims=True)
    el2 = jnp.where(lane_f == i1, neg, el)
    m2 = jnp.max(el2, axis=-1, keepdims=True)
    i2 = jnp.min(jnp.where(el2 == m2, lane_f, float(ROUTER_LANES)), axis=-1, keepdims=True)
    e2 = jnp.exp(m2 - m1)
    den = 1.0 + e2
    w1 = (1.0 / den) * g_w
    w2 = (e2 / den) * g_w
    first_lane = N_GROUPS + EXPERTS_PER_GROUP * gsel
    a = jnp.minimum(i1, i2) - first_lane
    b = jnp.maximum(i1, i2) - first_lane
    cls = gsel * len(PAIRS)
    for j, (pa, pb) in enumerate(PAIRS):
        cls = cls + jnp.where((a == pa) & (b == pb), float(j), 0.0)
    return (jnp.where(lane_f == i1, w1, 0.0) + jnp.where(lane_f == i2, w2, 0.0)
            + jnp.where(lane == CLASS_LANE, cls, 0.0))


def _gate_column(gates, expert):
    lane = lax.broadcasted_iota(jnp.int32, gates.shape, 1)
    return jnp.sum(jnp.where(lane == expert + N_GROUPS, gates, 0.0), axis=-1, keepdims=True)


def _expert_hidden(xb, wg, wu, gate):
    hg = jnp.dot(xb, wg, preferred_element_type=F32)
    hu = jnp.dot(xb, wu, preferred_element_type=F32)
    return (jax.nn.silu(hg) * hu * gate).astype(BF16)


FLAG_ACTIVE, FLAG_FIRST, FLAG_LAST, FLAG_NEW_A, FLAG_NEW_B = 1, 2, 4, 8, 16


def _moe_sparse_kernel(src_ref, tile_ref, ea_ref, eb_ref, lo_ref, hi_ref, flag_ref,
                       x_hbm, wga_ref, wua_ref, wda_ref, wgb_ref, wub_ref, wdb_ref, lng_ref, lnb_ref,
                       out_hbm, xbuf, obuf, acc, wg_s, wu_s, wd_s, gsem, ssem, *, tm, n_tiles, n_items):
    i = pl.program_id(0)
    t = tile_ref[i]
    slot = t % 2
    flags = flag_ref[i]

    def start_gather(tile, s):
        for r in range(tm):
            tok = src_ref[tile * tm + r]
            pltpu.make_async_copy(x_hbm.at[pl.ds(tok * WIDE_TILES, WIDE_TILES)],
                                  xbuf.at[s, pl.ds(r * WIDE_TILES, WIDE_TILES)], gsem.at[s]).start()

    def piece(c):
        return xbuf[slot, pl.ds(c, tm, stride=WIDE_TILES), :]

    def activations():
        return jnp.concatenate([piece(c) for c in range(D_MODEL // LANES)], axis=-1)

    def start_scatter(tile, s):
        for r in range(tm):
            tok = src_ref[tile * tm + r]
            pltpu.make_async_copy(obuf.at[s, pl.ds(r, 1)], out_hbm.at[pl.ds(tok, 1)],
                                  ssem.at[s]).start(priority=r % 2)

    def for_slot(value, fn):
        for s in (0, 1):
            pl.when(value == s)(functools.partial(fn, s))

    def wait_gather(s):
        pltpu.make_async_copy(x_hbm.at[pl.ds(0, tm * WIDE_TILES)], xbuf.at[s], gsem.at[s]).wait()

    def wait_scatter(s):
        pltpu.make_async_copy(obuf.at[s], out_hbm.at[pl.ds(0, tm)], ssem.at[s]).wait()

    @pl.when(i == 0)
    def _():
        start_gather(0, 0)

    @pl.when((flags & FLAG_FIRST) != 0)
    def _():
        @pl.when(t + 1 < n_tiles)
        def _():
            for_slot(1 - slot, lambda s: start_gather(t + 1, s))
        wait_gather(slot)
        acc[...] = jnp.zeros_like(acc)

    @pl.when((flags & FLAG_NEW_A) != 0)
    def _():
        wg_s[0] = wga_ref[0, 0].astype(BF16)
        wu_s[0] = wua_ref[0, 0].astype(BF16)
        wd_s[0] = wda_ref[0, 0].astype(BF16)

    @pl.when((flags & FLAG_NEW_B) != 0)
    def _():
        wg_s[1] = wgb_ref[0, 0].astype(BF16)
        wu_s[1] = wub_ref[0, 0].astype(BF16)
        wd_s[1] = wdb_ref[0, 0].astype(BF16)

    @pl.when((flags & FLAG_ACTIVE) != 0)
    def _():
        xb = activations().astype(BF16)
        gates = piece(WIDE_TILES - 1)
        row = t * tm + lax.broadcasted_iota(jnp.int32, (tm, 1), 0)
        in_class = (row >= lo_ref[i]) & (row < hi_ref[i])
        gate_a = jnp.where(in_class, _gate_column(gates, ea_ref[i]), 0.0)
        gate_b = jnp.where(in_class, _gate_column(gates, eb_ref[i]), 0.0)
        hid_a = _expert_hidden(xb, wg_s[0], wu_s[0], gate_a)
        hid_b = _expert_hidden(xb, wg_s[1], wu_s[1], gate_b)
        acc[...] += (jnp.dot(hid_a, wd_s[0], preferred_element_type=F32)
                     + jnp.dot(hid_b, wd_s[1], preferred_element_type=F32))

    @pl.when((flags & FLAG_LAST) != 0)
    def _():
        @pl.when(t >= 2)
        def _():
            wait_scatter(slot)
        obuf[slot] = _layer_norm(DN_ALPHA * activations() + acc[...], lng_ref[...], lnb_ref[...])
        for_slot(slot, lambda s: start_scatter(t, s))

    @pl.when(i == n_items - 1)
    def _():
        wait_scatter(0)
        wait_scatter(1)


def _moe_schedule(cls, *, tm):
    m = cls.shape[0]
    n_tiles = m // tm
    n_items = n_tiles + N_CLASSES - 1
    i32 = jnp.int32
    src = jnp.argsort(cls, stable=True).astype(i32)
    counts = jnp.sum((cls[:, None] == jnp.arange(N_CLASSES, dtype=i32)[None, :]).astype(i32), axis=0)
    ends = jnp.cumsum(counts)
    starts = ends - counts
    def count(mask):
        return jnp.sum(mask.astype(i32), axis=-1)

    def pick(onehot, values):
        return jnp.sum(jnp.where(onehot, values[None, :], 0), axis=-1)

    tile_lo = jnp.arange(n_tiles, dtype=i32) * tm
    first_c = count(ends[None, :] <= tile_lo[:, None])
    last_c = count(starts[None, :] < (tile_lo + tm)[:, None]) - 1
    per_tile = last_c - first_c + 1
    item_end = jnp.cumsum(per_tile)
    idx = jnp.arange(n_items, dtype=i32)
    tile = jnp.minimum(count(item_end[None, :] <= idx[:, None]), n_tiles - 1)
    in_tile = tile[:, None] == jnp.arange(n_tiles, dtype=i32)[None, :]
    end = pick(in_tile, item_end)
    begin = end - pick(in_tile, per_tile)
    active = idx < item_end[-1]
    c = jnp.where(active, pick(in_tile, first_c) + idx - begin, last_c[n_tiles - 1])
    in_class = c[:, None] == jnp.arange(N_CLASSES, dtype=i32)[None, :]
    class_a = [g * EXPERTS_PER_GROUP + pa for g in range(N_GROUPS) for pa, _ in PAIRS]
    class_b = [g * EXPERTS_PER_GROUP + pb for g in range(N_GROUPS) for _, pb in PAIRS]
    ea = pick(in_class, jnp.asarray(class_a, dtype=i32))
    eb = pick(in_class, jnp.asarray(class_b, dtype=i32))
    lo = jnp.where(active, pick(in_class, starts), 0)
    hi = jnp.where(active, pick(in_class, ends), 0)
    new_a = jnp.concatenate([jnp.ones((1,), bool), ea[1:] != ea[:-1]])
    new_b = jnp.concatenate([jnp.ones((1,), bool), eb[1:] != eb[:-1]])
    flags = (active * FLAG_ACTIVE + (active & (idx == begin)) * FLAG_FIRST
             + (active & (idx == end - 1)) * FLAG_LAST + new_a * FLAG_NEW_A + new_b * FLAG_NEW_B)
    return src, tile, ea, eb, lo.astype(i32), hi.astype(i32), flags.astype(i32)


def _moe_sparse(x, gates, layer, wg, wu, wd, lng, lnb, *, tm):
    m = gates.shape[0]
    n_tiles = m // tm
    n_items = n_tiles + N_CLASSES - 1
    cls = gates[:, CLASS_LANE].astype(jnp.int32)
    tables = _moe_schedule(cls, tm=tm)
    kern = functools.partial(_moe_sparse_kernel, tm=tm, n_tiles=n_tiles, n_items=n_items)

    def w_spec(shape, which):
        return pl.BlockSpec((1, 1) + shape, lambda i, src, tile, ea, eb, lo, hi, fl: (layer, (ea, eb)[which][i], 0, 0))

    up, down = (D_MODEL, D_EXPERT), (D_EXPERT, D_MODEL)
    grid_spec = pltpu.PrefetchScalarGridSpec(
        num_scalar_prefetch=len(tables),
        grid=(n_items,),
        in_specs=[
            pl.BlockSpec(memory_space=pl.ANY),
            w_spec(up, 0), w_spec(up, 0), w_spec(down, 0), w_spec(up, 1), w_spec(up, 1), w_spec(down, 1),
            _const_spec(lng.shape), _const_spec(lnb.shape),
        ],
        out_specs=pl.BlockSpec(memory_space=pl.ANY),
        scratch_shapes=[
            pltpu.VMEM((2, tm * WIDE_TILES, LANES), F32), pltpu.VMEM((2, tm, D_MODEL), F32),
            pltpu.VMEM((tm, D_MODEL), F32),
            pltpu.VMEM((2,) + up, BF16), pltpu.VMEM((2,) + up, BF16), pltpu.VMEM((2,) + down, BF16),
            pltpu.SemaphoreType.DMA((2,)), pltpu.SemaphoreType.DMA((2,)),
        ],
    )
    return pl.pallas_call(
        kern,
        grid_spec=grid_spec,
        out_shape=jax.ShapeDtypeStruct((m, D_MODEL), F32),
        compiler_params=_params("arbitrary"),
        name="moe_sparse",
    )(*tables, x, wg, wu, wd, wg, wu, wd, lng, lnb)


def _row(v):
    return v.reshape(1, -1)


def _to_time_major(s):
    return jnp.transpose(s, (1, 0, 2)).reshape(1, -1, s.shape[-1])


def _from_time_major(s, steps):
    return jnp.transpose(s.reshape(steps, DEC_BATCH, s.shape[-1]), (1, 0, 2))


def _kv_lane_view(cache):
    d, n = cache.shape[:2]
    halves = MEM_HEAD_DIM // LANES
    v = cache.reshape(d, n, N_MEM, MEM_HEADS, halves, LANES)
    return jnp.transpose(v, (0, 1, 2, 4, 3, 5)).reshape(d, n, KV_ROWS, LANES)


def _kv_from_lane_view(view):
    d, n = view.shape[:2]
    halves = MEM_HEAD_DIM // LANES
    v = view.reshape(d, n, N_MEM, halves, MEM_HEADS, LANES)
    return jnp.transpose(v, (0, 1, 2, 4, 3, 5)).reshape(d, n, N_MEM, MEM_HEADS, MEM_HEAD_DIM)


def kernel(x_prompt, x_sample, cache_mem_k, cache_mem_v, state_pool, state_conv_c, state_conv_d, mem_prompt,
           w_in_ab, ln_v_g, ln_v_b, w_spatial, b_spatial, w_pool, pool_scale, w_out_ab,
           w_in_cd, conv_c_w, conv_c_b, ln_c_g, ln_c_b, conv_d_w, w_out_cd,
           w_q, w_k, w_v, w_o, w_group, b_group, w_router, b_router, w_gate, w_up, w_down, ln_g, ln_b):
    mem_flat = mem_prompt.reshape(BATCH * N_MEM, D_MODEL)
    cache_k = _kv_lane_view(cache_mem_k)
    cache_v = _kv_lane_view(cache_mem_v)
    kp, vp, kp_view, vp_view = _kv_proj(mem_flat, w_k.astype(BF16), w_v.astype(BF16))

    w_in_ab, w_out_ab, w_pool = w_in_ab.astype(BF16), w_out_ab.astype(BF16), w_pool.astype(BF16)
    w_in_cd, w_out_cd = w_in_cd.astype(BF16), w_out_cd.astype(BF16)
    w_q, w_o = w_q.astype(BF16), w_o.astype(BF16)
    pad = ROUTER_LANES - N_GROUPS - N_EXPERTS
    w_route = jnp.pad(jnp.concatenate([w_group, w_router], axis=-1), ((0, 0), (0, 0), (0, pad)))
    b_route = jnp.pad(jnp.concatenate([b_group, b_router], axis=-1), ((0, 0), (0, pad)))[:, None, :]

    prompt = _Slab(x_prompt.reshape(PROMPT_ROWS, D_MODEL), 0, PROMPT_ROWS, 0, None)
    sample = _Slab(_to_time_major(x_sample)[0], 0, SAMPLE_ROWS, PROMPT_ROWS, None)

    def both(h):
        return (_Slab(h, 0, PROMPT_ROWS, 0, None), _Slab(h, PROMPT_ROWS, SAMPLE_ROWS, PROMPT_ROWS, None))

    pool_p, pool_s, chunk_v_s = [], [], []
    conv_c_p, conv_c_s, conv_d_p, conv_d_s = [], [], [], []

    for l in range(DEPTH):
        i = l // 2
        lng0, lnb0 = _row(ln_g[l, 0]), _row(ln_b[l, 0])
        if l % 2 == 0:
            bias = jnp.repeat(b_spatial[i].T, A_HEAD_DIM, axis=1)
            small = jnp.repeat(
                jnp.transpose(w_spatial[i][:, :DEC_SEQ, :DEC_SEQ], (1, 2, 0)).reshape(DEC_SEQ * DEC_SEQ, A_HEADS),
                A_HEAD_DIM, axis=1)
            common = (_Layer(w_in_ab, i), _row(ln_v_g[i]), _row(ln_v_b[i]))
            tail = (_Layer(w_pool, i), _row(pool_scale[i]), _Layer(w_out_ab, i), lng0, lnb0)
            h, hist_p = _mixer_ab(prompt, jnp.zeros((BATCH, POOL_HIST, D_B), F32), *common, w_spatial[i], bias,
                                  *tail, seqs=1, tt=512, pos0=0, with_v=False)
            h, v_rows, hist_s = _mixer_ab(sample._replace(dst=(h,)), _to_time_major(state_pool[i]), *common, small,
                                          bias[:DEC_SEQ], *tail, seqs=DEC_BATCH, tt=DEC_SEQ, pos0=PAST_LEN,
                                          with_v=True)
            pool_p.append(hist_p)
            pool_s.append(_from_time_major(hist_s, POOL_HIST))
            chunk_v_s.append(_from_time_major(v_rows, DEC_SEQ))
        else:
            wts = (_Layer(w_in_cd, i), conv_c_w[i], _row(conv_c_b[i]), _row(ln_c_g[i]), _row(ln_c_b[i]),
                   conv_d_w[i], _Layer(w_out_cd, i), lng0, lnb0)
            h, hc_p, hd_p = _mixer_cd(prompt, jnp.zeros((BATCH, CONV_C - 1, D_C), F32),
                                      jnp.zeros((BATCH, CONV_D - 1, D_D), F32), *wts, seqs=1, tt=512)
            h, hc_s, hd_s = _mixer_cd(sample._replace(dst=(h,)), _to_time_major(state_conv_c[i]),
                                      _to_time_major(state_conv_d[i]), *wts, seqs=DEC_BATCH, tt=DEC_SEQ)
            conv_c_p.append(hc_p)
            conv_d_p.append(hd_p)
            conv_c_s.append(_from_time_major(hc_s, CONV_C - 1))
            conv_d_s.append(_from_time_major(hd_s, CONV_D - 1))
        prompt, sample = both(h)

        att_w = (_Layer(w_q, l), _Layer(w_o, l), _Layer(w_route, l), _Layer(b_route, l),
                 _row(ln_g[l, 1]), _row(ln_b[l, 1]))
        wide_and_gates = _attn_prompt(prompt, kp, vp, l, *att_w, tq=1024)
        h_wide, gates = _attn_sample(sample._replace(dst=tuple(wide_and_gates)), cache_k, cache_v, l, *att_w,
                                     seq_blk=8)

        h = _moe_sparse(h_wide, gates, l, w_gate, w_up, w_down, _row(ln_g[l, 2]), _row(ln_b[l, 2]), tm=256)
        prompt, sample = both(h)

    y_prompt = h[:PROMPT_ROWS].reshape(BATCH, SEQ, D_MODEL)
    y_sample = _from_time_major(h[PROMPT_ROWS:], DEC_SEQ)
    return (y_prompt, y_sample, _kv_from_lane_view(kp_view), _kv_from_lane_view(vp_view), jnp.stack(pool_p),
            jnp.stack(conv_c_p), jnp.stack(conv_d_p), jnp.stack(chunk_v_s), jnp.stack(pool_s),
            jnp.stack(conv_c_s), jnp.stack(conv_d_s))
```

```python
import functools
from typing import Callable, NamedTuple, Optional

import jax
import jax.numpy as jnp
from jax import lax
from jax.experimental import pallas as pl
from jax.experimental.pallas import tpu as pltpu

D_MODEL = 1024
BATCH = 8
SEQ = 2048
DEPTH = 4
DEC_BATCH = 128
DEC_SEQ = 4
PAST_LEN = 16384

CHUNK = 128
A_HEADS = 4
D_A = D_MODEL // 2
A_HEAD_DIM = D_A // A_HEADS
POOL_WINDOWS = (2, 4, 8, 16)
B_GROUPS = len(POOL_WINDOWS)
D_B = D_MODEL // 2
B_GROUP_DIM = D_B // B_GROUPS
POOL_HIST = max(POOL_WINDOWS) - 1
D_C = D_MODEL // 2
CONV_C = 31
D_D = D_MODEL // 2
CONV_D = 3
N_MEM = 256
MEM_HEADS = 4
MEM_HEAD_DIM = D_MODEL // MEM_HEADS
N_GROUPS = 4
EXPERTS_PER_GROUP = 4
N_EXPERTS = N_GROUPS * EXPERTS_PER_GROUP
D_EXPERT = 512
DN_ALPHA = (2 * DEPTH) ** 0.25
LN_EPS = 1e-5
PROMPT_ROWS = BATCH * SEQ
SAMPLE_ROWS = DEC_BATCH * DEC_SEQ
TOTAL_ROWS = PROMPT_ROWS + SAMPLE_ROWS

LANES = 128
SUBLANES = 8
POOL_HIST_PAD = 16
CONV_C_HIST_PAD = 32
CONV_D_HIST_PAD = 8
ROUTER_LANES = 128
ROUTER_MASKED = -1e30
CLASS_LANE = N_GROUPS + N_EXPERTS
PAIRS = ((0, 1), (0, 2), (1, 2), (1, 3), (0, 3), (2, 3))
N_CLASSES = N_GROUPS * len(PAIRS)
D_WIDE = D_MODEL + ROUTER_LANES
WIDE_TILES = D_WIDE // LANES
KV_ROWS = N_MEM * MEM_HEADS * MEM_HEAD_DIM // LANES
ATT_ROWS = KV_ROWS // N_MEM * DEC_SEQ
VMEM_LIMIT_BYTES = 52 * 1024 * 1024

F32 = jnp.float32
BF16 = jnp.bfloat16


def _layer_norm(x, g, b):
    mu = jnp.mean(x, axis=-1, keepdims=True)
    xc = x - mu
    var = jnp.mean(xc * xc, axis=-1, keepdims=True)
    return xc * lax.rsqrt(var + LN_EPS) * g + b


def _dot(a, b):
    return jnp.dot(a.astype(BF16), b.astype(BF16), preferred_element_type=F32)


def _params(*semantics):
    return pltpu.CompilerParams(dimension_semantics=semantics, vmem_limit_bytes=VMEM_LIMIT_BYTES)


def _const_spec(shape):
    nd = len(shape)
    return pl.BlockSpec(shape, lambda *_: (0,) * nd)


class _Layer(NamedTuple):
    stack: jax.Array
    layer: int


def _param_spec(p):
    if isinstance(p, _Layer):
        nd = p.stack.ndim
        return pl.BlockSpec((1,) + p.stack.shape[1:], lambda *_: (p.layer,) + (0,) * (nd - 1))
    return _const_spec(p.shape)


def _param_arg(p):
    return p.stack if isinstance(p, _Layer) else p


def _state_blocks(state):
    return 1 if isinstance(state, _Layer) else state.shape[0]


def _state_spec(state, per_block_spec):
    return _param_spec(state) if isinstance(state, _Layer) else per_block_spec


class _Slab(NamedTuple):
    array: jax.Array
    row0: int
    n_rows: int
    out_row0: int
    dst: Optional[tuple]


class _SlabIO(NamedTuple):
    in_spec: pl.BlockSpec
    out_specs: list
    out_shapes: list
    alias_spec: list
    alias_arg: tuple
    aliases: dict
    adapt: Callable


def _slab_io(src, rows, outs, block_index, *, n_inputs):
    assert src.row0 % rows == 0 and src.out_row0 % rows == 0 and src.n_rows % rows == 0
    in0, out0 = src.row0 // rows, src.out_row0 // rows
    in_spec = pl.BlockSpec((rows, src.array.shape[1]), lambda *g: (in0 + block_index(*g), 0))
    out_specs, out_shapes = [], []
    for width, row_major_tiles in outs:
        pieces, lanes = (width // LANES, LANES) if row_major_tiles else (1, width)
        out_specs.append(pl.BlockSpec((rows * pieces, lanes), lambda *g: (out0 + block_index(*g), 0)))
        out_shapes.append(jax.ShapeDtypeStruct((TOTAL_ROWS * pieces, lanes), F32))
    if src.dst is None:
        return _SlabIO(in_spec, out_specs, out_shapes, [], (), {}, lambda kernel: kernel)
    n_dst = len(outs)
    assert len(src.dst) == n_dst

    def adapt(kernel):
        return lambda *refs: kernel(*refs[:n_inputs], *refs[n_inputs + n_dst:])

    return _SlabIO(in_spec, out_specs, out_shapes, [pl.BlockSpec(memory_space=pl.ANY)] * n_dst, tuple(src.dst),
                   {n_inputs + j: j for j in range(n_dst)}, adapt)


def _mixer_ab_kernel(x_ref, hist_ref, w_in_ref, lnv_g_ref, lnv_b_ref, ws_ref, bs_ref, wpool_ref,
                     pscale_ref, w_out_ref, lng_ref, lnb_ref, *refs, seqs, tt, pos0, with_v):
    if with_v:
        h_out_ref, v_out_ref, hist_out_ref, zbuf = refs
    else:
        h_out_ref, hist_out_ref, zbuf = refs
    t_idx = pl.program_id(1)
    rows = tt * seqs
    hp = POOL_HIST_PAD

    @pl.when(t_idx == 0)
    def _():
        zbuf[pl.ds(0, (hp - POOL_HIST) * seqs), :] = jnp.zeros(((hp - POOL_HIST) * seqs, D_B), F32)
        zbuf[pl.ds((hp - POOL_HIST) * seqs, POOL_HIST * seqs), :] = hist_ref[0]

    x = x_ref[...]
    h = _dot(x, w_in_ref[0])
    ua = jax.nn.gelu(h[:, :2 * D_A])
    u = ua[:, :D_A]
    v = _layer_norm(ua[:, D_A:], lnv_g_ref[...], lnv_b_ref[...])
    z = h[:, 2 * D_A:]
    if with_v:
        v_out_ref[0] = v
    zbuf[pl.ds(hp * seqs, rows), :] = z

    if seqs == 1:
        tri = (lax.broadcasted_iota(jnp.int32, (CHUNK, CHUNK), 0)
               >= lax.broadcasted_iota(jnp.int32, (CHUNK, CHUNK), 1))
        w_heads = [jnp.where(tri, ws_ref[hh], 0.0).astype(BF16) for hh in range(A_HEADS)]
        vb = v.astype(BF16)
        chunks = []
        for c in range(tt // CHUNK):
            heads = []
            for hh in range(A_HEADS):
                vc = vb[c * CHUNK:(c + 1) * CHUNK, hh * A_HEAD_DIM:(hh + 1) * A_HEAD_DIM]
                heads.append(jnp.dot(w_heads[hh], vc, preferred_element_type=F32))
            chunks.append(jnp.concatenate(heads, axis=-1) + bs_ref[...])
        mixed = jnp.concatenate(chunks, axis=0)
    else:
        parts = []
        for t in range(tt):
            acc = bs_ref[t:t + 1, :]
            for s in range(t + 1):
                acc = acc + ws_ref[t * tt + s:t * tt + s + 1, :] * v[s * seqs:(s + 1) * seqs, :]
            parts.append(acc)
        mixed = jnp.concatenate(parts, axis=0)
    a_out = u * mixed

    run = zbuf[...]
    width = 1
    outs = []
    for gi, w in enumerate(POOL_WINDOWS):
        lo, hi = gi * B_GROUP_DIM, (gi + 1) * B_GROUP_DIM
        while width < w:
            run = run + pltpu.roll(run, width * seqs, axis=0)
            width *= 2
        assert width == w
        acc = run[hp * seqs:, :B_GROUP_DIM]
        run = run[:, B_GROUP_DIM:]
        if pos0 + 1 >= w:
            cnt = float(w)
        else:
            assert seqs == 1
            pos = pos0 + t_idx * tt + lax.broadcasted_iota(jnp.int32, (rows, 1), 0)
            cnt = jnp.minimum(pos + 1, w).astype(F32)
        pooled = acc / cnt - z[:, lo:hi]
        outs.append(_dot(pooled, wpool_ref[0, gi]))
    b_out = jnp.concatenate(outs, axis=-1) * pscale_ref[...]

    y = _dot(jnp.concatenate([a_out, b_out], axis=-1), w_out_ref[0])
    h_out_ref[...] = _layer_norm(DN_ALPHA * x + y, lng_ref[...], lnb_ref[...])

    hist_out_ref[0] = zbuf[pl.ds((tt + hp - POOL_HIST) * seqs, POOL_HIST * seqs), :]
    zbuf[pl.ds(0, hp * seqs), :] = zbuf[pl.ds(tt * seqs, hp * seqs), :]


def _mixer_ab(src, hist, w_in, lnv_g, lnv_b, ws, bs, wpool, pscale, w_out, lng, lnb, *, seqs, tt, pos0, with_v):
    nb = _state_blocks(hist)
    rows = tt * seqs
    n_t = src.n_rows // (nb * rows)
    kern = functools.partial(_mixer_ab_kernel, seqs=seqs, tt=tt, pos0=pos0, with_v=with_v)
    weights = (w_in, lnv_g, lnv_b, ws, bs, wpool, pscale, w_out, lng, lnb)
    io = _slab_io(src, rows, [(D_MODEL, False)], lambda n, t: n * n_t + t, n_inputs=2 + len(weights))
    v_spec = pl.BlockSpec((1, rows, D_A), lambda n, t: (n, t, 0))
    hist_spec = pl.BlockSpec((1, POOL_HIST * seqs, D_B), lambda n, t: (n, 0, 0))
    v_shape = jax.ShapeDtypeStruct((nb, n_t * rows, D_A), F32)
    hist_shape = jax.ShapeDtypeStruct((nb, POOL_HIST * seqs, D_B), F32)
    return pl.pallas_call(
        io.adapt(kern),
        grid=(nb, n_t),
        in_specs=[io.in_spec, _state_spec(hist, hist_spec)] + [_param_spec(w) for w in weights] + io.alias_spec,
        out_specs=io.out_specs + ([v_spec, hist_spec] if with_v else [hist_spec]),
        out_shape=io.out_shapes + ([v_shape, hist_shape] if with_v else [hist_shape]),
        input_output_aliases=io.aliases,
        scratch_shapes=[pltpu.VMEM(((POOL_HIST_PAD + tt) * seqs, D_B), F32)],
        compiler_params=_params("parallel", "arbitrary"),
        name="mixer_ab",
    )(src.array, _param_arg(hist), *map(_param_arg, weights), *io.alias_arg)


def _shift_rows(taps, hist_pad):
    base = hist_pad - (taps - 1)
    return max((base + k) // SUBLANES * SUBLANES for k in range(taps) if (base + k) % SUBLANES)


def _dwconv(buf, w_ref, out, shifted, *, taps, hist_pad, seqs, rows, chunk):
    base = (hist_pad - (taps - 1)) * seqs
    if shifted is not None:
        span = rows + _shift_rows(taps, hist_pad)
        for b in range(1, SUBLANES):
            shifted[b - 1] = buf[pl.ds(b, span), :]

    def tap(k, r0, n, lanes):
        off = base + k * seqs
        phase = off % SUBLANES
        if shifted is None or phase == 0:
            return buf[pl.ds(off + r0, n), lanes]
        return shifted[phase - 1, pl.ds(off - phase + r0, n), lanes]

    if shifted is None:
        def full_width(i, carry):
            r0 = i * chunk if isinstance(i, int) else pl.multiple_of(i * chunk, chunk)
            acc = w_ref[0:1, :] * tap(0, r0, chunk, slice(None))
            for k in range(1, taps):
                acc = acc + w_ref[k:k + 1, :] * tap(k, r0, chunk, slice(None))
            out[pl.ds(r0, chunk), :] = acc
            return carry

        if seqs % SUBLANES == 0:
            lax.fori_loop(0, rows // chunk, full_width, 0)
        else:
            for i in range(rows // chunk):
                full_width(i, 0)
        return

    for lt in range(buf.shape[1] // LANES):
        lanes = slice(lt * LANES, (lt + 1) * LANES)
        wk = [jnp.broadcast_to(w_ref[k:k + 1, lanes], (SUBLANES, LANES)) for k in range(taps)]
        for rg in range(0, rows, SUBLANES):
            acc = wk[0] * tap(0, rg, SUBLANES, lanes)
            for k in range(1, taps):
                acc = acc + wk[k] * tap(k, rg, SUBLANES, lanes)
            out[pl.ds(rg, SUBLANES), lanes] = acc


def _mixer_cd_kernel(x_ref, hist_c_ref, hist_d_ref, w_in_ref, ccw_ref, ccb_ref, lncg_ref, lncb_ref,
                     cdw_ref, w_out_ref, lng_ref, lnb_ref,
                     h_out_ref, hist_c_out_ref, hist_d_out_ref, cbuf, dbuf, cc_scr, cd_scr, *maybe_shifted,
                     seqs, tt):
    shifted = maybe_shifted[0] if maybe_shifted else None
    t_idx = pl.program_id(1)
    rows = tt * seqs
    hc, hd = CONV_C_HIST_PAD, CONV_D_HIST_PAD
    nc, nd = CONV_C - 1, CONV_D - 1

    @pl.when(t_idx == 0)
    def _():
        cbuf[pl.ds((hc - nc) * seqs, nc * seqs), :] = hist_c_ref[0]
        dbuf[pl.ds((hd - nd) * seqs, nd * seqs), :] = hist_d_ref[0]

    x = x_ref[...]
    h = _dot(x, w_in_ref[0])
    glu = h[:, :D_C] * jax.nn.sigmoid(h[:, D_C:2 * D_C])
    o = 2 * D_C
    gate_b = h[:, o:o + D_D]
    gx = h[:, o + D_D:o + 2 * D_D] * h[:, o + 2 * D_D:]
    cbuf[pl.ds(hc * seqs, rows), :] = glu
    dbuf[pl.ds(hd * seqs, rows), :] = gx

    chunk = min(rows, 32)
    _dwconv(cbuf, ccw_ref, cc_scr, shifted, taps=CONV_C, hist_pad=hc, seqs=seqs, rows=rows, chunk=chunk)
    _dwconv(dbuf, cdw_ref, cd_scr, None, taps=CONV_D, hist_pad=hd, seqs=seqs, rows=rows, chunk=chunk)

    c_out = jax.nn.silu(_layer_norm(cc_scr[...] + ccb_ref[...], lncg_ref[...], lncb_ref[...]))
    d_out = gate_b * cd_scr[...]
    y = _dot(jnp.concatenate([c_out, d_out], axis=-1), w_out_ref[0])
    h_out_ref[...] = _layer_norm(DN_ALPHA * x + y, lng_ref[...], lnb_ref[...])

    hist_c_out_ref[0] = cbuf[pl.ds((tt + hc - nc) * seqs, nc * seqs), :]
    hist_d_out_ref[0] = dbuf[pl.ds((tt + hd - nd) * seqs, nd * seqs), :]
    cbuf[pl.ds(0, hc * seqs), :] = cbuf[pl.ds(tt * seqs, hc * seqs), :]
    dbuf[pl.ds(0, hd * seqs), :] = dbuf[pl.ds(tt * seqs, hd * seqs), :]


def _mixer_cd(src, hist_c, hist_d, w_in, ccw, ccb, lncg, lncb, cdw, w_out, lng, lnb, *, seqs, tt):
    nb = _state_blocks(hist_c)
    rows = tt * seqs
    n_t = src.n_rows // (nb * rows)
    nc, nd = CONV_C - 1, CONV_D - 1
    kern = functools.partial(_mixer_cd_kernel, seqs=seqs, tt=tt)
    weights = (w_in, ccw, ccb, lncg, lncb, cdw, w_out, lng, lnb)
    io = _slab_io(src, rows, [(D_MODEL, False)], lambda n, t: n * n_t + t, n_inputs=3 + len(weights))
    hist_c_spec = pl.BlockSpec((1, nc * seqs, D_C), lambda n, t: (n, 0, 0))
    hist_d_spec = pl.BlockSpec((1, nd * seqs, D_D), lambda n, t: (n, 0, 0))
    return pl.pallas_call(
        io.adapt(kern),
        grid=(nb, n_t),
        in_specs=[io.in_spec, _state_spec(hist_c, hist_c_spec), _state_spec(hist_d, hist_d_spec)]
        + [_param_spec(w) for w in weights] + io.alias_spec,
        out_specs=io.out_specs + [hist_c_spec, hist_d_spec],
        out_shape=io.out_shapes + [
            jax.ShapeDtypeStruct((nb, nc * seqs, D_C), F32),
            jax.ShapeDtypeStruct((nb, nd * seqs, D_D), F32),
        ],
        input_output_aliases=io.aliases,
        scratch_shapes=[
            pltpu.VMEM(((CONV_C_HIST_PAD + tt) * seqs, D_C), F32),
            pltpu.VMEM(((CONV_D_HIST_PAD + tt) * seqs, D_D), F32),
            pltpu.VMEM((rows, D_C), F32),
            pltpu.VMEM((rows, D_D), F32),
        ] + ([pltpu.VMEM((SUBLANES - 1, rows + _shift_rows(CONV_C, CONV_C_HIST_PAD), D_C), F32)]
             if seqs % SUBLANES else []),
        compiler_params=_params("parallel", "arbitrary"),
        name="mixer_cd",
    )(src.array, _param_arg(hist_c), _param_arg(hist_d), *map(_param_arg, weights), *io.alias_arg)


KV_SEQS_PER_STEP = 2


def _kv_proj_kernel(mem_ref, wk_ref, wv_ref, k_ref, v_ref, kview_ref, vview_ref):
    m = mem_ref[...].astype(BF16)
    halves = MEM_HEAD_DIM // LANES
    for w_ref, o_ref, view_ref in ((wk_ref, k_ref, kview_ref), (wv_ref, v_ref, vview_ref)):
        y = jnp.dot(m, w_ref[0], preferred_element_type=F32)
        o_ref[0] = y.astype(BF16)
        for s in range(KV_SEQS_PER_STEP):
            for hh in range(MEM_HEADS):
                for lt in range(halves):
                    col = hh * MEM_HEAD_DIM + lt * LANES
                    view_ref[0, s, pl.ds(lt * MEM_HEADS + hh, N_MEM, stride=halves * MEM_HEADS), :] = (
                        y[s * N_MEM:(s + 1) * N_MEM, col:col + LANES])


def _kv_proj(mem, wk, wv):
    rows = jax.ShapeDtypeStruct((DEPTH, BATCH * N_MEM, D_MODEL), BF16)
    view = jax.ShapeDtypeStruct((DEPTH, BATCH, KV_ROWS, LANES), F32)
    w_spec = pl.BlockSpec((1, D_MODEL, D_MODEL), lambda l, n: (l, 0, 0))
    per = KV_SEQS_PER_STEP
    return pl.pallas_call(
        _kv_proj_kernel,
        grid=(DEPTH, BATCH // per),
        in_specs=[pl.BlockSpec((per * N_MEM, D_MODEL), lambda l, n: (n, 0)), w_spec, w_spec],
        out_specs=[pl.BlockSpec((1, per * N_MEM, D_MODEL), lambda l, n: (l, n, 0))] * 2
        + [pl.BlockSpec((1, per, KV_ROWS, LANES), lambda l, n: (l, n, 0, 0))] * 2,
        out_shape=[rows, rows, view, view],
        compiler_params=_params("parallel", "arbitrary"),
        name="kv_proj",
    )(mem, wk, wv)


def _attend(q, k, v):
    scale = MEM_HEAD_DIM ** -0.5
    qb = q.astype(BF16)
    outs = []
    for hh in range(MEM_HEADS):
        lo, hi = hh * MEM_HEAD_DIM, (hh + 1) * MEM_HEAD_DIM
        s = lax.dot_general(qb[:, lo:hi], k[:, lo:hi], (((1,), (1,)), ((), ())),
                            preferred_element_type=F32) * scale
        s = s - jnp.max(s, axis=-1, keepdims=True)
        p = jnp.exp(s)
        p = p / jnp.sum(p, axis=-1, keepdims=True)
        outs.append(jnp.dot(p.astype(BF16), v[:, lo:hi], preferred_element_type=F32))
    return jnp.concatenate(outs, axis=-1)


def _attn_prompt_kernel(x_ref, k_ref, v_ref, wq_ref, wo_ref, wr_ref, br_ref, lng_ref, lnb_ref, o_ref, g_ref):
    x = x_ref[...]
    q = _dot(x, wq_ref[0])
    o = _attend(q, k_ref[0], v_ref[0])
    y = _dot(o, wo_ref[0])
    h2 = _layer_norm(DN_ALPHA * x + y, lng_ref[...], lnb_ref[...])
    _store_wide_rows(o_ref, g_ref, h2, _route(_router_logits(h2, wr_ref[0]) + br_ref[0]))


def _store_wide_rows(o_ref, g_ref, h2, gates):
    rows = h2.shape[0]
    for c in range(WIDE_TILES):
        piece = gates if c == WIDE_TILES - 1 else h2[:, c * LANES:(c + 1) * LANES]
        o_ref[pl.ds(c, rows, stride=WIDE_TILES), :] = piece
    g_ref[...] = gates


ATTN_OUTS = [(D_WIDE, True), (ROUTER_LANES, False)]


def _attn_prompt(src, k, v, layer, wq, wo, wr, br, lng, lnb, *, tq):
    n_t = src.n_rows // (BATCH * tq)
    weights = (wq, wo, wr, br, lng, lnb)
    io = _slab_io(src, tq, ATTN_OUTS, lambda n, i: n * n_t + i, n_inputs=3 + len(weights))
    kv_spec = pl.BlockSpec((1, N_MEM, D_MODEL), lambda n, i: (layer, n, 0))
    return pl.pallas_call(
        io.adapt(_attn_prompt_kernel),
        grid=(BATCH, n_t),
        in_specs=[io.in_spec, kv_spec, kv_spec] + [_param_spec(w) for w in weights] + io.alias_spec,
        out_specs=io.out_specs,
        out_shape=io.out_shapes,
        input_output_aliases=io.aliases,
        compiler_params=_params("parallel", "arbitrary"),
        name="attn_prompt",
    )(src.array, k, v, *map(_param_arg, weights), *io.alias_arg)


def _att_row(half, head, t):
    return (half * MEM_HEADS + head) * DEC_SEQ + t


def _attn_sample_kernel(x_ref, k_ref, v_ref, wq_ref, wo_ref, wr_ref, br_ref, lng_ref, lnb_ref, o_ref, g_ref,
                        q_scr, qp_scr, op_scr, *, seq_blk, n_blk):
    i = pl.program_id(0)
    halves = MEM_HEAD_DIM // LANES
    blocks = [(t, hh, lt) for t in range(DEC_SEQ) for hh in range(MEM_HEADS) for lt in range(halves)]

    @pl.when(i == 0)
    def _():
        q_scr[...] = _dot(x_ref[...], wq_ref[0])
        for t, hh, lt in blocks:
            col = hh * MEM_HEAD_DIM + lt * LANES
            qp_scr[pl.ds(_att_row(lt, hh, t), DEC_BATCH, stride=ATT_ROWS), :] = (
                q_scr[t * DEC_BATCH:(t + 1) * DEC_BATCH, col:col + LANES])

    half_rows = ATT_ROWS // halves
    col = lax.broadcasted_iota(jnp.int32, (half_rows, KV_ROWS), 1)
    row_head = lax.shift_right_logical(lax.broadcasted_iota(jnp.int32, (half_rows, KV_ROWS), 0),
                                       DEC_SEQ.bit_length() - 1)
    col_cls = col & (halves * MEM_HEADS - 1)
    match0 = col_cls == row_head
    match1 = col_cls == row_head + MEM_HEADS
    scale = MEM_HEAD_DIM ** -0.5

    def body(j, carry):
        n = i * seq_blk + j
        r0 = pl.multiple_of(n * ATT_ROWS, ATT_ROWS)
        qp = qp_scr[pl.ds(r0, ATT_ROWS), :].astype(BF16)
        s = lax.dot_general(qp, k_ref[0, j].astype(BF16), (((1,), (1,)), ((), ())),
                            preferred_element_type=F32)
        part = jnp.where(match0, s[:half_rows], 0.0) + pltpu.roll(
            jnp.where(match1, s[half_rows:], 0.0), KV_ROWS - MEM_HEADS, axis=1)
        sv = jnp.where(match0, part * scale, ROUTER_MASKED)
        e = jnp.exp(sv - jnp.max(sv, axis=-1, keepdims=True))
        p = e / jnp.sum(e, axis=-1, keepdims=True)
        pp = jnp.concatenate([p, pltpu.roll(p, MEM_HEADS, axis=1)], axis=0).astype(BF16)
        op_scr[pl.ds(r0, ATT_ROWS), :] = jnp.dot(pp, v_ref[0, j].astype(BF16), preferred_element_type=F32)
        return carry

    lax.fori_loop(0, seq_blk, body, 0, unroll=True)

    @pl.when(i == n_blk - 1)
    def _():
        for t, hh, lt in blocks:
            col = hh * MEM_HEAD_DIM + lt * LANES
            q_scr[t * DEC_BATCH:(t + 1) * DEC_BATCH, col:col + LANES] = (
                op_scr[pl.ds(_att_row(lt, hh, t), DEC_BATCH, stride=ATT_ROWS), :])
        x = x_ref[...]
        y = _dot(q_scr[...], wo_ref[0])
        h2 = _layer_norm(DN_ALPHA * x + y, lng_ref[...], lnb_ref[...])
        _store_wide_rows(o_ref, g_ref, h2, _route(_router_logits(h2, wr_ref[0]) + br_ref[0]))


def _attn_sample(src, k, v, layer, wq, wo, wr, br, lng, lnb, *, seq_blk):
    m = src.n_rows
    n_blk = DEC_BATCH // seq_blk
    kern = functools.partial(_attn_sample_kernel, seq_blk=seq_blk, n_blk=n_blk)
    weights = (wq, wo, wr, br, lng, lnb)
    io = _slab_io(src, m, ATTN_OUTS, lambda i: 0, n_inputs=3 + len(weights))
    kv_spec = pl.BlockSpec((1, seq_blk, KV_ROWS, LANES), lambda i: (layer, i, 0, 0))
    return pl.pallas_call(
        io.adapt(kern),
        grid=(n_blk,),
        in_specs=[io.in_spec, kv_spec, kv_spec] + [_param_spec(w) for w in weights] + io.alias_spec,
        out_specs=io.out_specs,
        out_shape=io.out_shapes,
        input_output_aliases=io.aliases,
        scratch_shapes=[pltpu.VMEM((m, D_MODEL), F32),
                        pltpu.VMEM((DEC_BATCH * ATT_ROWS, LANES), F32),
                        pltpu.VMEM((DEC_BATCH * ATT_ROWS, LANES), F32)],
        compiler_params=_params("arbitrary"),
        name="attn_sample",
    )(src.array, k, v, *map(_param_arg, weights), *io.alias_arg)


def _split_bf16(x):
    hi = x.astype(BF16)
    return hi, (x - hi.astype(F32)).astype(BF16)


def _router_logits(h, w):
    h_hi, h_lo = _split_bf16(h)
    w_hi, w_lo = _split_bf16(w)
    a = jnp.dot(h_hi, jnp.concatenate([w_hi, w_lo], axis=-1), preferred_element_type=F32)
    b = jnp.dot(h_lo, w_hi, preferred_element_type=F32)
    return a[:, :ROUTER_LANES] + a[:, ROUTER_LANES:] + b


def _route(logits):
    lane = lax.broadcasted_iota(jnp.int32, logits.shape, 1)
    lane_f = lane.astype(F32)
    neg = ROUTER_MASKED
    is_grp = lane < N_GROUPS
    gl = jnp.where(is_grp, logits, neg)
    gmax = jnp.max(gl, axis=-1, keepdims=True)
    gsel = jnp.min(jnp.where(gl == gmax, lane_f, float(ROUTER_LANES)), axis=-1, keepdims=True)
    g_w = 1.0 / jnp.sum(jnp.exp(gl - gmax), axis=-1, keepdims=True)
    e_idx = lane - N_GROUPS
    e_grp = lax.shift_right_arithmetic(e_idx, 2).astype(F32)
    in_grp = (e_idx >= 0) & (e_idx < N_EXPERTS) & (e_grp == gsel)
    el = jnp.where(in_grp, logits, neg)
    m1 = jnp.max(el, axis=-1, keepdims=True)
    i1 = jnp.min(jnp.where(el == m1, lane_f, float(ROUTER_LANES)), axis=-1, keepdims=True)
    el2 = jnp.where(lane_f == i1, neg, el)
    m2 = jnp.max(el2, axis=-1, keepdims=True)
    i2 = jnp.min(jnp.where(el2 == m2, lane_f, float(ROUTER_LANES)), axis=-1, keepdims=True)
    e2 = jnp.exp(m2 - m1)
    den = 1.0 + e2
    w1 = (1.0 / den) * g_w
    w2 = (e2 / den) * g_w
    first_lane = N_GROUPS + EXPERTS_PER_GROUP * gsel
    a = jnp.minimum(i1, i2) - first_lane
    b = jnp.maximum(i1, i2) - first_lane
    cls = gsel * len(PAIRS)
    for j, (pa, pb) in enumerate(PAIRS):
        cls = cls + jnp.where((a == pa) & (b == pb), float(j), 0.0)
    return (jnp.where(lane_f == i1, w1, 0.0) + jnp.where(lane_f == i2, w2, 0.0)
            + jnp.where(lane == CLASS_LANE, cls, 0.0))


def _gate_column(gates, expert):
    lane = lax.broadcasted_iota(jnp.int32, gates.shape, 1)
    return jnp.sum(jnp.where(lane == expert + N_GROUPS, gates, 0.0), axis=-1, keepdims=True)


def _expert_hidden(xb, wg, wu, gate):
    hg = jnp.dot(xb, wg, preferred_element_type=F32)
    hu = jnp.dot(xb, wu, preferred_element_type=F32)
    return (jax.nn.silu(hg) * hu * gate).astype(BF16)


FLAG_ACTIVE, FLAG_FIRST, FLAG_LAST, FLAG_NEW_A, FLAG_NEW_B = 1, 2, 4, 8, 16


def _moe_sparse_kernel(src_ref, tile_ref, ea_ref, eb_ref, lo_ref, hi_ref, flag_ref,
                       x_hbm, wga_ref, wua_ref, wda_ref, wgb_ref, wub_ref, wdb_ref, lng_ref, lnb_ref,
                       out_hbm, xbuf, obuf, acc, wg_s, wu_s, wd_s, gsem, ssem, *, tm, n_tiles, n_items):
    i = pl.program_id(0)
    t = tile_ref[i]
    slot = t % 2
    flags = flag_ref[i]

    def start_gather(tile, s):
        for r in range(tm):
            tok = src_ref[tile * tm + r]
            pltpu.make_async_copy(x_hbm.at[pl.ds(tok * WIDE_TILES, WIDE_TILES)],
                                  xbuf.at[s, pl.ds(r * WIDE_TILES, WIDE_TILES)], gsem.at[s]).start()

    def piece(c):
        return xbuf[slot, pl.ds(c, tm, stride=WIDE_TILES), :]

    def activations():
        return jnp.concatenate([piece(c) for c in range(D_MODEL // LANES)], axis=-1)

    def start_scatter(tile, s):
        for r in range(tm):
            tok = src_ref[tile * tm + r]
            pltpu.make_async_copy(obuf.at[s, pl.ds(r, 1)], out_hbm.at[pl.ds(tok, 1)],
                                  ssem.at[s]).start(priority=r % 2)

    def for_slot(value, fn):
        for s in (0, 1):
            pl.when(value == s)(functools.partial(fn, s))

    def wait_gather(s):
        pltpu.make_async_copy(x_hbm.at[pl.ds(0, tm * WIDE_TILES)], xbuf.at[s], gsem.at[s]).wait()

    def wait_scatter(s):
        pltpu.make_async_copy(obuf.at[s], out_hbm.at[pl.ds(0, tm)], ssem.at[s]).wait()

    @pl.when(i == 0)
    def _():
        start_gather(0, 0)

    @pl.when((flags & FLAG_FIRST) != 0)
    def _():
        @pl.when(t + 1 < n_tiles)
        def _():
            for_slot(1 - slot, lambda s: start_gather(t + 1, s))
        wait_gather(slot)
        acc[...] = jnp.zeros_like(acc)

    @pl.when((flags & FLAG_NEW_A) != 0)
    def _():
        wg_s[0] = wga_ref[0, 0].astype(BF16)
        wu_s[0] = wua_ref[0, 0].astype(BF16)
        wd_s[0] = wda_ref[0, 0].astype(BF16)

    @pl.when((flags & FLAG_NEW_B) != 0)
    def _():
        wg_s[1] = wgb_ref[0, 0].astype(BF16)
        wu_s[1] = wub_ref[0, 0].astype(BF16)
        wd_s[1] = wdb_ref[0, 0].astype(BF16)

    @pl.when((flags & FLAG_ACTIVE) != 0)
    def _():
        xb = activations().astype(BF16)
        gates = piece(WIDE_TILES - 1)
        row = t * tm + lax.broadcasted_iota(jnp.int32, (tm, 1), 0)
        in_class = (row >= lo_ref[i]) & (row < hi_ref[i])
        gate_a = jnp.where(in_class, _gate_column(gates, ea_ref[i]), 0.0)
        gate_b = jnp.where(in_class, _gate_column(gates, eb_ref[i]), 0.0)
        hid_a = _expert_hidden(xb, wg_s[0], wu_s[0], gate_a)
        hid_b = _expert_hidden(xb, wg_s[1], wu_s[1], gate_b)
        acc[...] += (jnp.dot(hid_a, wd_s[0], preferred_element_type=F32)
                     + jnp.dot(hid_b, wd_s[1], preferred_element_type=F32))

    @pl.when((flags & FLAG_LAST) != 0)
    def _():
        @pl.when(t >= 2)
        def _():
            wait_scatter(slot)
        obuf[slot] = _layer_norm(DN_ALPHA * activations() + acc[...], lng_ref[...], lnb_ref[...])
        for_slot(slot, lambda s: start_scatter(t, s))

    @pl.when(i == n_items - 1)
    def _():
        wait_scatter(0)
        wait_scatter(1)


def _moe_schedule(cls, *, tm):
    m = cls.shape[0]
    n_tiles = m // tm
    n_items = n_tiles + N_CLASSES - 1
    i32 = jnp.int32
    src = jnp.argsort(cls, stable=True).astype(i32)
    counts = jnp.sum((cls[:, None] == jnp.arange(N_CLASSES, dtype=i32)[None, :]).astype(i32), axis=0)
    ends = jnp.cumsum(counts)
    starts = ends - counts
    def count(mask):
        return jnp.sum(mask.astype(i32), axis=-1)

    def pick(onehot, values):
        return jnp.sum(jnp.where(onehot, values[None, :], 0), axis=-1)

    tile_lo = jnp.arange(n_tiles, dtype=i32) * tm
    first_c = count(ends[None, :] <= tile_lo[:, None])
    last_c = count(starts[None, :] < (tile_lo + tm)[:, None]) - 1
    per_tile = last_c - first_c + 1
    item_end = jnp.cumsum(per_tile)
    idx = jnp.arange(n_items, dtype=i32)
    tile = jnp.minimum(count(item_end[None, :] <= idx[:, None]), n_tiles - 1)
    in_tile = tile[:, None] == jnp.arange(n_tiles, dtype=i32)[None, :]
    end = pick(in_tile, item_end)
    begin = end - pick(in_tile, per_tile)
    active = idx < item_end[-1]
    c = jnp.where(active, pick(in_tile, first_c) + idx - begin, last_c[n_tiles - 1])
    in_class = c[:, None] == jnp.arange(N_CLASSES, dtype=i32)[None, :]
    class_a = [g * EXPERTS_PER_GROUP + pa for g in range(N_GROUPS) for pa, _ in PAIRS]
    class_b = [g * EXPERTS_PER_GROUP + pb for g in range(N_GROUPS) for _, pb in PAIRS]
    ea = pick(in_class, jnp.asarray(class_a, dtype=i32))
    eb = pick(in_class, jnp.asarray(class_b, dtype=i32))
    lo = jnp.where(active, pick(in_class, starts), 0)
    hi = jnp.where(active, pick(in_class, ends), 0)
    new_a = jnp.concatenate([jnp.ones((1,), bool), ea[1:] != ea[:-1]])
    new_b = jnp.concatenate([jnp.ones((1,), bool), eb[1:] != eb[:-1]])
    flags = (active * FLAG_ACTIVE + (active & (idx == begin)) * FLAG_FIRST
             + (active & (idx == end - 1)) * FLAG_LAST + new_a * FLAG_NEW_A + new_b * FLAG_NEW_B)
    return src, tile, ea, eb, lo.astype(i32), hi.astype(i32), flags.astype(i32)


def _moe_sparse(x, gates, layer, wg, wu, wd, lng, lnb, *, tm):
    m = gates.shape[0]
    n_tiles = m // tm
    n_items = n_tiles + N_CLASSES - 1
    cls = gates[:, CLASS_LANE].astype(jnp.int32)
    tables = _moe_schedule(cls, tm=tm)
    kern = functools.partial(_moe_sparse_kernel, tm=tm, n_tiles=n_tiles, n_items=n_items)

    def w_spec(shape, which):
        return pl.BlockSpec((1, 1) + shape, lambda i, src, tile, ea, eb, lo, hi, fl: (layer, (ea, eb)[which][i], 0, 0))

    up, down = (D_MODEL, D_EXPERT), (D_EXPERT, D_MODEL)
    grid_spec = pltpu.PrefetchScalarGridSpec(
        num_scalar_prefetch=len(tables),
        grid=(n_items,),
        in_specs=[
            pl.BlockSpec(memory_space=pl.ANY),
            w_spec(up, 0), w_spec(up, 0), w_spec(down, 0), w_spec(up, 1), w_spec(up, 1), w_spec(down, 1),
            _const_spec(lng.shape), _const_spec(lnb.shape),
        ],
        out_specs=pl.BlockSpec(memory_space=pl.ANY),
        scratch_shapes=[
            pltpu.VMEM((2, tm * WIDE_TILES, LANES), F32), pltpu.VMEM((2, tm, D_MODEL), F32),
            pltpu.VMEM((tm, D_MODEL), F32),
            pltpu.VMEM((2,) + up, BF16), pltpu.VMEM((2,) + up, BF16), pltpu.VMEM((2,) + down, BF16),
            pltpu.SemaphoreType.DMA((2,)), pltpu.SemaphoreType.DMA((2,)),
        ],
    )
    return pl.pallas_call(
        kern,
        grid_spec=grid_spec,
        out_shape=jax.ShapeDtypeStruct((m, D_MODEL), F32),
        compiler_params=_params("arbitrary"),
        name="moe_sparse",
    )(*tables, x, wg, wu, wd, wg, wu, wd, lng, lnb)


def _row(v):
    return v.reshape(1, -1)


def _to_time_major(s):
    return jnp.transpose(s, (1, 0, 2)).reshape(1, -1, s.shape[-1])


def _states_time_major(s):
    return jnp.transpose(s, (0, 2, 1, 3)).reshape(s.shape[0], -1, s.shape[-1])


def _from_time_major(s, steps):
    return jnp.transpose(s.reshape(steps, DEC_BATCH, s.shape[-1]), (1, 0, 2))


def _kv_lane_view(cache):
    d, n = cache.shape[:2]
    halves = MEM_HEAD_DIM // LANES
    v = cache.reshape(d, n, N_MEM, MEM_HEADS, halves, LANES)
    return jnp.transpose(v, (0, 1, 2, 4, 3, 5)).reshape(d, n, KV_ROWS, LANES)


def _kv_from_lane_view(view):
    d, n = view.shape[:2]
    halves = MEM_HEAD_DIM // LANES
    v = view.reshape(d, n, N_MEM, halves, MEM_HEADS, LANES)
    return jnp.transpose(v, (0, 1, 2, 4, 3, 5)).reshape(d, n, N_MEM, MEM_HEADS, MEM_HEAD_DIM)


def kernel(x_prompt, x_sample, cache_mem_k, cache_mem_v, state_pool, state_conv_c, state_conv_d, mem_prompt,
           w_in_ab, ln_v_g, ln_v_b, w_spatial, b_spatial, w_pool, pool_scale, w_out_ab,
           w_in_cd, conv_c_w, conv_c_b, ln_c_g, ln_c_b, conv_d_w, w_out_cd,
           w_q, w_k, w_v, w_o, w_group, b_group, w_router, b_router, w_gate, w_up, w_down, ln_g, ln_b):
    mem_flat = mem_prompt.reshape(BATCH * N_MEM, D_MODEL)
    cache_k = _kv_lane_view(cache_mem_k)
    cache_v = _kv_lane_view(cache_mem_v)
    kp, vp, kp_view, vp_view = _kv_proj(mem_flat, w_k.astype(BF16), w_v.astype(BF16))

    w_in_ab, w_out_ab, w_pool = w_in_ab.astype(BF16), w_out_ab.astype(BF16), w_pool.astype(BF16)
    w_in_cd, w_out_cd = w_in_cd.astype(BF16), w_out_cd.astype(BF16)
    w_q, w_o = w_q.astype(BF16), w_o.astype(BF16)
    pad = ROUTER_LANES - N_GROUPS - N_EXPERTS
    w_route = jnp.pad(jnp.concatenate([w_group, w_router], axis=-1), ((0, 0), (0, 0), (0, pad)))
    b_route = jnp.pad(jnp.concatenate([b_group, b_router], axis=-1), ((0, 0), (0, pad)))[:, None, :]

    state_pool, state_conv_c, state_conv_d = map(_states_time_major, (state_pool, state_conv_c, state_conv_d))

    prompt = _Slab(x_prompt.reshape(PROMPT_ROWS, D_MODEL), 0, PROMPT_ROWS, 0, None)
    sample = _Slab(_to_time_major(x_sample)[0], 0, SAMPLE_ROWS, PROMPT_ROWS, None)

    def both(h):
        return (_Slab(h, 0, PROMPT_ROWS, 0, None), _Slab(h, PROMPT_ROWS, SAMPLE_ROWS, PROMPT_ROWS, None))

    pool_p, pool_s, chunk_v_s = [], [], []
    conv_c_p, conv_c_s, conv_d_p, conv_d_s = [], [], [], []

    for l in range(DEPTH):
        i = l // 2
        lng0, lnb0 = _row(ln_g[l, 0]), _row(ln_b[l, 0])
        if l % 2 == 0:
            bias = jnp.repeat(b_spatial[i].T, A_HEAD_DIM, axis=1)
            small = jnp.repeat(
                jnp.transpose(w_spatial[i][:, :DEC_SEQ, :DEC_SEQ], (1, 2, 0)).reshape(DEC_SEQ * DEC_SEQ, A_HEADS),
                A_HEAD_DIM, axis=1)
            common = (_Layer(w_in_ab, i), _row(ln_v_g[i]), _row(ln_v_b[i]))
            tail = (_Layer(w_pool, i), _row(pool_scale[i]), _Layer(w_out_ab, i), lng0, lnb0)
            h, hist_p = _mixer_ab(prompt, jnp.zeros((BATCH, POOL_HIST, D_B), F32), *common, w_spatial[i], bias,
                                  *tail, seqs=1, tt=512, pos0=0, with_v=False)
            h, v_rows, hist_s = _mixer_ab(sample._replace(dst=(h,)), _Layer(state_pool, i), *common, small,
                                          bias[:DEC_SEQ], *tail, seqs=DEC_BATCH, tt=DEC_SEQ, pos0=PAST_LEN,
                                          with_v=True)
            pool_p.append(hist_p)
            pool_s.append(_from_time_major(hist_s, POOL_HIST))
            chunk_v_s.append(_from_time_major(v_rows, DEC_SEQ))
        else:
            wts = (_Layer(w_in_cd, i), conv_c_w[i], _row(conv_c_b[i]), _row(ln_c_g[i]), _row(ln_c_b[i]),
                   conv_d_w[i], _Layer(w_out_cd, i), lng0, lnb0)
            h, hc_p, hd_p = _mixer_cd(prompt, jnp.zeros((BATCH, CONV_C - 1, D_C), F32),
                                      jnp.zeros((BATCH, CONV_D - 1, D_D), F32), *wts, seqs=1, tt=512)
            h, hc_s, hd_s = _mixer_cd(sample._replace(dst=(h,)), _Layer(state_conv_c, i),
                                      _Layer(state_conv_d, i), *wts, seqs=DEC_BATCH, tt=DEC_SEQ)
            conv_c_p.append(hc_p)
            conv_d_p.append(hd_p)
            conv_c_s.append(_from_time_major(hc_s, CONV_C - 1))
            conv_d_s.append(_from_time_major(hd_s, CONV_D - 1))
        prompt, sample = both(h)

        att_w = (_Layer(w_q, l), _Layer(w_o, l), _Layer(w_route, l), _Layer(b_route, l),
                 _row(ln_g[l, 1]), _row(ln_b[l, 1]))
        wide_and_gates = _attn_prompt(prompt, kp, vp, l, *att_w, tq=1024)
        h_wide, gates = _attn_sample(sample._replace(dst=tuple(wide_and_gates)), cache_k, cache_v, l, *att_w,
                                     seq_blk=8)

        h = _moe_sparse(h_wide, gates, l, w_gate, w_up, w_down, _row(ln_g[l, 2]), _row(ln_b[l, 2]), tm=256)
        prompt, sample = both(h)

    y_prompt = h[:PROMPT_ROWS].reshape(BATCH, SEQ, D_MODEL)
    y_sample = _from_time_major(h[PROMPT_ROWS:], DEC_SEQ)
    return (y_prompt, y_sample, _kv_from_lane_view(kp_view), _kv_from_lane_view(vp_view), jnp.stack(pool_p),
            jnp.stack(conv_c_p), jnp.stack(conv_d_p), jnp.stack(chunk_v_s), jnp.stack(pool_s),
            jnp.stack(conv_c_s), jnp.stack(conv_d_s))
```
